```python
import math
import jax, jax.numpy as jnp
from jax import lax
import numpy as np

D_MODEL = 2048
BATCH = 1
SEQ = 8192
DEPTH = 2
DEC_BATCH = 128
DEC_SEQ = 1
PAST_LEN = 2048
PAGE_SIZE = 128

HEAD_DIM = 64
RW_HEADS = 8
RW_WIDTH = RW_HEADS * HEAD_DIM
DECAY_LORA = 64
AAA_LORA = 64
GATE_LORA = 128
RW_COLS = 3 * RW_WIDTH + DECAY_LORA + AAA_LORA + GATE_LORA
RW_SPLITS = [RW_WIDTH, 2 * RW_WIDTH, 3 * RW_WIDTH, 3 * RW_WIDTH + DECAY_LORA, 3 * RW_WIDTH + DECAY_LORA + AAA_LORA]
RW_GN_EPS = 64e-5
NSA_HEADS = 8
NSA_KV_HEADS = 2
NSA_REP = NSA_HEADS // NSA_KV_HEADS
NSA_WIDTH = NSA_HEADS * HEAD_DIM
NSA_KV_WIDTH = NSA_KV_HEADS * HEAD_DIM
NSA_COLS = NSA_WIDTH + 6 * NSA_KV_WIDTH + 3 * NSA_HEADS
NSA_SPLITS = [NSA_WIDTH + i * NSA_KV_WIDTH for i in range(7)]
CMP_BLOCK = 32
CMP_STRIDE = 16
SEL_BLOCK = 64
N_SELECT = 16
WINDOW = 512
SEL_FORCE = 1e4
FOX_HEADS = 16
FOX_WIDTH = FOX_HEADS * HEAD_DIM
FOX_COLS = 3 * FOX_WIDTH + FOX_HEADS
FOX_SPLITS = [FOX_WIDTH, 2 * FOX_WIDTH, 3 * FOX_WIDTH]
IN_COLS = RW_COLS + NSA_COLS + FOX_COLS
IN_SPLITS = [RW_COLS, RW_COLS + NSA_COLS]
MIX_WIDTH = RW_WIDTH + NSA_WIDTH + FOX_WIDTH
N_BUCKETS = 32
MAX_DISTANCE = 1024
D_FF = ((8 * D_MODEL // 3 + 255) // 256) * 256
Q_BLOCK = 128
DEEPNORM_ALPHA = (2 * DEPTH) ** 0.25
DEEPNORM_BETA = (8 * DEPTH) ** -0.25
LN_EPS = 1e-5
RMS_EPS = 1e-6
ATT_SCALE = HEAD_DIM ** -0.5

kernel_name = 'hybrid_rwkv7_nsa_fox_decode_step'


def layer_norm(x, g, b):
    xf = x.astype(jnp.float32)
    mu = jnp.mean(xf, -1, keepdims=True)
    var = jnp.mean(jnp.square(xf - mu), -1, keepdims=True)
    return ((xf - mu) * lax.rsqrt(var + LN_EPS) * g + b).astype(x.dtype)


def swiglu(x, w_gate, w_up, w_down):
    return (jax.nn.silu(x @ w_gate) * (x @ w_up)) @ w_down


def head_rmsnorm(o, g, n_heads):
    B, T, W = o.shape
    of = o.astype(jnp.float32).reshape(B, T, n_heads, W // n_heads)
    of = of * lax.rsqrt(jnp.mean(jnp.square(of), -1, keepdims=True) + RMS_EPS)
    return (of.reshape(B, T, W) * g).astype(o.dtype)


def masked_softmax(logits, mask):
    s = jnp.where(mask, logits, -jnp.inf)
    m = jnp.max(s, axis=-1, keepdims=True)
    m = jnp.where(jnp.isfinite(m), m, 0.0)
    e = jnp.where(mask, jnp.exp(s - m), 0.0)
    return e / jnp.maximum(jnp.sum(e, -1, keepdims=True), 1e-30)


def t5_bucket(dist):
    d = jnp.maximum(dist, 0)
    max_exact = N_BUCKETS // 2
    df = jnp.maximum(d, 1).astype(jnp.float32)
    large = max_exact + (jnp.log(df / max_exact) / math.log(MAX_DISTANCE / max_exact)
                         * (N_BUCKETS - max_exact)).astype(jnp.int32)
    large = jnp.minimum(large, N_BUCKETS - 1)
    return jnp.where(d < max_exact, d, large)


def rwkv_mix(p_rw, p_prev, s0, mu, w0, w2, a0, a2, g2, k_k, k_a, r_k, gn_g, gn_b):
    f32 = jnp.float32
    B, T = p_rw.shape[:2]
    shifted = jnp.concatenate([p_prev[:, None].astype(p_rw.dtype), p_rw[:, :-1]], axis=1)
    z = p_rw + (shifted - p_rw) * mu
    r, k, v, wd, ad, gd = jnp.split(z, RW_SPLITS, axis=-1)
    heads = lambda t: t.astype(f32).reshape(B, T, RW_HEADS, HEAD_DIM)
    w_log = -jax.nn.softplus(-(w0 + jnp.tanh(wd) @ w2).astype(f32)) - 0.5
    decay = heads(jnp.exp(-jnp.exp(w_log)))
    a = heads(jax.nn.sigmoid((a0 + ad @ a2).astype(f32)))
    g = jax.nn.sigmoid(gd) @ g2
    kk = heads(k * k_k)
    kk = kk / jnp.maximum(jnp.linalg.norm(kk, axis=-1, keepdims=True), 1e-12)
    kh = heads(k) * (1.0 + (a - 1.0) * k_a.astype(f32).reshape(RW_HEADS, HEAD_DIM))
    rh, vh = heads(r), heads(v)

    def step(S, inp):
        r_t, w_t, k_t, v_t, kk_t, a_t = inp
        sa = jnp.einsum('bhvk,bhk->bhv', S, -kk_t)
        S = (S * w_t[:, :, None, :] + sa[..., None] * (kk_t * a_t)[:, :, None, :]
             + v_t[..., None] * k_t[:, :, None, :])
        return S, jnp.einsum('bhvk,bhk->bhv', S, r_t)

    xs = tuple(jnp.swapaxes(t, 0, 1) for t in (rh, decay, kh, vh, kk, a))
    s_T, out = lax.scan(step, s0.astype(f32), xs)
    out = jnp.swapaxes(out, 0, 1)
    m = jnp.mean(out, -1, keepdims=True)
    var = jnp.mean(jnp.square(out - m), -1, keepdims=True)
    y = ((out - m) * lax.rsqrt(var + RW_GN_EPS)).reshape(B, T, RW_WIDTH) * gn_g + gn_b
    bonus = (jnp.sum(rh * kh * r_k, -1, keepdims=True) * vh).reshape(B, T, RW_WIDTH)
    y = (y + bonus) * g
    return y.astype(p_rw.dtype), s_T, p_rw[:, -1]


def nsa_split(p_nsa):
    B, T = p_nsa.shape[:2]
    q, kc, vc, ks, vs, kw, vw, gl = jnp.split(p_nsa, NSA_SPLITS, axis=-1)
    kvh = lambda t: t.reshape(B, T, NSA_KV_HEADS, HEAD_DIM)
    return (q.reshape(B, T, NSA_HEADS, HEAD_DIM), kvh(kc), kvh(vc), kvh(ks), kvh(vs),
            kvh(kw), kvh(vw), gl.reshape(B, T, NSA_HEADS, 3))


def nsa_compress(k, w, b):
    B, L = k.shape[:2]
    nc = (L - CMP_BLOCK) // CMP_STRIDE + 1
    ch = k[:, :(nc + 1) * CMP_STRIDE].reshape(B, nc + 1, CMP_STRIDE, NSA_KV_HEADS, HEAD_DIM)
    first = jnp.einsum('bcjgd,jde->bcge', ch, w[:CMP_STRIDE])
    second = jnp.einsum('bcjgd,jde->bcge', ch, w[CMP_STRIDE:])
    return first[:, :-1] + second[:, 1:] + b


def sel_blocks(k, ns):
    B, L = k.shape[:2]
    k = jnp.pad(k, ((0, 0), (0, ns * SEL_BLOCK - L), (0, 0), (0, 0)))
    return k.reshape(B, ns, SEL_BLOCK, NSA_KV_HEADS, HEAD_DIM).transpose(0, 3, 1, 2, 4)


def cover_matrix(nc, ns):
    start = jnp.arange(nc)[:, None] * CMP_STRIDE
    j = jnp.arange(ns)[None, :]
    return ((start < (j + 1) * SEL_BLOCK) & (start + CMP_BLOCK > j * SEL_BLOCK)).astype(jnp.float32)


def nsa_context(kc_raw, vc_raw, ks, vs, cmp_w, cmp_b):
    L = kc_raw.shape[1]
    kc = nsa_compress(kc_raw, cmp_w[0], cmp_b[0])
    vc = nsa_compress(vc_raw, cmp_w[1], cmp_b[1])
    nc = kc.shape[1]
    c_end = jnp.arange(nc) * CMP_STRIDE + CMP_BLOCK - 1
    ns = -(-L // SEL_BLOCK)
    return kc, vc, c_end, sel_blocks(ks, ns), sel_blocks(vs, ns), cover_matrix(nc, ns)


def nsa_attend(q, gates, tq, kc, vc, c_end, kb, vb, cover, kw, vw, pw, rel_bias):
    f32 = jnp.float32
    B, Tq = q.shape[:2]
    G, R = NSA_KV_HEADS, NSA_REP
    qg = q.reshape(B, Tq, G, R, HEAD_DIM)

    def bias_2d(dist):
        return rel_bias[t5_bucket(dist)].reshape(dist.shape + (G, R)).transpose(2, 3, 0, 1).astype(f32)

    dist_c = tq[:, None] - c_end[None, :]
    lc = jnp.einsum('btgrd,bigd->bgrti', qg, kc).astype(f32) * ATT_SCALE + bias_2d(dist_c)
    p_c = masked_softmax(lc, dist_c >= 0)
    o_c = jnp.einsum('bgrti,bigd->btgrd', p_c.astype(vc.dtype), vc)

    ns = kb.shape[2]
    imp = jnp.einsum('bgrti,ij->btgj', p_c, cover)
    jq = tq // SEL_BLOCK
    jb = jnp.arange(ns)[None, :]
    forced = (jb == 0) | (jb == jq[:, None]) | (jb == jq[:, None] - 1)
    allowed = jb <= jq[:, None]
    score = jnp.where(allowed[None, :, None, :],
                      imp + jnp.where(forced, SEL_FORCE, 0.0)[None, :, None, :], -jnp.inf)
    top_v, top_i = lax.top_k(score, min(N_SELECT, ns))
    top_i = top_i.transpose(0, 2, 1, 3)
    valid = jnp.isfinite(top_v).transpose(0, 2, 1, 3)
    n = top_i.shape[-1]
    bi = jnp.arange(B)[:, None, None, None]
    gi = jnp.arange(G)[None, :, None, None]
    ksel = kb[bi, gi, top_i]
    vsel = vb[bi, gi, top_i]
    pos = top_i[..., None] * SEL_BLOCK + jnp.arange(SEL_BLOCK)
    dist_s = tq[None, None, :, None, None] - pos
    table_g = rel_bias.reshape(N_BUCKETS, G, R).transpose(1, 0, 2)
    bias_s = jnp.moveaxis(table_g[gi[..., None], t5_bucket(dist_s)], -1, 2).astype(f32)
    ls = jnp.einsum('btgrd,bgtnkd->bgrtnk', qg, ksel).astype(f32) * ATT_SCALE + bias_s
    ok = ((dist_s >= 0) & valid[..., None])[:, :, None]
    p_s = masked_softmax(ls.reshape(B, G, R, Tq, n * SEL_BLOCK), ok.reshape(B, G, 1, Tq, n * SEL_BLOCK))
    o_s = jnp.einsum('bgrtm,bgtmd->btgrd', p_s.astype(vsel.dtype),
                     vsel.reshape(B, G, Tq, n * SEL_BLOCK, HEAD_DIM))

    dist_w = tq[:, None] - pw[None, :]
    lw = jnp.einsum('btgrd,bsgd->bgrts', qg, kw).astype(f32) * ATT_SCALE + bias_2d(dist_w)
    p_w = masked_softmax(lw, (dist_w >= 0) & (dist_w < WINDOW) & (pw >= 0)[None, :])
    o_w = jnp.einsum('bgrts,bsgd->btgrd', p_w.astype(vw.dtype), vw)

    gt = jax.nn.sigmoid(gates.astype(f32)).reshape(B, Tq, G, R, 3)
    o = gt[..., 0:1] * o_c + gt[..., 1:2] * o_s + gt[..., 2:3] * o_w
    return o.reshape(B, Tq, NSA_WIDTH).astype(q.dtype)


def nsa_prompt(p_nsa, cmp_w, cmp_b, rel_bias):
    q, kc_raw, vc_raw, ks, vs, kw, vw, gl = nsa_split(p_nsa)
    B, T = q.shape[:2]
    ctx = nsa_context(kc_raw, vc_raw, ks, vs, cmp_w, cmp_b)
    pad = ((0, 0), (WINDOW, 0), (0, 0), (0, 0))
    kw_pad, vw_pad = jnp.pad(kw, pad), jnp.pad(vw, pad)
    nqb = T // Q_BLOCK
    span = WINDOW + Q_BLOCK
    qb = q.reshape(B, nqb, Q_BLOCK, NSA_HEADS, HEAD_DIM).swapaxes(0, 1)
    gb = gl.reshape(B, nqb, Q_BLOCK, NSA_HEADS, 3).swapaxes(0, 1)

    def block(args):
        qi, gi, s0 = args
        kwb = lax.dynamic_slice_in_dim(kw_pad, s0, span, axis=1)
        vwb = lax.dynamic_slice_in_dim(vw_pad, s0, span, axis=1)
        return nsa_attend(qi, gi, s0 + jnp.arange(Q_BLOCK), *ctx, kwb, vwb,
                          s0 - WINDOW + jnp.arange(span), rel_bias)

    o = lax.map(block, (qb, gb, jnp.arange(nqb) * Q_BLOCK))
    o = o.swapaxes(0, 1).reshape(B, T, NSA_WIDTH)
    rows = jnp.stack([kc_raw, vc_raw, ks, vs], axis=2)
    win = jnp.stack([kw, vw], axis=2)[:, T - min(WINDOW, T):]
    return o, rows, win


def nsa_sample(p_nsa, past, win_buf, cmp_w, cmp_b, rel_bias):
    q, kc_new, vc_new, ks_new, vs_new, kw, vw, gl = nsa_split(p_nsa)
    S = q.shape[1]
    P = past.shape[1]
    wb = win_buf.shape[1]
    cat = lambda i, new: jnp.concatenate([past[:, :, i], new], axis=1)
    ctx = nsa_context(cat(0, kc_new), cat(1, vc_new), cat(2, ks_new), cat(3, vs_new), cmp_w, cmp_b)
    win_all = jnp.concatenate([win_buf, jnp.stack([kw, vw], axis=2)], axis=1)
    o = nsa_attend(q, gl, P + jnp.arange(S), *ctx, win_all[:, :, 0], win_all[:, :, 1],
                   P - wb + jnp.arange(wb + S), rel_bias)
    rows = jnp.stack([kc_new, vc_new, ks_new, vs_new], axis=2)
    return o, rows, win_all[:, S:]


def fox_project(p_fox, b_f):
    B, T = p_fox.shape[:2]
    q, k, v, f = jnp.split(p_fox, FOX_SPLITS, axis=-1)
    h = lambda t: t.reshape(B, T, FOX_HEADS, HEAD_DIM)
    logf = jax.nn.log_sigmoid(f.astype(jnp.float32) + b_f.astype(jnp.float32))
    return h(q), h(k), h(v), logf


def fox_prompt(p_fox, b_f):
    q, k, v, logf = fox_project(p_fox, b_f)
    B, T = q.shape[:2]
    c = jnp.cumsum(logf, axis=1).transpose(0, 2, 1)
    nqb = T // Q_BLOCK
    qb = q.reshape(B, nqb, Q_BLOCK, FOX_HEADS, HEAD_DIM).swapaxes(0, 1)
    cb = c.reshape(B, FOX_HEADS, nqb, Q_BLOCK).transpose(2, 0, 1, 3)
    ts = jnp.arange(T)

    def block(args):
        qi, ci, s0 = args
        tq = s0 + jnp.arange(Q_BLOCK)
        logits = (jnp.einsum('bthd,bshd->bhts', qi, k).astype(jnp.float32) * ATT_SCALE
                  + (ci[..., None] - c[:, :, None, :]))
        p = masked_softmax(logits, ts[None, :] <= tq[:, None])
        return jnp.einsum('bhts,bshd->bthd', p.astype(v.dtype), v).reshape(B, Q_BLOCK, FOX_WIDTH)

    o = lax.map(block, (qb, cb, jnp.arange(nqb) * Q_BLOCK))
    return o.swapaxes(0, 1).reshape(B, T, FOX_WIDTH), k, v, logf


def fox_sample(p_fox, b_f, k_past, v_past, logf_past):
    q, k, v, logf = fox_project(p_fox, b_f)
    B, S = q.shape[:2]
    P = k_past.shape[1]
    c_past = jnp.cumsum(logf_past.astype(jnp.float32), axis=1)
    c_new = c_past[:, -1:] + jnp.cumsum(logf, axis=1)
    cp, cn = c_past.transpose(0, 2, 1), c_new.transpose(0, 2, 1)
    l_past = (jnp.einsum('bthd,bshd->bhts', q, k_past).astype(jnp.float32) * ATT_SCALE
              + (cn[..., None] - cp[:, :, None, :]))
    l_new = (jnp.einsum('bthd,bshd->bhts', q, k).astype(jnp.float32) * ATT_SCALE
             + (cn[..., None] - cn[:, :, None, :]))
    ar = jnp.arange(S)
    mask = jnp.concatenate([jnp.ones((S, P), bool), ar[None, :] <= ar[:, None]], axis=1)
    p = masked_softmax(jnp.concatenate([l_past, l_new], axis=-1), mask)
    o = (jnp.einsum('bhts,bshd->bthd', p[..., :P].astype(v.dtype), v_past)
         + jnp.einsum('bhts,bshd->bthd', p[..., P:].astype(v.dtype), v))
    return o.reshape(B, S, FOX_WIDTH), k, v, logf


def setup_inputs(seed: int = 0) -> dict:
    key = jax.random.key(seed)
    ks = iter(jax.random.split(key, 40))
    nrm = lambda shape, scale=1.0: scale * jax.random.normal(next(ks), shape, jnp.float32)
    n_pages = PAST_LEN // PAGE_SIZE
    n_used = DEC_BATCH * n_pages
    n_phys = n_used + max(1, n_used // 4)
    wb = min(WINDOW, PAST_LEN)
    page_table = jax.random.permutation(next(ks), n_phys)[:n_used].reshape(DEC_BATCH, n_pages).astype(jnp.int32)
    decay_speed = -6.0 + 5.0 * jnp.linspace(0.0, 1.0, RW_WIDTH) ** 0.85
    return {
        'x_prompt': nrm((BATCH, SEQ, D_MODEL)),
        'x_sample': nrm((DEC_BATCH, DEC_SEQ, D_MODEL)),
        'cache_nsa_kv': nrm((DEPTH, n_phys, PAGE_SIZE, 4, NSA_KV_HEADS, HEAD_DIM)),
        'cache_fox_k': nrm((DEPTH, n_phys, PAGE_SIZE, FOX_HEADS, HEAD_DIM)),
        'cache_fox_v': nrm((DEPTH, n_phys, PAGE_SIZE, FOX_HEADS, HEAD_DIM)),
        'cache_fox_logf': jax.nn.log_sigmoid(4.0 + nrm((DEPTH, n_phys, PAGE_SIZE, FOX_HEADS))),
        'state_nsa_win': nrm((DEPTH, DEC_BATCH, wb, 2, NSA_KV_HEADS, HEAD_DIM)),
        'state_rwkv_wkv': nrm((DEPTH, DEC_BATCH, RW_HEADS, HEAD_DIM, HEAD_DIM)),
        'state_rwkv_shift': nrm((DEPTH, DEC_BATCH, RW_COLS)),
        'page_table': page_table,
        'ln_g': 1.0 + nrm((DEPTH, 3, D_MODEL), 0.02),
        'ln_b': nrm((DEPTH, 3, D_MODEL), 0.02),
        'ffn_w_gate': nrm((DEPTH, 2, D_MODEL, D_FF), D_MODEL ** -0.5),
        'ffn_w_up': nrm((DEPTH, 2, D_MODEL, D_FF), D_MODEL ** -0.5),
        'ffn_w_down': nrm((DEPTH, 2, D_FF, D_MODEL), DEEPNORM_BETA * D_FF ** -0.5),
        'w_in': nrm((DEPTH, D_MODEL, IN_COLS), D_MODEL ** -0.5),
        'w_out': nrm((DEPTH, MIX_WIDTH, D_MODEL), DEEPNORM_BETA * MIX_WIDTH ** -0.5),
        'rw_mu': jax.random.uniform(next(ks), (DEPTH, RW_COLS), jnp.float32),
        'rw_w0': decay_speed[None, :] + nrm((DEPTH, RW_WIDTH), 0.1),
        'rw_w2': nrm((DEPTH, DECAY_LORA, RW_WIDTH), 0.1 * DECAY_LORA ** -0.5),
        'rw_a0': nrm((DEPTH, RW_WIDTH), 0.1),
        'rw_a2': nrm((DEPTH, AAA_LORA, RW_WIDTH), AAA_LORA ** -0.5),
        'rw_g2': nrm((DEPTH, GATE_LORA, RW_WIDTH), GATE_LORA ** -0.5),
        'rw_k_k': 0.85 + nrm((DEPTH, RW_WIDTH), 0.05),
        'rw_k_a': 1.0 + nrm((DEPTH, RW_WIDTH), 0.05),
        'rw_r_k': nrm((DEPTH, RW_HEADS, HEAD_DIM), 0.1),
        'rw_gn_g': 1.0 + nrm((DEPTH, RW_WIDTH), 0.02),
        'rw_gn_b': nrm((DEPTH, RW_WIDTH), 0.02),
        'nsa_cmp_w': nrm((DEPTH, 2, CMP_BLOCK, HEAD_DIM, HEAD_DIM), (CMP_BLOCK * HEAD_DIM) ** -0.5),
        'nsa_cmp_b': nrm((DEPTH, 2, HEAD_DIM), 0.02),
        'nsa_out_g': 1.0 + nrm((DEPTH, NSA_WIDTH), 0.02),
        'rel_bias': nrm((N_BUCKETS, NSA_HEADS), 0.5),
        'fox_b_f': jnp.linspace(1.0, 6.0, FOX_HEADS)[None, :] + nrm((DEPTH, FOX_HEADS), 0.1),
        'fox_out_g': 1.0 + nrm((DEPTH, FOX_WIDTH), 0.02),
    }


def reference(x_prompt, x_sample, cache_nsa_kv, cache_fox_k, cache_fox_v, cache_fox_logf,
              state_nsa_win, state_rwkv_wkv, state_rwkv_shift, page_table,
              ln_g, ln_b, ffn_w_gate, ffn_w_up, ffn_w_down, w_in, w_out,
              rw_mu, rw_w0, rw_w2, rw_a0, rw_a2, rw_g2, rw_k_k, rw_k_a, rw_r_k, rw_gn_g, rw_gn_b,
              nsa_cmp_w, nsa_cmp_b, nsa_out_g, rel_bias, fox_b_f, fox_out_g):
    db = x_sample.shape[0]
    past_len = page_table.shape[1] * cache_nsa_kv.shape[2]

    def ffn_sub(x, l, j):
        h = swiglu(x, ffn_w_gate[l, j], ffn_w_up[l, j], ffn_w_down[l, j])
        return layer_norm(DEEPNORM_ALPHA * x + 0.5 * h, ln_g[l, 2 * j], ln_b[l, 2 * j])

    def mix_sub(x, l, o_rw, o_nsa, o_fox):
        o = jnp.concatenate([o_rw, head_rmsnorm(o_nsa, nsa_out_g[l], NSA_HEADS),
                             head_rmsnorm(o_fox, fox_out_g[l], FOX_HEADS)], axis=-1)
        return layer_norm(DEEPNORM_ALPHA * x + o @ w_out[l], ln_g[l, 1], ln_b[l, 1])

    def rw_params(l):
        return (rw_mu[l], rw_w0[l], rw_w2[l], rw_a0[l], rw_a2[l], rw_g2[l], rw_k_k[l], rw_k_a[l],
                rw_r_k[l], rw_gn_g[l], rw_gn_b[l])

    xp = x_prompt
    bp = xp.shape[0]
    p_kv, p_fk, p_fv, p_fl, p_win, p_wkv, p_shift = [], [], [], [], [], [], []
    for l in range(DEPTH):
        xp = ffn_sub(xp, l, 0)
        p_rw, p_nsa, p_fox = jnp.split(xp @ w_in[l], IN_SPLITS, axis=-1)
        o_rw, s_wkv, s_shift = rwkv_mix(p_rw, jnp.zeros((bp, RW_COLS), xp.dtype),
                                        jnp.zeros((bp, RW_HEADS, HEAD_DIM, HEAD_DIM), jnp.float32),
                                        *rw_params(l))
        o_nsa, kv_rows, win = nsa_prompt(p_nsa, nsa_cmp_w[l], nsa_cmp_b[l], rel_bias)
        o_fox, fk, fv, fl = fox_prompt(p_fox, fox_b_f[l])
        xp = mix_sub(xp, l, o_rw, o_nsa, o_fox)
        xp = ffn_sub(xp, l, 1)
        p_kv.append(kv_rows); p_fk.append(fk); p_fv.append(fv); p_fl.append(fl)
        p_win.append(win); p_wkv.append(s_wkv); p_shift.append(s_shift)

    xs = x_sample
    s_kv, s_fk, s_fv, s_fl, s_win, s_wkv_l, s_shift_l = [], [], [], [], [], [], []
    for l in range(DEPTH):
        xs = ffn_sub(xs, l, 0)
        p_rw, p_nsa, p_fox = jnp.split(xs @ w_in[l], IN_SPLITS, axis=-1)
        o_rw, s_wkv, s_shift = rwkv_mix(p_rw, state_rwkv_shift[l], state_rwkv_wkv[l], *rw_params(l))
        nsa_past = cache_nsa_kv[l, page_table].reshape(db, past_len, 4, NSA_KV_HEADS, HEAD_DIM)
        o_nsa, kv_rows, win = nsa_sample(p_nsa, nsa_past, state_nsa_win[l], nsa_cmp_w[l], nsa_cmp_b[l], rel_bias)
        k_past = cache_fox_k[l, page_table].reshape(db, past_len, FOX_HEADS, HEAD_DIM)
        v_past = cache_fox_v[l, page_table].reshape(db, past_len, FOX_HEADS, HEAD_DIM)
        lf_past = cache_fox_logf[l, page_table].reshape(db, past_len, FOX_HEADS)
        o_fox, fk, fv, fl = fox_sample(p_fox, fox_b_f[l], k_past, v_past, lf_past)
        xs = mix_sub(xs, l, o_rw, o_nsa, o_fox)
        xs = ffn_sub(xs, l, 1)
        s_kv.append(kv_rows); s_fk.append(fk); s_fv.append(fv); s_fl.append(fl)
        s_win.append(win); s_wkv_l.append(s_wkv); s_shift_l.append(s_shift)

    return (xp, xs,
            jnp.stack(p_kv), jnp.stack(p_fk), jnp.stack(p_fv), jnp.stack(p_fl),
            jnp.stack(p_win), jnp.stack(p_wkv), jnp.stack(p_shift),
            jnp.stack(s_kv), jnp.stack(s_fk), jnp.stack(s_fv), jnp.stack(s_fl),
            jnp.stack(s_win), jnp.stack(s_wkv_l), jnp.stack(s_shift_l))
```

```python
import functools
import math

import jax
import jax.numpy as jnp
from jax import lax
from jax.experimental import pallas as pl
from jax.experimental.pallas import tpu as pltpu

F32 = jnp.float32
BF16 = jnp.bfloat16

D_MODEL = 2048
DEPTH = 2
HEAD_DIM = 64
RW_HEADS = 8
RW_WIDTH = 512
DECAY_LORA = 64
AAA_LORA = 64
GATE_LORA = 128
RW_COLS = 3 * RW_WIDTH + DECAY_LORA + AAA_LORA + GATE_LORA
RW_SPLITS = [RW_WIDTH, 2 * RW_WIDTH, 3 * RW_WIDTH, 3 * RW_WIDTH + DECAY_LORA, 3 * RW_WIDTH + DECAY_LORA + AAA_LORA]
RW_GN_EPS = 64e-5
NSA_HEADS = 8
NSA_KV_HEADS = 2
NSA_REP = NSA_HEADS // NSA_KV_HEADS
NSA_WIDTH = 512
NSA_KV_WIDTH = 128
NSA_COLS = NSA_WIDTH + 6 * NSA_KV_WIDTH + 3 * NSA_HEADS
NSA_SPLITS = [NSA_WIDTH + i * NSA_KV_WIDTH for i in range(7)]
CMP_BLOCK = 32
CMP_STRIDE = 16
SEL_BLOCK = 64
N_SELECT = 16
WINDOW = 512
SEL_FORCE = 1e4
FOX_HEADS = 16
FOX_WIDTH = 1024
FOX_COLS = 3 * FOX_WIDTH + FOX_HEADS
FOX_SPLITS = [FOX_WIDTH, 2 * FOX_WIDTH, 3 * FOX_WIDTH]
IN_SPLITS = [RW_COLS, RW_COLS + NSA_COLS]
N_BUCKETS = 32
MAX_DISTANCE = 1024
Q_BLOCK = 128
DEEPNORM_ALPHA = (2 * DEPTH) ** 0.25
LN_EPS = 1e-5
RMS_EPS = 1e-6
ATT_SCALE = HEAD_DIM ** -0.5

LANES = 128
VMEM_LIMIT = 56 * 1024 * 1024
NEG = -1e30


def _cparams(sem):
    return pltpu.CompilerParams(dimension_semantics=sem, vmem_limit_bytes=VMEM_LIMIT)


def _pad_cols(w, n):
    return jnp.pad(w, ((0, 0), (0, n - w.shape[1])))


def _ffn_in_kernel(x_ref, wg_ref, wu_ref, h_ref):
    x = x_ref[...]
    g = jnp.dot(x, wg_ref[...], preferred_element_type=F32)
    u = jnp.dot(x, wu_ref[...], preferred_element_type=F32)
    h_ref[...] = (g * (1.0 / (1.0 + jnp.exp(-g))) * u).astype(BF16)


def ffn_in(x_bf, wg, wu):
    m, d = x_bf.shape
    f = wg.shape[1]
    tm = min(m, 1024)
    tn = 512
    return pl.pallas_call(
        _ffn_in_kernel,
        grid=(m // tm, f // tn),
        in_specs=[pl.BlockSpec((tm, d), lambda i, j: (i, 0)),
                  pl.BlockSpec((d, tn), lambda i, j: (0, j)),
                  pl.BlockSpec((d, tn), lambda i, j: (0, j))],
        out_specs=pl.BlockSpec((tm, tn), lambda i, j: (i, j)),
        out_shape=jax.ShapeDtypeStruct((m, f), BF16),
        compiler_params=_cparams(("parallel", "arbitrary")),
        name="ffn_in",
    )(x_bf, wg, wu)


def _layer_norm(y, g, b):
    mu = jnp.mean(y, -1, keepdims=True)
    d = y - mu
    var = jnp.mean(d * d, -1, keepdims=True)
    return d * lax.rsqrt(var + LN_EPS) * g + b


def _ffn_out_kernel(h_ref, wd_ref, x_ref, g_ref, b_ref, y_ref, ybf_ref, acc_ref):
    k = pl.program_id(1)

    @pl.when(k == 0)
    def _():
        acc_ref[...] = jnp.zeros_like(acc_ref)

    acc_ref[...] += jnp.dot(h_ref[...], wd_ref[...], preferred_element_type=F32)

    @pl.when(k == pl.num_programs(1) - 1)
    def _():
        y = _layer_norm(DEEPNORM_ALPHA * x_ref[...] + 0.5 * acc_ref[...], g_ref[...], b_ref[...])
        y_ref[...] = y
        ybf_ref[...] = y.astype(BF16)


def ffn_out(h_bf, wd, x, g, b):
    m, f = h_bf.shape
    d = wd.shape[1]
    tm = min(m, 512)
    tk = 512
    return pl.pallas_call(
        _ffn_out_kernel,
        grid=(m // tm, f // tk),
        in_specs=[pl.BlockSpec((tm, tk), lambda i, k: (i, k)),
                  pl.BlockSpec((tk, d), lambda i, k: (k, 0)),
                  pl.BlockSpec((tm, d), lambda i, k: (i, 0)),
                  pl.BlockSpec((1, d), lambda i, k: (0, 0)),
                  pl.BlockSpec((1, d), lambda i, k: (0, 0))],
        out_specs=[pl.BlockSpec((tm, d), lambda i, k: (i, 0)),
                   pl.BlockSpec((tm, d), lambda i, k: (i, 0))],
        out_shape=[jax.ShapeDtypeStruct((m, d), F32), jax.ShapeDtypeStruct((m, d), BF16)],
        scratch_shapes=[pltpu.VMEM((tm, d), F32)],
        compiler_params=_cparams(("parallel", "arbitrary")),
        name="ffn_out",
    )(h_bf, wd, x, g.reshape(1, d), b.reshape(1, d))


def _proj_kernel(x_ref, w_ref, o_ref):
    o_ref[...] = jnp.dot(x_ref[...], w_ref[...], preferred_element_type=F32)


def proj(x_bf, w_bf):
    m, d = x_bf.shape
    n = w_bf.shape[1]
    tm = min(m, 512)
    tn = 128
    for c in (640, 512, 384, 256):
        if n % c == 0:
            tn = c
            break
    return pl.pallas_call(
        _proj_kernel,
        grid=(m // tm, n // tn),
        in_specs=[pl.BlockSpec((tm, d), lambda i, j: (i, 0)),
                  pl.BlockSpec((d, tn), lambda i, j: (0, j))],
        out_specs=pl.BlockSpec((tm, tn), lambda i, j: (i, j)),
        out_shape=jax.ShapeDtypeStruct((m, n), F32),
        compiler_params=_cparams(("parallel", "arbitrary")),
        name="proj",
    )(x_bf, w_bf)


def _log_sigmoid(x):
    return jnp.minimum(x, 0.0) - jnp.log1p(jnp.exp(-jnp.abs(x)))


def _fox_proj_kernel(x_ref, w_ref, bf_ref, qbf_ref, k_ref, v_ref, kbf_ref, vbf_ref, lf_ref):
    x = x_ref[...]
    q = jnp.dot(x, w_ref[:, 0:FOX_WIDTH], preferred_element_type=F32)
    qbf_ref[...] = (q * ATT_SCALE).astype(BF16)
    k = jnp.dot(x, w_ref[:, FOX_WIDTH:2 * FOX_WIDTH], preferred_element_type=F32)
    k_ref[...] = k
    kbf_ref[...] = k.astype(BF16)
    v = jnp.dot(x, w_ref[:, 2 * FOX_WIDTH:3 * FOX_WIDTH], preferred_element_type=F32)
    v_ref[...] = v
    vbf_ref[...] = v.astype(BF16)
    f = jnp.dot(x, w_ref[:, 3 * FOX_WIDTH:3 * FOX_WIDTH + LANES], preferred_element_type=F32)
    lf_ref[...] = _log_sigmoid(f + bf_ref[...])


def fox_proj(x_bf, w_bf, b_f):
    m, d = x_bf.shape
    n = w_bf.shape[1]
    tm = min(m, 256)
    row = lambda width: pl.BlockSpec((tm, width), lambda i: (i, 0))
    bias = jnp.pad(b_f.astype(F32), (0, LANES - FOX_HEADS)).reshape(1, LANES)
    return pl.pallas_call(
        _fox_proj_kernel,
        grid=(m // tm,),
        in_specs=[row(d), pl.BlockSpec((d, n), lambda i: (0, 0)), pl.BlockSpec((1, LANES), lambda i: (0, 0))],
        out_specs=[row(FOX_WIDTH)] * 5 + [row(LANES)],
        out_shape=[jax.ShapeDtypeStruct((m, FOX_WIDTH), BF16),
                   jax.ShapeDtypeStruct((m, FOX_WIDTH), F32), jax.ShapeDtypeStruct((m, FOX_WIDTH), F32),
                   jax.ShapeDtypeStruct((m, FOX_WIDTH), BF16), jax.ShapeDtypeStruct((m, FOX_WIDTH), BF16),
                   jax.ShapeDtypeStruct((m, LANES), F32)],
        compiler_params=_cparams(("parallel",)),
        name="fox_proj",
    )(x_bf, w_bf, bias)


def _split3(x):
    hi = x.astype(BF16)
    r = x - hi.astype(F32)
    mid = r.astype(BF16)
    lo = (r - mid.astype(F32)).astype(BF16)
    return hi, mid, lo


def _seg_sum(x, ones_blk):
    hi, mid, lo = _split3(x)
    d = lambda a: jnp.dot(a, ones_blk, preferred_element_type=F32)
    return d(hi) + d(mid) + d(lo)


def _head_ones():
    r = lax.broadcasted_iota(jnp.int32, (LANES, LANES), 0) // HEAD_DIM
    c = lax.broadcasted_iota(jnp.int32, (LANES, LANES), 1) // HEAD_DIM
    return (r == c).astype(BF16)


def _mix_out_kernel(orw_ref, onsa_ref, ofox_ref, x_ref, w_ref, gn_ref, gf_ref, g_ref, b_ref, y_ref, ybf_ref):
    ones_blk = _head_ones()

    def rms(o, gain):
        cols = []
        for c in range(o.shape[1] // LANES):
            blk = o[:, c * LANES:(c + 1) * LANES]
            ms = _seg_sum(blk * blk, ones_blk) * (1.0 / HEAD_DIM)
            cols.append(blk * lax.rsqrt(ms + RMS_EPS) * gain[:, c * LANES:(c + 1) * LANES])
        return jnp.concatenate(cols, axis=1)

    o = jnp.concatenate([orw_ref[...], rms(onsa_ref[...], gn_ref[...]), rms(ofox_ref[...], gf_ref[...])], axis=1)
    acc = jnp.dot(o.astype(BF16), w_ref[...], preferred_element_type=F32)
    y = _layer_norm(DEEPNORM_ALPHA * x_ref[...] + acc, g_ref[...], b_ref[...])
    y_ref[...] = y
    ybf_ref[...] = y.astype(BF16)


def mix_out(o_rw, o_nsa, o_fox, x, w_bf, gn, gf, g, b):
    m, d = x.shape
    tm = min(m, 256)
    row = lambda width: pl.BlockSpec((tm, width), lambda i: (i, 0))
    full = lambda a, c: pl.BlockSpec((a, c), lambda i: (0, 0))
    return pl.pallas_call(
        _mix_out_kernel,
        grid=(m // tm,),
        in_specs=[row(RW_WIDTH), row(NSA_WIDTH), row(FOX_WIDTH), row(d), full(d, d),
                  full(1, NSA_WIDTH), full(1, FOX_WIDTH), full(1, d), full(1, d)],
        out_specs=[row(d), row(d)],
        out_shape=[jax.ShapeDtypeStruct((m, d), F32), jax.ShapeDtypeStruct((m, d), BF16)],
        compiler_params=_cparams(("parallel",)),
        name="mix_out",
    )(o_rw, o_nsa, o_fox, x, w_bf, gn.reshape(1, -1), gf.reshape(1, -1), g.reshape(1, d), b.reshape(1, d))


CUM_BLOCK = 512


def _cumsum_kernel(x_ref, o_ref, carry_ref):
    @pl.when(pl.program_id(0) == 0)
    def _():
        carry_ref[...] = jnp.zeros_like(carry_ref)

    r = lax.broadcasted_iota(jnp.int32, (CUM_BLOCK, CUM_BLOCK), 0)
    c = lax.broadcasted_iota(jnp.int32, (CUM_BLOCK, CUM_BLOCK), 1)
    upper = (r <= c).astype(BF16)
    hi, mid, lo = _split3(x_ref[...])
    d = lambda a: jnp.dot(a, upper, preferred_element_type=F32)
    cs = d(hi) + d(mid) + d(lo) + carry_ref[:, 0:1]
    o_ref[...] = cs
    carry_ref[...] = jnp.broadcast_to(cs[:, CUM_BLOCK - 1:CUM_BLOCK], carry_ref.shape)


def cumsum_lanes(x):
    rows, t = x.shape
    return pl.pallas_call(
        _cumsum_kernel,
        grid=(t // CUM_BLOCK,),
        in_specs=[pl.BlockSpec((rows, CUM_BLOCK), lambda i: (0, i))],
        out_specs=pl.BlockSpec((rows, CUM_BLOCK), lambda i: (0, i)),
        out_shape=jax.ShapeDtypeStruct((rows, t), F32),
        scratch_shapes=[pltpu.VMEM((rows, LANES), F32)],
        compiler_params=_cparams(("arbitrary",)),
        name="cumsum",
    )(x)


def _fox_flash_kernel(qt_ref, kt_ref, q_ref, k_ref, v_ref, ck_ref, o_ref, m_ref, l_ref, acc_ref, *, tq, tk):
    step = pl.program_id(1)
    qi = qt_ref[step]
    ki = kt_ref[step]

    @pl.when(ki == 0)
    def _():
        m_ref[...] = jnp.full_like(m_ref, NEG)
        l_ref[...] = jnp.zeros_like(l_ref)
        acc_ref[...] = jnp.zeros_like(acc_ref)

    left = lax.broadcasted_iota(jnp.int32, (1, LANES), 1) < HEAD_DIM

    def tile(diagonal):
        q = q_ref[...]
        k = k_ref[...]
        v = v_ref[...]
        pv, alphas = [], []
        for hh in range(2):
            qm = jnp.where(left if hh == 0 else jnp.logical_not(left), q, jnp.zeros_like(q))
            s = lax.dot_general(qm, k, (((1,), (1,)), ((), ())), preferred_element_type=F32)
            s = s - ck_ref[0, hh:hh + 1, :]
            if diagonal:
                rows = lax.broadcasted_iota(jnp.int32, (tq, tk), 0)
                cols = lax.broadcasted_iota(jnp.int32, (tq, tk), 1)
                s = jnp.where(rows >= cols, s, NEG)
            m_old = m_ref[hh]
            m_new = jnp.maximum(m_old, jnp.max(s, axis=-1, keepdims=True))
            p = jnp.exp(s - m_new)
            alpha = jnp.exp(m_old - m_new)
            l_ref[hh] = alpha * l_ref[hh] + jnp.sum(p, axis=-1, keepdims=True)
            m_ref[hh] = m_new
            pv.append(jnp.dot(p.astype(BF16), v, preferred_element_type=F32))
            alphas.append(alpha)
        acc_ref[...] = (jnp.where(left, alphas[0], alphas[1]) * acc_ref[...]
                        + jnp.where(left, pv[0], pv[1]))

    @pl.when(ki < qi)
    def _():
        tile(False)

    @pl.when(ki == qi)
    def _():
        tile(True)
        o_ref[...] = acc_ref[...] / jnp.where(left, l_ref[0], l_ref[1])


def _tri_steps(n):
    qs, ks = [], []
    for qi in range(n):
        for ki in range(qi + 1):
            qs.append(qi)
            ks.append(ki)
    return jnp.asarray(qs, jnp.int32), jnp.asarray(ks, jnp.int32)


def fox_flash(q_bf, k_bf, v_bf, ck, t):
    tq = tk = min(t, 512)
    n = t // tq
    qt, kt = _tri_steps(n)
    pairs = FOX_WIDTH // LANES
    grid_spec = pltpu.PrefetchScalarGridSpec(
        num_scalar_prefetch=2,
        grid=(pairs, qt.shape[0]),
        in_specs=[pl.BlockSpec((tq, LANES), lambda p, s, qt, kt: (qt[s], p)),
                  pl.BlockSpec((tk, LANES), lambda p, s, qt, kt: (kt[s], p)),
                  pl.BlockSpec((tk, LANES), lambda p, s, qt, kt: (kt[s], p)),
                  pl.BlockSpec((1, 8, tk), lambda p, s, qt, kt: (p, 0, kt[s]))],
        out_specs=pl.BlockSpec((tq, LANES), lambda p, s, qt, kt: (qt[s], p)),
        scratch_shapes=[pltpu.VMEM((2, tq, 1), F32), pltpu.VMEM((2, tq, 1), F32), pltpu.VMEM((tq, LANES), F32)],
    )
    return pl.pallas_call(
        functools.partial(_fox_flash_kernel, tq=tq, tk=tk),
        grid_spec=grid_spec,
        out_shape=jax.ShapeDtypeStruct((t, FOX_WIDTH), F32),
        compiler_params=_cparams(("parallel", "arbitrary")),
        name="fox_flash",
    )(qt, kt, q_bf, k_bf, v_bf, ck)


def fox_prompt_attn(qbf, kbf, vbf, logf128, t):
    lf = logf128[:t, :FOX_HEADS]
    lf_t = jnp.pad(lf.T.reshape(FOX_HEADS // 2, 2, t), ((0, 0), (0, 6), (0, 0))).reshape(-1, t)
    ck = cumsum_lanes(lf_t).reshape(FOX_HEADS // 2, 8, t)
    return fox_flash(qbf, kbf, vbf, ck, t)


def masked_softmax(logits, mask):
    s = jnp.where(mask, logits, -jnp.inf)
    m = jnp.max(s, axis=-1, keepdims=True)
    m = jnp.where(jnp.isfinite(m), m, 0.0)
    e = jnp.where(mask, jnp.exp(s - m), 0.0)
    return e / jnp.maximum(jnp.sum(e, -1, keepdims=True), 1e-30)


def t5_bucket(dist):
    d = jnp.maximum(dist, 0)
    max_exact = N_BUCKETS // 2
    df = jnp.maximum(d, 1).astype(jnp.float32)
    large = max_exact + (jnp.log(df / max_exact) / math.log(MAX_DISTANCE / max_exact)
                         * (N_BUCKETS - max_exact)).astype(jnp.int32)
    large = jnp.minimum(large, N_BUCKETS - 1)
    return jnp.where(d < max_exact, d, large)


def rwkv_mix(p_rw, p_prev, s0, mu, w0, w2, a0, a2, g2, k_k, k_a, r_k, gn_g, gn_b):
    f32 = jnp.float32
    B, T = p_rw.shape[:2]
    shifted = jnp.concatenate([p_prev[:, None].astype(p_rw.dtype), p_rw[:, :-1]], axis=1)
    z = p_rw + (shifted - p_rw) * mu
    r, k, v, wd, ad, gd = jnp.split(z, RW_SPLITS, axis=-1)
    heads = lambda t: t.astype(f32).reshape(B, T, RW_HEADS, HEAD_DIM)
    w_log = -jax.nn.softplus(-(w0 + jnp.tanh(wd) @ w2).astype(f32)) - 0.5
    decay = heads(jnp.exp(-jnp.exp(w_log)))
    a = heads(jax.nn.sigmoid((a0 + ad @ a2).astype(f32)))
    g = jax.nn.sigmoid(gd) @ g2
    kk = heads(k * k_k)
    kk = kk / jnp.maximum(jnp.linalg.norm(kk, axis=-1, keepdims=True), 1e-12)
    kh = heads(k) * (1.0 + (a - 1.0) * k_a.astype(f32).reshape(RW_HEADS, HEAD_DIM))
    rh, vh = heads(r), heads(v)

    def step(S, inp):
        r_t, w_t, k_t, v_t, kk_t, a_t = inp
        sa = jnp.einsum('bhvk,bhk->bhv', S, -kk_t)
        S = (S * w_t[:, :, None, :] + sa[..., None] * (kk_t * a_t)[:, :, None, :]
             + v_t[..., None] * k_t[:, :, None, :])
        return S, jnp.einsum('bhvk,bhk->bhv', S, r_t)

    xs = tuple(jnp.swapaxes(t, 0, 1) for t in (rh, decay, kh, vh, kk, a))
    s_T, out = lax.scan(step, s0.astype(f32), xs)
    out = jnp.swapaxes(out, 0, 1)
    m = jnp.mean(out, -1, keepdims=True)
    var = jnp.mean(jnp.square(out - m), -1, keepdims=True)
    y = ((out - m) * lax.rsqrt(var + RW_GN_EPS)).reshape(B, T, RW_WIDTH) * gn_g + gn_b
    bonus = (jnp.sum(rh * kh * r_k, -1, keepdims=True) * vh).reshape(B, T, RW_WIDTH)
    y = (y + bonus) * g
    return y.astype(p_rw.dtype), s_T, p_rw[:, -1]


def nsa_split(p_nsa):
    B, T = p_nsa.shape[:2]
    q, kc, vc, ks, vs, kw, vw, gl = jnp.split(p_nsa, NSA_SPLITS, axis=-1)
    kvh = lambda t: t.reshape(B, T, NSA_KV_HEADS, HEAD_DIM)
    return (q.reshape(B, T, NSA_HEADS, HEAD_DIM), kvh(kc), kvh(vc), kvh(ks), kvh(vs),
            kvh(kw), kvh(vw), gl.reshape(B, T, NSA_HEADS, 3))


def nsa_compress(k, w, b):
    B, L = k.shape[:2]
    nc = (L - CMP_BLOCK) // CMP_STRIDE + 1
    ch = k[:, :(nc + 1) * CMP_STRIDE].reshape(B, nc + 1, CMP_STRIDE, NSA_KV_HEADS, HEAD_DIM)
    first = jnp.einsum('bcjgd,jde->bcge', ch, w[:CMP_STRIDE])
    second = jnp.einsum('bcjgd,jde->bcge', ch, w[CMP_STRIDE:])
    return first[:, :-1] + second[:, 1:] + b


def sel_blocks(k, ns):
    B, L = k.shape[:2]
    k = jnp.pad(k, ((0, 0), (0, ns * SEL_BLOCK - L), (0, 0), (0, 0)))
    return k.reshape(B, ns, SEL_BLOCK, NSA_KV_HEADS, HEAD_DIM).transpose(0, 3, 1, 2, 4)


def cover_matrix(nc, ns):
    start = jnp.arange(nc)[:, None] * CMP_STRIDE
    j = jnp.arange(ns)[None, :]
    return ((start < (j + 1) * SEL_BLOCK) & (start + CMP_BLOCK > j * SEL_BLOCK)).astype(jnp.float32)


def nsa_context(kc_raw, vc_raw, ks, vs, cmp_w, cmp_b):
    L = kc_raw.shape[1]
    kc = nsa_compress(kc_raw, cmp_w[0], cmp_b[0])
    vc = nsa_compress(vc_raw, cmp_w[1], cmp_b[1])
    nc = kc.shape[1]
    c_end = jnp.arange(nc) * CMP_STRIDE + CMP_BLOCK - 1
    ns = -(-L // SEL_BLOCK)
    return kc, vc, c_end, sel_blocks(ks, ns), sel_blocks(vs, ns), cover_matrix(nc, ns)


def nsa_attend(q, gates, tq, kc, vc, c_end, kb, vb, cover, kw, vw, pw, rel_bias):
    f32 = jnp.float32
    B, Tq = q.shape[:2]
    G, R = NSA_KV_HEADS, NSA_REP
    qg = q.reshape(B, Tq, G, R, HEAD_DIM)

    def bias_2d(dist):
        return rel_bias[t5_bucket(dist)].reshape(dist.shape + (G, R)).transpose(2, 3, 0, 1).astype(f32)

    dist_c = tq[:, None] - c_end[None, :]
    lc = jnp.einsum('btgrd,bigd->bgrti', qg, kc).astype(f32) * ATT_SCALE + bias_2d(dist_c)
    p_c = masked_softmax(lc, dist_c >= 0)
    o_c = jnp.einsum('bgrti,bigd->btgrd', p_c.astype(vc.dtype), vc)

    ns = kb.shape[2]
    imp = jnp.einsum('bgrti,ij->btgj', p_c, cover)
    jq = tq // SEL_BLOCK
    jb = jnp.arange(ns)[None, :]
    forced = (jb == 0) | (jb == jq[:, None]) | (jb == jq[:, None] - 1)
    allowed = jb <= jq[:, None]
    score = jnp.where(allowed[None, :, None, :],
                      imp + jnp.where(forced, SEL_FORCE, 0.0)[None, :, None, :], -jnp.inf)
    top_v, top_i = lax.top_k(score, min(N_SELECT, ns))
    top_i = top_i.transpose(0, 2, 1, 3)
    valid = jnp.isfinite(top_v).transpose(0, 2, 1, 3)
    n = top_i.shape[-1]
    bi = jnp.arange(B)[:, None, None, None]
    gi = jnp.arange(G)[None, :, None, None]
    ksel = kb[bi, gi, top_i]
    vsel = vb[bi, gi, top_i]
    pos = top_i[..., None] * SEL_BLOCK + jnp.arange(SEL_BLOCK)
    dist_s = tq[None, None, :, None, None] - pos
    table_g = rel_bias.reshape(N_BUCKETS, G, R).transpose(1, 0, 2)
    bias_s = jnp.moveaxis(table_g[gi[..., None], t5_bucket(dist_s)], -1, 2).astype(f32)
    ls = jnp.einsum('btgrd,bgtnkd->bgrtnk', qg, ksel).astype(f32) * ATT_SCALE + bias_s
    ok = ((dist_s >= 0) & valid[..., None])[:, :, None]
    p_s = masked_softmax(ls.reshape(B, G, R, Tq, n * SEL_BLOCK), ok.reshape(B, G, 1, Tq, n * SEL_BLOCK))
    o_s = jnp.einsum('bgrtm,bgtmd->btgrd', p_s.astype(vsel.dtype),
                     vsel.reshape(B, G, Tq, n * SEL_BLOCK, HEAD_DIM))

    dist_w = tq[:, None] - pw[None, :]
    lw = jnp.einsum('btgrd,bsgd->bgrts', qg, kw).astype(f32) * ATT_SCALE + bias_2d(dist_w)
    p_w = masked_softmax(lw, (dist_w >= 0) & (dist_w < WINDOW) & (pw >= 0)[None, :])
    o_w = jnp.einsum('bgrts,bsgd->btgrd', p_w.astype(vw.dtype), vw)

    gt = jax.nn.sigmoid(gates.astype(f32)).reshape(B, Tq, G, R, 3)
    o = gt[..., 0:1] * o_c + gt[..., 1:2] * o_s + gt[..., 2:3] * o_w
    return o.reshape(B, Tq, NSA_WIDTH).astype(q.dtype)


def nsa_prompt(p_nsa, cmp_w, cmp_b, rel_bias):
    q, kc_raw, vc_raw, ks, vs, kw, vw, gl = nsa_split(p_nsa)
    B, T = q.shape[:2]
    ctx = nsa_context(kc_raw, vc_raw, ks, vs, cmp_w, cmp_b)
    pad = ((0, 0), (WINDOW, 0), (0, 0), (0, 0))
    kw_pad, vw_pad = jnp.pad(kw, pad), jnp.pad(vw, pad)
    nqb = T // Q_BLOCK
    span = WINDOW + Q_BLOCK
    qb = q.reshape(B, nqb, Q_BLOCK, NSA_HEADS, HEAD_DIM).swapaxes(0, 1)
    gb = gl.reshape(B, nqb, Q_BLOCK, NSA_HEADS, 3).swapaxes(0, 1)

    def block(args):
        qi, gi, s0 = args
        kwb = lax.dynamic_slice_in_dim(kw_pad, s0, span, axis=1)
        vwb = lax.dynamic_slice_in_dim(vw_pad, s0, span, axis=1)
        return nsa_attend(qi, gi, s0 + jnp.arange(Q_BLOCK), *ctx, kwb, vwb,
                          s0 - WINDOW + jnp.arange(span), rel_bias)

    o = lax.map(block, (qb, gb, jnp.arange(nqb) * Q_BLOCK))
    o = o.swapaxes(0, 1).reshape(B, T, NSA_WIDTH)
    rows = jnp.stack([kc_raw, vc_raw, ks, vs], axis=2)
    win = jnp.stack([kw, vw], axis=2)[:, T - min(WINDOW, T):]
    return o, rows, win


def nsa_sample(p_nsa, past, win_buf, cmp_w, cmp_b, rel_bias):
    q, kc_new, vc_new, ks_new, vs_new, kw, vw, gl = nsa_split(p_nsa)
    S = q.shape[1]
    P = past.shape[1]
    wb = win_buf.shape[1]
    cat = lambda i, new: jnp.concatenate([past[:, :, i], new], axis=1)
    ctx = nsa_context(cat(0, kc_new), cat(1, vc_new), cat(2, ks_new), cat(3, vs_new), cmp_w, cmp_b)
    win_all = jnp.concatenate([win_buf, jnp.stack([kw, vw], axis=2)], axis=1)
    o = nsa_attend(q, gl, P + jnp.arange(S), *ctx, win_all[:, :, 0], win_all[:, :, 1],
                   P - wb + jnp.arange(wb + S), rel_bias)
    rows = jnp.stack([kc_new, vc_new, ks_new, vs_new], axis=2)
    return o, rows, win_all[:, S:]


def fox_sample_attn(q, k, v, logf, k_past, v_past, logf_past):
    B, S = q.shape[:2]
    P = k_past.shape[1]
    c_past = jnp.cumsum(logf_past.astype(jnp.float32), axis=1)
    c_new = c_past[:, -1:] + jnp.cumsum(logf, axis=1)
    cp, cn = c_past.transpose(0, 2, 1), c_new.transpose(0, 2, 1)
    l_past = (jnp.einsum('bthd,bshd->bhts', q, k_past).astype(jnp.float32) * ATT_SCALE
              + (cn[..., None] - cp[:, :, None, :]))
    l_new = (jnp.einsum('bthd,bshd->bhts', q, k).astype(jnp.float32) * ATT_SCALE
             + (cn[..., None] - cn[:, :, None, :]))
    ar = jnp.arange(S)
    mask = jnp.concatenate([jnp.ones((S, P), bool), ar[None, :] <= ar[:, None]], axis=1)
    p = masked_softmax(jnp.concatenate([l_past, l_new], axis=-1), mask)
    o = (jnp.einsum('bhts,bshd->bthd', p[..., :P].astype(v.dtype), v_past)
         + jnp.einsum('bhts,bshd->bthd', p[..., P:].astype(v.dtype), v))
    return o.reshape(B, S, FOX_WIDTH)


def kernel(x_prompt, x_sample, cache_nsa_kv, cache_fox_k, cache_fox_v, cache_fox_logf, state_nsa_win, state_rwkv_wkv, state_rwkv_shift, page_table, ln_g, ln_b, ffn_w_gate, ffn_w_up, ffn_w_down, w_in, w_out, rw_mu, rw_w0, rw_w2, rw_a0, rw_a2, rw_g2, rw_k_k, rw_k_a, rw_r_k, rw_gn_g, rw_gn_b, nsa_cmp_w, nsa_cmp_b, nsa_out_g, rel_bias, fox_b_f, fox_out_g):
    db = x_sample.shape[0]
    t = x_prompt.shape[1]
    past_len = page_table.shape[1] * cache_nsa_kv.shape[2]

    wg_bf = ffn_w_gate.astype(BF16)
    wu_bf = ffn_w_up.astype(BF16)
    wd_bf = ffn_w_down.astype(BF16)
    wout_bf = w_out.astype(BF16)
    w_rw = w_in[:, :, :RW_COLS].astype(BF16)
    w_nsa = jnp.pad(w_in[:, :, RW_COLS:RW_COLS + NSA_COLS].astype(BF16), ((0, 0), (0, 0), (0, 1408 - NSA_COLS)))
    w_fox = jnp.pad(w_in[:, :, RW_COLS + NSA_COLS:].astype(BF16), ((0, 0), (0, 0), (0, 3200 - FOX_COLS)))

    def ffn_sub(x, x_bf, l, j):
        h = ffn_in(x_bf, wg_bf[l, j], wu_bf[l, j])
        return ffn_out(h, wd_bf[l, j], x, ln_g[l, 2 * j], ln_b[l, 2 * j])

    def rw_params(l):
        return (rw_mu[l], rw_w0[l], rw_w2[l], rw_a0[l], rw_a2[l], rw_g2[l], rw_k_k[l], rw_k_a[l],
                rw_r_k[l], rw_gn_g[l], rw_gn_b[l])

    xp = x_prompt[0]
    xp_bf = xp.astype(BF16)
    p_kv, p_fk, p_fv, p_fl, p_win, p_wkv, p_shift = [], [], [], [], [], [], []
    for l in range(DEPTH):
        xp, xp_bf = ffn_sub(xp, xp_bf, l, 0)
        p_rw = proj(xp_bf, w_rw[l])[None]
        p_nsa = proj(xp_bf, w_nsa[l])[None, :, :NSA_COLS]
        qbf, fk, fv, kbf, vbf, lf128 = fox_proj(xp_bf, w_fox[l], fox_b_f[l])
        o_rw, s_wkv, s_shift = rwkv_mix(p_rw, jnp.zeros((1, RW_COLS), F32),
                                        jnp.zeros((1, RW_HEADS, HEAD_DIM, HEAD_DIM), F32), *rw_params(l))
        o_nsa, kv_rows, win = nsa_prompt(p_nsa, nsa_cmp_w[l], nsa_cmp_b[l], rel_bias)
        o_fox = fox_prompt_attn(qbf, kbf, vbf, lf128, t)
        xp, xp_bf = mix_out(o_rw[0], o_nsa[0], o_fox, xp, wout_bf[l], nsa_out_g[l], fox_out_g[l],
                            ln_g[l, 1], ln_b[l, 1])
        xp, xp_bf = ffn_sub(xp, xp_bf, l, 1)
        p_kv.append(kv_rows)
        p_fk.append(fk.reshape(1, t, FOX_HEADS, HEAD_DIM))
        p_fv.append(fv.reshape(1, t, FOX_HEADS, HEAD_DIM))
        p_fl.append(lf128[None, :, :FOX_HEADS])
        p_win.append(win)
        p_wkv.append(s_wkv)
        p_shift.append(s_shift)

    xs = x_sample[:, 0]
    xs_bf = xs.astype(BF16)
    s_kv, s_fk, s_fv, s_fl, s_win, s_wkv_l, s_shift_l = [], [], [], [], [], [], []
    for l in range(DEPTH):
        xs, xs_bf = ffn_sub(xs, xs_bf, l, 0)
        p_rw = proj(xs_bf, w_rw[l])[:, None]
        p_nsa = proj(xs_bf, w_nsa[l])[:, None, :NSA_COLS]
        qbf, fk, fv, kbf, vbf, lf128 = fox_proj(xs_bf, w_fox[l], fox_b_f[l])
        o_rw, s_wkv, s_shift = rwkv_mix(p_rw, state_rwkv_shift[l], state_rwkv_wkv[l], *rw_params(l))
        nsa_past = cache_nsa_kv[l, page_table].reshape(db, past_len, 4, NSA_KV_HEADS, HEAD_DIM)
        o_nsa, kv_rows, win = nsa_sample(p_nsa, nsa_past, state_nsa_win[l], nsa_cmp_w[l], nsa_cmp_b[l], rel_bias)
        k_past = cache_fox_k[l, page_table].reshape(db, past_len, FOX_HEADS, HEAD_DIM)
        v_past = cache_fox_v[l, page_table].reshape(db, past_len, FOX_HEADS, HEAD_DIM)
        lf_past = cache_fox_logf[l, page_table].reshape(db, past_len, FOX_HEADS)
        heads = lambda a: a.reshape(db, 1, FOX_HEADS, HEAD_DIM)
        q_f32 = qbf.astype(F32) * (1.0 / ATT_SCALE)
        o_fox = fox_sample_attn(heads(q_f32), heads(fk), heads(fv), lf128[:, None, :FOX_HEADS],
                                k_past, v_past, lf_past)
        xs, xs_bf = mix_out(o_rw[:, 0], o_nsa[:, 0], o_fox[:, 0], xs, wout_bf[l], nsa_out_g[l], fox_out_g[l],
                            ln_g[l, 1], ln_b[l, 1])
        xs, xs_bf = ffn_sub(xs, xs_bf, l, 1)
        s_kv.append(kv_rows)
        s_fk.append(heads(fk))
        s_fv.append(heads(fv))
        s_fl.append(lf128[:, None, :FOX_HEADS])
        s_win.append(win)
        s_wkv_l.append(s_wkv)
        s_shift_l.append(s_shift)

    return (xp[None], xs[:, None],
            jnp.stack(p_kv), jnp.stack(p_fk), jnp.stack(p_fv), jnp.stack(p_fl),
            jnp.stack(p_win), jnp.stack(p_wkv), jnp.stack(p_shift),
            jnp.stack(s_kv), jnp.stack(s_fk), jnp.stack(s_fv), jnp.stack(s_fl),
            jnp.stack(s_win), jnp.stack(s_wkv_l), jnp.stack(s_shift_l))
```

```python
import functools
import math

import jax
import jax.numpy as jnp
from jax import lax
from jax.experimental import pallas as pl
from jax.experimental.pallas import tpu as pltpu

F32 = jnp.float32
BF16 = jnp.bfloat16

D_MODEL = 2048
DEPTH = 2
HEAD_DIM = 64
RW_HEADS = 8
RW_WIDTH = 512
DECAY_LORA = 64
AAA_LORA = 64
GATE_LORA = 128
RW_COLS = 3 * RW_WIDTH + DECAY_LORA + AAA_LORA + GATE_LORA
RW_SPLITS = [RW_WIDTH, 2 * RW_WIDTH, 3 * RW_WIDTH, 3 * RW_WIDTH + DECAY_LORA, 3 * RW_WIDTH + DECAY_LORA + AAA_LORA]
RW_GN_EPS = 64e-5
NSA_HEADS = 8
NSA_KV_HEADS = 2
NSA_REP = NSA_HEADS // NSA_KV_HEADS
NSA_WIDTH = 512
NSA_KV_WIDTH = 128
NSA_COLS = NSA_WIDTH + 6 * NSA_KV_WIDTH + 3 * NSA_HEADS
NSA_SPLITS = [NSA_WIDTH + i * NSA_KV_WIDTH for i in range(7)]
CMP_BLOCK = 32
CMP_STRIDE = 16
SEL_BLOCK = 64
N_SELECT = 16
WINDOW = 512
SEL_FORCE = 1e4
FOX_HEADS = 16
FOX_WIDTH = 1024
FOX_COLS = 3 * FOX_WIDTH + FOX_HEADS
FOX_SPLITS = [FOX_WIDTH, 2 * FOX_WIDTH, 3 * FOX_WIDTH]
IN_SPLITS = [RW_COLS, RW_COLS + NSA_COLS]
N_BUCKETS = 32
MAX_DISTANCE = 1024
Q_BLOCK = 128
DEEPNORM_ALPHA = (2 * DEPTH) ** 0.25
LN_EPS = 1e-5
RMS_EPS = 1e-6
ATT_SCALE = HEAD_DIM ** -0.5

LANES = 128
VMEM_LIMIT = 56 * 1024 * 1024
NEG = -1e30


def _cparams(sem):
    return pltpu.CompilerParams(dimension_semantics=sem, vmem_limit_bytes=VMEM_LIMIT)


def _pad_cols(w, n):
    return jnp.pad(w, ((0, 0), (0, n - w.shape[1])))


def _ffn_in_kernel(x_ref, wg_ref, wu_ref, h_ref):
    x = x_ref[...]
    g = jnp.dot(x, wg_ref[...], preferred_element_type=F32)
    u = jnp.dot(x, wu_ref[...], preferred_element_type=F32)
    h_ref[...] = (g * (1.0 / (1.0 + jnp.exp(-g))) * u).astype(BF16)


def ffn_in(x_bf, wg, wu):
    m, d = x_bf.shape
    f = wg.shape[1]
    tm = min(m, 1024)
    tn = 512
    return pl.pallas_call(
        _ffn_in_kernel,
        grid=(m // tm, f // tn),
        in_specs=[pl.BlockSpec((tm, d), lambda i, j: (i, 0)),
                  pl.BlockSpec((d, tn), lambda i, j: (0, j)),
                  pl.BlockSpec((d, tn), lambda i, j: (0, j))],
        out_specs=pl.BlockSpec((tm, tn), lambda i, j: (i, j)),
        out_shape=jax.ShapeDtypeStruct((m, f), BF16),
        compiler_params=_cparams(("parallel", "arbitrary")),
        name="ffn_in",
    )(x_bf, wg, wu)


def _layer_norm(y, g, b):
    mu = jnp.mean(y, -1, keepdims=True)
    d = y - mu
    var = jnp.mean(d * d, -1, keepdims=True)
    return d * lax.rsqrt(var + LN_EPS) * g + b


def _ffn_out_kernel(h_ref, wd_ref, x_ref, g_ref, b_ref, y_ref, ybf_ref, acc_ref):
    k = pl.program_id(1)

    @pl.when(k == 0)
    def _():
        acc_ref[...] = jnp.zeros_like(acc_ref)

    acc_ref[...] += jnp.dot(h_ref[...], wd_ref[...], preferred_element_type=F32)

    @pl.when(k == pl.num_programs(1) - 1)
    def _():
        y = _layer_norm(DEEPNORM_ALPHA * x_ref[...] + 0.5 * acc_ref[...], g_ref[...], b_ref[...])
        y_ref[...] = y
        ybf_ref[...] = y.astype(BF16)


def ffn_out(h_bf, wd, x, g, b):
    m, f = h_bf.shape
    d = wd.shape[1]
    tm = min(m, 512)
    tk = 512
    return pl.pallas_call(
        _ffn_out_kernel,
        grid=(m // tm, f // tk),
        in_specs=[pl.BlockSpec((tm, tk), lambda i, k: (i, k)),
                  pl.BlockSpec((tk, d), lambda i, k: (k, 0)),
                  pl.BlockSpec((tm, d), lambda i, k: (i, 0)),
                  pl.BlockSpec((1, d), lambda i, k: (0, 0)),
                  pl.BlockSpec((1, d), lambda i, k: (0, 0))],
        out_specs=[pl.BlockSpec((tm, d), lambda i, k: (i, 0)),
                   pl.BlockSpec((tm, d), lambda i, k: (i, 0))],
        out_shape=[jax.ShapeDtypeStruct((m, d), F32), jax.ShapeDtypeStruct((m, d), BF16)],
        scratch_shapes=[pltpu.VMEM((tm, d), F32)],
        compiler_params=_cparams(("parallel", "arbitrary")),
        name="ffn_out",
    )(h_bf, wd, x, g.reshape(1, d), b.reshape(1, d))


def _proj_kernel(x_ref, w_ref, o_ref):
    o_ref[...] = jnp.dot(x_ref[...], w_ref[...], preferred_element_type=F32)


def proj(x_bf, w_bf):
    m, d = x_bf.shape
    n = w_bf.shape[1]
    tm = min(m, 512)
    tn = 128
    for c in (640, 512, 384, 256):
        if n % c == 0:
            tn = c
            break
    return pl.pallas_call(
        _proj_kernel,
        grid=(m // tm, n // tn),
        in_specs=[pl.BlockSpec((tm, d), lambda i, j: (i, 0)),
                  pl.BlockSpec((d, tn), lambda i, j: (0, j))],
        out_specs=pl.BlockSpec((tm, tn), lambda i, j: (i, j)),
        out_shape=jax.ShapeDtypeStruct((m, n), F32),
        compiler_params=_cparams(("parallel", "arbitrary")),
        name="proj",
    )(x_bf, w_bf)


def _log_sigmoid(x):
    return jnp.minimum(x, 0.0) - jnp.log1p(jnp.exp(-jnp.abs(x)))


def _fox_proj_kernel(x_ref, w_ref, bf_ref, qbf_ref, k_ref, v_ref, kbf_ref, vbf_ref, lf_ref):
    x = x_ref[...]
    q = jnp.dot(x, w_ref[:, 0:FOX_WIDTH], preferred_element_type=F32)
    qbf_ref[...] = (q * ATT_SCALE).astype(BF16)
    k = jnp.dot(x, w_ref[:, FOX_WIDTH:2 * FOX_WIDTH], preferred_element_type=F32)
    k_ref[...] = k
    kbf_ref[...] = k.astype(BF16)
    v = jnp.dot(x, w_ref[:, 2 * FOX_WIDTH:3 * FOX_WIDTH], preferred_element_type=F32)
    v_ref[...] = v
    vbf_ref[...] = v.astype(BF16)
    f = jnp.dot(x, w_ref[:, 3 * FOX_WIDTH:3 * FOX_WIDTH + LANES], preferred_element_type=F32)
    lf_ref[...] = _log_sigmoid(f + bf_ref[...])


def fox_proj(x_bf, w_bf, b_f):
    m, d = x_bf.shape
    n = w_bf.shape[1]
    tm = min(m, 256)
    row = lambda width: pl.BlockSpec((tm, width), lambda i: (i, 0))
    bias = jnp.pad(b_f.astype(F32), (0, LANES - FOX_HEADS)).reshape(1, LANES)
    return pl.pallas_call(
        _fox_proj_kernel,
        grid=(m // tm,),
        in_specs=[row(d), pl.BlockSpec((d, n), lambda i: (0, 0)), pl.BlockSpec((1, LANES), lambda i: (0, 0))],
        out_specs=[row(FOX_WIDTH)] * 5 + [row(LANES)],
        out_shape=[jax.ShapeDtypeStruct((m, FOX_WIDTH), BF16),
                   jax.ShapeDtypeStruct((m, FOX_WIDTH), F32), jax.ShapeDtypeStruct((m, FOX_WIDTH), F32),
                   jax.ShapeDtypeStruct((m, FOX_WIDTH), BF16), jax.ShapeDtypeStruct((m, FOX_WIDTH), BF16),
                   jax.ShapeDtypeStruct((m, LANES), F32)],
        compiler_params=_cparams(("parallel",)),
        name="fox_proj",
    )(x_bf, w_bf, bias)


def _split3(x):
    hi = x.astype(BF16)
    r = x - hi.astype(F32)
    mid = r.astype(BF16)
    lo = (r - mid.astype(F32)).astype(BF16)
    return hi, mid, lo


def _seg_sum(x, ones_blk):
    hi, mid, lo = _split3(x)
    d = lambda a: jnp.dot(a, ones_blk, preferred_element_type=F32)
    return d(hi) + d(mid) + d(lo)


def _head_ones():
    r = lax.broadcasted_iota(jnp.int32, (LANES, LANES), 0) // HEAD_DIM
    c = lax.broadcasted_iota(jnp.int32, (LANES, LANES), 1) // HEAD_DIM
    return (r == c).astype(BF16)


def _mix_out_kernel(orw_ref, oc_ref, os_ref, ow_ref, gl_ref, ofox_ref, x_ref, w_ref, gn_ref, gf_ref, g_ref, b_ref,
                    y_ref, ybf_ref):
    ones_blk = _head_ones()
    gate = 1.0 / (1.0 + jnp.exp(-gl_ref[...]))
    pieces = _split3(gate)
    er = lax.broadcasted_iota(jnp.int32, (LANES, NSA_WIDTH), 0)
    ec = lax.broadcasted_iota(jnp.int32, (LANES, NSA_WIDTH), 1) // HEAD_DIM
    onsa = jnp.zeros(oc_ref.shape, F32)
    for c, br_ref in enumerate((oc_ref, os_ref, ow_ref)):
        spread = (er == ec * 3 + c).astype(BF16)
        gate_c = sum(jnp.dot(piece, spread, preferred_element_type=F32) for piece in pieces)
        onsa = onsa + gate_c * br_ref[...]

    def rms(o, gain):
        cols = []
        for c in range(o.shape[1] // LANES):
            blk = o[:, c * LANES:(c + 1) * LANES]
            ms = _seg_sum(blk * blk, ones_blk) * (1.0 / HEAD_DIM)
            cols.append(blk * lax.rsqrt(ms + RMS_EPS) * gain[:, c * LANES:(c + 1) * LANES])
        return jnp.concatenate(cols, axis=1)

    o = jnp.concatenate([orw_ref[...], rms(onsa, gn_ref[...]), rms(ofox_ref[...], gf_ref[...])], axis=1)
    acc = jnp.dot(o.astype(BF16), w_ref[...], preferred_element_type=F32)
    y = _layer_norm(DEEPNORM_ALPHA * x_ref[...] + acc, g_ref[...], b_ref[...])
    y_ref[...] = y
    ybf_ref[...] = y.astype(BF16)


def mix_out(o_rw, o_c, o_s, o_w, gl, o_fox, x, w_bf, gn, gf, g, b):
    m, d = x.shape
    tm = min(m, 256)
    row = lambda width: pl.BlockSpec((tm, width), lambda i: (i, 0))
    full = lambda a, c: pl.BlockSpec((a, c), lambda i: (0, 0))
    return pl.pallas_call(
        _mix_out_kernel,
        grid=(m // tm,),
        in_specs=[row(RW_WIDTH), row(NSA_WIDTH), row(NSA_WIDTH), row(NSA_WIDTH), row(LANES), row(FOX_WIDTH),
                  row(d), full(d, d), full(1, NSA_WIDTH), full(1, FOX_WIDTH), full(1, d), full(1, d)],
        out_specs=[row(d), row(d)],
        out_shape=[jax.ShapeDtypeStruct((m, d), F32), jax.ShapeDtypeStruct((m, d), BF16)],
        compiler_params=_cparams(("parallel",)),
        name="mix_out",
    )(o_rw, o_c, o_s, o_w, gl, o_fox, x, w_bf, gn.reshape(1, -1), gf.reshape(1, -1), g.reshape(1, d),
      b.reshape(1, d))


CUM_BLOCK = 512


def _cumsum_kernel(x_ref, o_ref, carry_ref):
    @pl.when(pl.program_id(0) == 0)
    def _():
        carry_ref[...] = jnp.zeros_like(carry_ref)

    r = lax.broadcasted_iota(jnp.int32, (CUM_BLOCK, CUM_BLOCK), 0)
    c = lax.broadcasted_iota(jnp.int32, (CUM_BLOCK, CUM_BLOCK), 1)
    upper = (r <= c).astype(BF16)
    hi, mid, lo = _split3(x_ref[...])
    d = lambda a: jnp.dot(a, upper, preferred_element_type=F32)
    cs = d(hi) + d(mid) + d(lo) + carry_ref[:, 0:1]
    o_ref[...] = cs
    carry_ref[...] = jnp.broadcast_to(cs[:, CUM_BLOCK - 1:CUM_BLOCK], carry_ref.shape)


def cumsum_lanes(x):
    rows, t = x.shape
    return pl.pallas_call(
        _cumsum_kernel,
        grid=(t // CUM_BLOCK,),
        in_specs=[pl.BlockSpec((rows, CUM_BLOCK), lambda i: (0, i))],
        out_specs=pl.BlockSpec((rows, CUM_BLOCK), lambda i: (0, i)),
        out_shape=jax.ShapeDtypeStruct((rows, t), F32),
        scratch_shapes=[pltpu.VMEM((rows, LANES), F32)],
        compiler_params=_cparams(("arbitrary",)),
        name="cumsum",
    )(x)


def _fox_flash_kernel(qt_ref, kt_ref, q_ref, k_ref, v_ref, ck_ref, o_ref, m_ref, l_ref, acc_ref, *, tq, tk):
    step = pl.program_id(1)
    qi = qt_ref[step]
    ki = kt_ref[step]

    @pl.when(ki == 0)
    def _():
        m_ref[...] = jnp.full_like(m_ref, NEG)
        l_ref[...] = jnp.zeros_like(l_ref)
        acc_ref[...] = jnp.zeros_like(acc_ref)

    left = lax.broadcasted_iota(jnp.int32, (1, LANES), 1) < HEAD_DIM

    def tile(diagonal):
        q = q_ref[...]
        k = k_ref[...]
        v = v_ref[...]
        pv, alphas = [], []
        for hh in range(2):
            qm = jnp.where(left if hh == 0 else jnp.logical_not(left), q, jnp.zeros_like(q))
            s = lax.dot_general(qm, k, (((1,), (1,)), ((), ())), preferred_element_type=F32)
            s = s - ck_ref[0, hh:hh + 1, :]
            if diagonal:
                rows = lax.broadcasted_iota(jnp.int32, (tq, tk), 0)
                cols = lax.broadcasted_iota(jnp.int32, (tq, tk), 1)
                s = jnp.where(rows >= cols, s, NEG)
            m_old = m_ref[hh]
            m_new = jnp.maximum(m_old, jnp.max(s, axis=-1, keepdims=True))
            p = jnp.exp(s - m_new)
            alpha = jnp.exp(m_old - m_new)
            l_ref[hh] = alpha * l_ref[hh] + jnp.sum(p, axis=-1, keepdims=True)
            m_ref[hh] = m_new
            pv.append(jnp.dot(p.astype(BF16), v, preferred_element_type=F32))
            alphas.append(alpha)
        acc_ref[...] = (jnp.where(left, alphas[0], alphas[1]) * acc_ref[...]
                        + jnp.where(left, pv[0], pv[1]))

    @pl.when(ki < qi)
    def _():
        tile(False)

    @pl.when(ki == qi)
    def _():
        tile(True)
        o_ref[...] = acc_ref[...] / jnp.where(left, l_ref[0], l_ref[1])


def _tri_steps(n):
    qs, ks = [], []
    for qi in range(n):
        for ki in range(qi + 1):
            qs.append(qi)
            ks.append(ki)
    return jnp.asarray(qs, jnp.int32), jnp.asarray(ks, jnp.int32)


def fox_flash(q_bf, k_bf, v_bf, ck, t):
    tq = tk = min(t, 512)
    n = t // tq
    qt, kt = _tri_steps(n)
    pairs = FOX_WIDTH // LANES
    grid_spec = pltpu.PrefetchScalarGridSpec(
        num_scalar_prefetch=2,
        grid=(pairs, qt.shape[0]),
        in_specs=[pl.BlockSpec((tq, LANES), lambda p, s, qt, kt: (qt[s], p)),
                  pl.BlockSpec((tk, LANES), lambda p, s, qt, kt: (kt[s], p)),
                  pl.BlockSpec((tk, LANES), lambda p, s, qt, kt: (kt[s], p)),
                  pl.BlockSpec((1, 8, tk), lambda p, s, qt, kt: (p, 0, kt[s]))],
        out_specs=pl.BlockSpec((tq, LANES), lambda p, s, qt, kt: (qt[s], p)),
        scratch_shapes=[pltpu.VMEM((2, tq, 1), F32), pltpu.VMEM((2, tq, 1), F32), pltpu.VMEM((tq, LANES), F32)],
    )
    return pl.pallas_call(
        functools.partial(_fox_flash_kernel, tq=tq, tk=tk),
        grid_spec=grid_spec,
        out_shape=jax.ShapeDtypeStruct((t, FOX_WIDTH), F32),
        compiler_params=_cparams(("parallel", "arbitrary")),
        name="fox_flash",
    )(qt, kt, q_bf, k_bf, v_bf, ck)


def fox_prompt_attn(qbf, kbf, vbf, logf128, t):
    lf = logf128[:t, :FOX_HEADS]
    lf_t = jnp.pad(lf.T.reshape(FOX_HEADS // 2, 2, t), ((0, 0), (0, 6), (0, 0))).reshape(-1, t)
    ck = cumsum_lanes(lf_t).reshape(FOX_HEADS // 2, 8, t)
    return fox_flash(qbf, kbf, vbf, ck, t)


T5_THRESH = (21, 27, 35, 46, 59, 77, 99, 128, 166, 216, 280, 363, 470, 609, 790)
NSA_TILE = 512
NSA_DELTAS = 4
NOT_ALLOWED = -1e30
REMOVED = -2e30


def _t5_bucket_int(dist):
    d = jnp.maximum(dist, 0)
    large = jnp.full(d.shape, N_BUCKETS // 2, jnp.int32)
    for th in T5_THRESH:
        large = large + (d >= th).astype(jnp.int32)
    return jnp.where(d < N_BUCKETS // 2, d, large)


def _bias_lookup(bucket, tab_ref, h):
    val = jnp.full(bucket.shape, tab_ref[0, h], F32)
    for b in range(1, N_BUCKETS):
        val = jnp.where(bucket == b, tab_ref[b, h], val)
    return val


def _bias_tab_kernel(tab_ref, o_ref, *, tb):
    h = pl.program_id(0)
    dl = pl.program_id(1)
    r = lax.broadcasted_iota(jnp.int32, (tb, tb), 0)
    c = lax.broadcasted_iota(jnp.int32, (tb, tb), 1)
    o_ref[0, 0] = _bias_lookup(_t5_bucket_int(dl * tb + r - c), tab_ref, h)


def bias_tiles(rel_bias, tb):
    return pl.pallas_call(
        functools.partial(_bias_tab_kernel, tb=tb),
        grid=(NSA_HEADS, NSA_DELTAS),
        in_specs=[pl.BlockSpec(memory_space=pltpu.SMEM)],
        out_specs=pl.BlockSpec((1, 1, tb, tb), lambda h, d: (h, d, 0, 0)),
        out_shape=jax.ShapeDtypeStruct((NSA_HEADS, NSA_DELTAS, tb, tb), F32),
        compiler_params=_cparams(("parallel", "parallel")),
        name="nsa_bias_tiles",
    )(rel_bias)


NSA_PROJ_COLS = 2432


def nsa_proj_weight(w):
    q, kc, vc, ks, vs, kw, vw, gl = jnp.split(w, NSA_SPLITS, axis=-1)
    dup = []
    for src in (ks, vs, kw, vw):
        for g in range(NSA_KV_HEADS):
            blk = src[:, g * HEAD_DIM:(g + 1) * HEAD_DIM]
            dup += [blk, blk]
    return jnp.concatenate([q, kc, vc, ks, vs, kw, vw, _pad_cols(gl, LANES)] + dup, axis=1).astype(BF16)


def _nsa_proj_kernel(x_ref, w_ref, qbf_ref, kv_ref, kwv_ref, gl_ref, dup_ref):
    x = x_ref[...]
    d = lambda a, b: jnp.dot(x, w_ref[:, a:b], preferred_element_type=F32)
    qbf_ref[...] = (d(0, 512) * ATT_SCALE).astype(BF16)
    kv_ref[...] = d(512, 1024)
    kwv_ref[...] = d(1024, 1280)
    gl_ref[...] = d(1280, 1408)
    dup_ref[...] = d(1408, 2432).astype(BF16)


def nsa_proj(x_bf, w_bf):
    m, dm = x_bf.shape
    tm = min(m, 256)
    row = lambda width: pl.BlockSpec((tm, width), lambda i: (i, 0))
    return pl.pallas_call(
        _nsa_proj_kernel,
        grid=(m // tm,),
        in_specs=[row(dm), pl.BlockSpec((dm, NSA_PROJ_COLS), lambda i: (0, 0))],
        out_specs=[row(512), row(512), row(256), row(LANES), row(1024)],
        out_shape=[jax.ShapeDtypeStruct((m, 512), BF16), jax.ShapeDtypeStruct((m, 512), F32),
                   jax.ShapeDtypeStruct((m, 256), F32), jax.ShapeDtypeStruct((m, LANES), F32),
                   jax.ShapeDtypeStruct((m, 1024), BF16)],
        compiler_params=_cparams(("parallel",)),
        name="nsa_proj",
    )(x_bf, w_bf)


def nsa_compress_weight(cmp_w, cmp_b):
    w = cmp_w.reshape(2, 2, CMP_STRIDE, HEAD_DIM, HEAD_DIM)
    eye = jnp.eye(2, dtype=F32)
    t = jnp.einsum('whjde,wv,gk->jwgdhvke', w, eye, eye)
    t = jnp.broadcast_to(t[..., None, :], t.shape[:-1] + (2, HEAD_DIM))
    wmat = t.reshape(CMP_STRIDE * 256, 1024).astype(BF16)
    bias = jnp.broadcast_to(cmp_b[:, None, None, :], (2, NSA_KV_HEADS, 2, HEAD_DIM)).reshape(1, 512)
    return wmat, bias.astype(F32)


def _nsa_compress_kernel(x_ref, w_ref, b_ref, o_ref, acc_ref):
    j = pl.program_id(0)

    @pl.when(j == 0)
    def _():
        acc_ref[...] = jnp.zeros_like(acc_ref)

    acc_ref[...] += jnp.dot(x_ref[...].astype(BF16), w_ref[...], preferred_element_type=F32)

    @pl.when(j == pl.num_programs(0) - 1)
    def _():
        n = acc_ref.shape[0]
        first = acc_ref[:, 0:512]
        second = pltpu.roll(acc_ref[:, 512:1024], n - 1, axis=0)
        o_ref[...] = (first + second + b_ref[...]).astype(BF16)


def nsa_compress_rows(kv4, wmat, bias):
    t = kv4.shape[0]
    n = t // CMP_STRIDE
    x = kv4.reshape(n, CMP_STRIDE * 512)
    return pl.pallas_call(
        _nsa_compress_kernel,
        grid=(CMP_STRIDE,),
        in_specs=[pl.BlockSpec((n, 256), lambda j: (0, 2 * j)),
                  pl.BlockSpec((256, 1024), lambda j: (j, 0)),
                  pl.BlockSpec((1, 512), lambda j: (0, 0))],
        out_specs=pl.BlockSpec((n, 512), lambda j: (0, 0)),
        out_shape=jax.ShapeDtypeStruct((n, 512), BF16),
        scratch_shapes=[pltpu.VMEM((n, 1024), F32)],
        compiler_params=_cparams(("arbitrary",)),
        name="nsa_compress",
    )(x, wmat, bias)


def _nsa_cmp_kernel(tab_ref, q_ref, kv_ref, oc_ref, sel_ref, *, tq, ncp, n_sel):
    t0 = pl.program_id(0) * tq
    rows = lax.broadcasted_iota(jnp.int32, (tq, ncp), 0) + t0
    cols = lax.broadcasted_iota(jnp.int32, (tq, ncp), 1)
    dist = rows - (cols * CMP_STRIDE + (CMP_BLOCK - 1))
    valid = dist >= 0
    bucket = _t5_bucket_int(dist)
    left = lax.broadcasted_iota(jnp.int32, (1, LANES), 1) < HEAD_DIM
    ci = lax.broadcasted_iota(jnp.int32, (ncp, LANES), 0) * CMP_STRIDE
    cj = lax.broadcasted_iota(jnp.int32, (ncp, LANES), 1)
    cover = jnp.where((ci < (cj + 1) * SEL_BLOCK) & (ci + CMP_BLOCK > cj * SEL_BLOCK), 1.0, 0.0).astype(BF16)
    imp = [jnp.zeros((tq, LANES), F32) for _ in range(NSA_KV_HEADS)]
    for p in range(NSA_HEADS // 2):
        g = p // (NSA_REP // 2)
        qp = q_ref[:, p * LANES:(p + 1) * LANES]
        kc = kv_ref[:, g * LANES:(g + 1) * LANES]
        vc = kv_ref[:, (NSA_KV_HEADS + g) * LANES:(NSA_KV_HEADS + g + 1) * LANES]
        outs = []
        for hh in range(2):
            qm = jnp.where(left if hh == 0 else jnp.logical_not(left), qp, jnp.zeros_like(qp))
            s = lax.dot_general(qm, kc, (((1,), (1,)), ((), ())), preferred_element_type=F32)
            s = jnp.where(valid, s + _bias_lookup(bucket, tab_ref, 2 * p + hh), NEG)
            m = jnp.max(s, axis=-1, keepdims=True)
            e = jnp.where(valid, jnp.exp(s - m), 0.0)
            pr = (e / jnp.maximum(jnp.sum(e, axis=-1, keepdims=True), 1e-30)).astype(BF16)
            outs.append(jnp.dot(pr, vc, preferred_element_type=F32))
            imp[g] = imp[g] + jnp.dot(pr, cover, preferred_element_type=F32)
        oc_ref[:, p * LANES:(p + 1) * LANES] = jnp.where(left, outs[0], outs[1])

    lane = lax.broadcasted_iota(jnp.int32, (tq, LANES), 1)
    lane_f = lane.astype(F32)
    jq = (lax.broadcasted_iota(jnp.int32, (tq, LANES), 0) + t0) // SEL_BLOCK
    forced = (lane == 0) | (lane == jq) | (lane == jq - 1)
    allowed = lane <= jq
    for g in range(NSA_KV_HEADS):
        work = jnp.where(allowed, imp[g] + jnp.where(forced, SEL_FORCE, 0.0), NOT_ALLOWED)
        sel = jnp.zeros((tq, LANES), F32)
        for _ in range(n_sel):
            mx = jnp.max(work, axis=-1, keepdims=True)
            idx = jnp.min(jnp.where(work == mx, lane_f, float(LANES)), axis=-1, keepdims=True)
            hit = lane_f == idx
            sel = jnp.where(hit & (mx > 0.5 * NOT_ALLOWED), 1.0, sel)
            work = jnp.where(hit, REMOVED, work)
        sel_ref[:, g * LANES:(g + 1) * LANES] = sel.astype(BF16)


def nsa_cmp_select(qbf, kvc, rel_bias, t):
    tq = min(t, 256)
    ncp = kvc.shape[0]
    n_sel = min(N_SELECT, t // SEL_BLOCK)
    return pl.pallas_call(
        functools.partial(_nsa_cmp_kernel, tq=tq, ncp=ncp, n_sel=n_sel),
        grid=(t // tq,),
        in_specs=[pl.BlockSpec(memory_space=pltpu.SMEM),
                  pl.BlockSpec((tq, 512), lambda i: (i, 0)),
                  pl.BlockSpec((ncp, 512), lambda i: (0, 0))],
        out_specs=[pl.BlockSpec((tq, 512), lambda i: (i, 0)), pl.BlockSpec((tq, 256), lambda i: (i, 0))],
        out_shape=[jax.ShapeDtypeStruct((t, 512), F32), jax.ShapeDtypeStruct((t, 256), BF16)],
        compiler_params=_cparams(("parallel",)),
        name="nsa_cmp_select",
    )(rel_bias, qbf, kvc)


def _nsa_flash_kernel(qt_ref, kt_ref, ft_ref, q_ref, k_ref, v_ref, b_ref, *rest, tq, tk, selected):
    if selected:
        sm_ref, ex_ref, o_ref, m_ref, l_ref, acc_ref = rest
    else:
        o_ref, m_ref, l_ref, acc_ref = rest
    step = pl.program_id(1)
    qi = qt_ref[step]
    ki = kt_ref[step]

    @pl.when(ft_ref[step] == 1)
    def _():
        m_ref[...] = jnp.full_like(m_ref, NEG)
        l_ref[...] = jnp.zeros_like(l_ref)
        acc_ref[...] = jnp.zeros_like(acc_ref)

    left = lax.broadcasted_iota(jnp.int32, (1, LANES), 1) < HEAD_DIM

    def tile(diagonal):
        q = q_ref[...]
        k = k_ref[...]
        v = v_ref[...]
        rows = lax.broadcasted_iota(jnp.int32, (tq, tk), 0)
        cols = lax.broadcasted_iota(jnp.int32, (tq, tk), 1)
        if selected:
            valid = jnp.dot(sm_ref[...], ex_ref[...], preferred_element_type=F32) > 0.5
            if diagonal:
                valid = valid & (rows >= cols)
        else:
            valid = (rows >= cols) if diagonal else (rows < cols)
        pv, alphas = [], []
        for hh in range(2):
            qm = jnp.where(left if hh == 0 else jnp.logical_not(left), q, jnp.zeros_like(q))
            s = lax.dot_general(qm, k, (((1,), (1,)), ((), ())), preferred_element_type=F32)
            s = jnp.where(valid, s + b_ref[hh, 0], NEG)
            m_old = m_ref[hh]
            m_new = jnp.maximum(m_old, jnp.max(s, axis=-1, keepdims=True))
            p = jnp.where(valid, jnp.exp(s - m_new), 0.0)
            alpha = jnp.exp(m_old - m_new)
            l_ref[hh] = alpha * l_ref[hh] + jnp.sum(p, axis=-1, keepdims=True)
            m_ref[hh] = m_new
            pv.append(jnp.dot(p.astype(BF16), v, preferred_element_type=F32))
            alphas.append(alpha)
        acc_ref[...] = (jnp.where(left, alphas[0], alphas[1]) * acc_ref[...]
                        + jnp.where(left, pv[0], pv[1]))

    @pl.when(ki < qi)
    def _():
        tile(False)

    @pl.when(ki == qi)
    def _():
        tile(True)
        o_ref[...] = acc_ref[...] / jnp.where(left, l_ref[0], l_ref[1])


def nsa_flash(qbf, dup, btab, t, selmask=None):
    selected = selmask is not None
    tq = tk = btab.shape[-1]
    n = t // tq
    qs, ks, fs = [], [], []
    for qi in range(n):
        lo = 0 if selected else max(qi - (WINDOW // tk), 0)
        for ki in range(lo, qi + 1):
            qs.append(qi)
            ks.append(ki)
            fs.append(1 if ki == lo else 0)
    qt, kt, ft = (jnp.asarray(a, jnp.int32) for a in (qs, ks, fs))
    kcol = 0 if selected else 4
    half = NSA_REP // 2
    in_specs = [pl.BlockSpec((tq, LANES), lambda p, s, qt, kt, ft: (qt[s], p)),
                pl.BlockSpec((tk, LANES), lambda p, s, qt, kt, ft: (kt[s], kcol + p // half)),
                pl.BlockSpec((tk, LANES), lambda p, s, qt, kt, ft: (kt[s], kcol + 2 + p // half)),
                pl.BlockSpec((2, 1, tq, tk),
                             lambda p, s, qt, kt, ft: (p, jnp.minimum(qt[s] - kt[s], NSA_DELTAS - 1), 0, 0))]
    args = [qbf, dup, dup, btab]
    if selected:
        n_blk = tk // SEL_BLOCK
        jj = jnp.arange(LANES)[:, None]
        ll = jnp.arange(t)[None, :]
        expand = (jj == ll // SEL_BLOCK).astype(BF16)
        in_specs += [pl.BlockSpec((tq, LANES), lambda p, s, qt, kt, ft: (qt[s], p // half)),
                     pl.BlockSpec((LANES, tk), lambda p, s, qt, kt, ft: (0, kt[s]))]
        args += [selmask, expand]
    grid_spec = pltpu.PrefetchScalarGridSpec(
        num_scalar_prefetch=3,
        grid=(NSA_HEADS // 2, len(qs)),
        in_specs=in_specs,
        out_specs=pl.BlockSpec((tq, LANES), lambda p, s, qt, kt, ft: (qt[s], p)),
        scratch_shapes=[pltpu.VMEM((2, tq, 1), F32), pltpu.VMEM((2, tq, 1), F32), pltpu.VMEM((tq, LANES), F32)],
    )
    return pl.pallas_call(
        functools.partial(_nsa_flash_kernel, tq=tq, tk=tk, selected=selected),
        grid_spec=grid_spec,
        out_shape=jax.ShapeDtypeStruct((t, NSA_WIDTH), F32),
        compiler_params=_cparams(("parallel", "arbitrary")),
        name="nsa_sel_flash" if selected else "nsa_win_flash",
    )(qt, kt, ft, *args)


def nsa_prompt_attn(qbf, kv4, dup, wmat, cbias, rel_bias, btab, t):
    kvc = nsa_compress_rows(kv4[:t], wmat, cbias)
    o_c, selmask = nsa_cmp_select(qbf, kvc, rel_bias, t)
    o_s = nsa_flash(qbf, dup, btab, t, selmask)
    o_w = nsa_flash(qbf, dup, btab, t)
    return o_c, o_s, o_w


RW_CHUNK = 128
RW_PAIRS = RW_WIDTH // LANES


def _softplus(x):
    return jnp.maximum(x, 0.0) + jnp.log1p(jnp.exp(-jnp.abs(x)))


def _sigmoid(x):
    return 1.0 / (1.0 + jnp.exp(-x))


def _seg_sum_wide(x, ones_blk):
    return jnp.concatenate([_seg_sum(x[:, c * LANES:(c + 1) * LANES], ones_blk)
                            for c in range(x.shape[1] // LANES)], axis=1)


def _rwkv_features(p, shifted, mu, w0, w2p, a0, a2p, g2, k_k, k_a, ones_blk):
    z = p + (shifted - p) * mu
    r = z[:, 0:RW_WIDTH]
    k = z[:, RW_WIDTH:2 * RW_WIDTH]
    v = z[:, 2 * RW_WIDTH:3 * RW_WIDTH]
    lora = z[:, 3 * RW_WIDTH:3 * RW_WIDTH + LANES]
    gd = z[:, 3 * RW_WIDTH + LANES:]
    w_log = -_softplus(-(w0 + jnp.dot(jnp.tanh(lora).astype(BF16), w2p, preferred_element_type=F32))) - 0.5
    decay = jnp.exp(-jnp.exp(w_log))
    a = _sigmoid(a0 + jnp.dot(lora.astype(BF16), a2p, preferred_element_type=F32))
    g = jnp.dot(_sigmoid(gd).astype(BF16), g2, preferred_element_type=F32)
    kk = k * k_k
    kk = kk / jnp.maximum(jnp.sqrt(_seg_sum_wide(kk * kk, ones_blk)), 1e-12)
    kh = k * (1.0 + (a - 1.0) * k_a)
    return r, decay, kh, v, kk, kk * a, g


def _rwkv_finish(out, r, kh, v, g, r_k, gn_g, gn_b, ones_blk):
    m = _seg_sum_wide(out, ones_blk) * (1.0 / HEAD_DIM)
    d = out - m
    var = _seg_sum_wide(d * d, ones_blk) * (1.0 / HEAD_DIM)
    y = d * lax.rsqrt(var + RW_GN_EPS) * gn_g + gn_b
    bonus = _seg_sum_wide(r * kh * r_k, ones_blk) * v
    return (y + bonus) * g


def _rwkv_kernel(p_ref, mu_ref, w0_ref, w2_ref, a0_ref, a2_ref, g2_ref, kk_ref, ka_ref, rk_ref, gng_ref, gnb_ref,
                 y_ref, sout_ref, s_ref, prev_ref, rows_ref, vt_ref, ot_ref):
    n = RW_CHUNK
    c = pl.program_id(0)

    @pl.when(c == 0)
    def _():
        s_ref[...] = jnp.zeros_like(s_ref)
        prev_ref[...] = jnp.zeros_like(prev_ref)

    p = p_ref[...]
    first = lax.broadcasted_iota(jnp.int32, (n, 1), 0) == 0
    shifted = jnp.where(first, prev_ref[0:1, :], pltpu.roll(p, 1, axis=0))
    prev_ref[...] = jnp.broadcast_to(p[n - 1:n, :], prev_ref.shape)
    ones_blk = _head_ones()
    r, decay, kh, v, kk, kka, g = _rwkv_features(p, shifted, mu_ref[...], w0_ref[...], w2_ref[...], a0_ref[...],
                                                 a2_ref[...], g2_ref[...], kk_ref[...], ka_ref[...], ones_blk)
    for qi, arr in enumerate((r, decay, kh, kk, kka)):
        rows_ref[qi, 0] = arr
        rows_ref[qi, 1] = pltpu.roll(arr, RW_WIDTH - HEAD_DIM, axis=1)
    for pr in range(RW_PAIRS):
        vt_ref[pr] = v[:, pr * LANES:(pr + 1) * LANES].T
    ot_ref[...] = jnp.zeros_like(ot_ref)
    lane_t = lax.broadcasted_iota(jnp.int32, (HEAD_DIM, n), 1)

    def group(j, carry):
        base = pl.multiple_of(j * 8, 8)
        for h in range(RW_HEADS):
            pr, par = h // 2, h % 2
            lo, ro = pr * LANES, par * HEAD_DIM
            blk = [rows_ref[q, par, pl.ds(base, 8), lo:lo + HEAD_DIM] for q in range(5)]
            s = s_ref[h]
            ot = ot_ref[pr, ro:ro + HEAD_DIM, :]
            vt = vt_ref[pr, ro:ro + HEAD_DIM, :]
            for i in range(8):
                hit = lane_t == base + i
                rv = lambda q: blk[q][i:i + 1, :]
                sa = -jnp.sum(s * rv(3), axis=1, keepdims=True)
                vcol = jnp.sum(jnp.where(hit, vt, 0.0), axis=1, keepdims=True)
                s = s * rv(1) + sa * rv(4) + vcol * rv(2)
                ocol = jnp.sum(s * rv(0), axis=1, keepdims=True)
                ot = jnp.where(hit, ocol, ot)
            s_ref[h] = s
            ot_ref[pr, ro:ro + HEAD_DIM, :] = ot
        return carry

    lax.fori_loop(0, n // 8, group, 0)
    out = jnp.concatenate([ot_ref[pr].T for pr in range(RW_PAIRS)], axis=1)
    y_ref[...] = _rwkv_finish(out, r, kh, v, g, rk_ref[...], gng_ref[...], gnb_ref[...], ones_blk)

    @pl.when(c == pl.num_programs(0) - 1)
    def _():
        sout_ref[...] = s_ref[...]


def _rwkv_params(mu, w0, w2, a0, a2, g2, k_k, k_a, r_k, gn_g, gn_b):
    row = lambda a: a.reshape(1, -1).astype(F32)
    zeros = jnp.zeros((DECAY_LORA, RW_WIDTH), F32)
    w2p = jnp.concatenate([w2, zeros], axis=0).astype(BF16)
    a2p = jnp.concatenate([zeros, a2], axis=0).astype(BF16)
    return (row(mu), row(w0), w2p, row(a0), a2p, g2.astype(BF16), row(k_k), row(k_a), row(r_k), row(gn_g), row(gn_b))


def rwkv_prompt(p_rw, t, params):
    n = RW_CHUNK
    args = _rwkv_params(*params)
    full = lambda a: pl.BlockSpec(a.shape, lambda i: (0,) * a.ndim)
    return pl.pallas_call(
        _rwkv_kernel,
        grid=(t // n,),
        in_specs=[pl.BlockSpec((n, RW_COLS), lambda i: (i, 0))] + [full(a) for a in args],
        out_specs=[pl.BlockSpec((n, RW_WIDTH), lambda i: (i, 0)),
                   pl.BlockSpec((RW_HEADS, HEAD_DIM, HEAD_DIM), lambda i: (0, 0, 0))],
        out_shape=[jax.ShapeDtypeStruct((t, RW_WIDTH), F32),
                   jax.ShapeDtypeStruct((RW_HEADS, HEAD_DIM, HEAD_DIM), F32)],
        scratch_shapes=[pltpu.VMEM((RW_HEADS, HEAD_DIM, HEAD_DIM), F32),
                        pltpu.VMEM((8, RW_COLS), F32),
                        pltpu.VMEM((5, 2, n, RW_WIDTH), F32),
                        pltpu.VMEM((RW_PAIRS, LANES, n), F32),
                        pltpu.VMEM((RW_PAIRS, LANES, n), F32)],
        compiler_params=_cparams(("arbitrary",)),
        name="rwkv_prompt",
    )(p_rw, *args)


def masked_softmax(logits, mask):
    s = jnp.where(mask, logits, -jnp.inf)
    m = jnp.max(s, axis=-1, keepdims=True)
    m = jnp.where(jnp.isfinite(m), m, 0.0)
    e = jnp.where(mask, jnp.exp(s - m), 0.0)
    return e / jnp.maximum(jnp.sum(e, -1, keepdims=True), 1e-30)


def t5_bucket(dist):
    d = jnp.maximum(dist, 0)
    max_exact = N_BUCKETS // 2
    df = jnp.maximum(d, 1).astype(jnp.float32)
    large = max_exact + (jnp.log(df / max_exact) / math.log(MAX_DISTANCE / max_exact)
                         * (N_BUCKETS - max_exact)).astype(jnp.int32)
    large = jnp.minimum(large, N_BUCKETS - 1)
    return jnp.where(d < max_exact, d, large)


def rwkv_mix(p_rw, p_prev, s0, mu, w0, w2, a0, a2, g2, k_k, k_a, r_k, gn_g, gn_b):
    f32 = jnp.float32
    B, T = p_rw.shape[:2]
    shifted = jnp.concatenate([p_prev[:, None].astype(p_rw.dtype), p_rw[:, :-1]], axis=1)
    z = p_rw + (shifted - p_rw) * mu
    r, k, v, wd, ad, gd = jnp.split(z, RW_SPLITS, axis=-1)
    heads = lambda t: t.astype(f32).reshape(B, T, RW_HEADS, HEAD_DIM)
    w_log = -jax.nn.softplus(-(w0 + jnp.tanh(wd) @ w2).astype(f32)) - 0.5
    decay = heads(jnp.exp(-jnp.exp(w_log)))
    a = heads(jax.nn.sigmoid((a0 + ad @ a2).astype(f32)))
    g = jax.nn.sigmoid(gd) @ g2
    kk = heads(k * k_k)
    kk = kk / jnp.maximum(jnp.linalg.norm(kk, axis=-1, keepdims=True), 1e-12)
    kh = heads(k) * (1.0 + (a - 1.0) * k_a.astype(f32).reshape(RW_HEADS, HEAD_DIM))
    rh, vh = heads(r), heads(v)

    def step(S, inp):
        r_t, w_t, k_t, v_t, kk_t, a_t = inp
        sa = jnp.einsum('bhvk,bhk->bhv', S, -kk_t)
        S = (S * w_t[:, :, None, :] + sa[..., None] * (kk_t * a_t)[:, :, None, :]
             + v_t[..., None] * k_t[:, :, None, :])
        return S, jnp.einsum('bhvk,bhk->bhv', S, r_t)

    xs = tuple(jnp.swapaxes(t, 0, 1) for t in (rh, decay, kh, vh, kk, a))
    s_T, out = lax.scan(step, s0.astype(f32), xs)
    out = jnp.swapaxes(out, 0, 1)
    m = jnp.mean(out, -1, keepdims=True)
    var = jnp.mean(jnp.square(out - m), -1, keepdims=True)
    y = ((out - m) * lax.rsqrt(var + RW_GN_EPS)).reshape(B, T, RW_WIDTH) * gn_g + gn_b
    bonus = (jnp.sum(rh * kh * r_k, -1, keepdims=True) * vh).reshape(B, T, RW_WIDTH)
    y = (y + bonus) * g
    return y.astype(p_rw.dtype), s_T, p_rw[:, -1]


def nsa_split(p_nsa):
    B, T = p_nsa.shape[:2]
    q, kc, vc, ks, vs, kw, vw, gl = jnp.split(p_nsa, NSA_SPLITS, axis=-1)
    kvh = lambda t: t.reshape(B, T, NSA_KV_HEADS, HEAD_DIM)
    return (q.reshape(B, T, NSA_HEADS, HEAD_DIM), kvh(kc), kvh(vc), kvh(ks), kvh(vs),
            kvh(kw), kvh(vw), gl.reshape(B, T, NSA_HEADS, 3))


def nsa_compress(k, w, b):
    B, L = k.shape[:2]
    nc = (L - CMP_BLOCK) // CMP_STRIDE + 1
    ch = k[:, :(nc + 1) * CMP_STRIDE].reshape(B, nc + 1, CMP_STRIDE, NSA_KV_HEADS, HEAD_DIM)
    first = jnp.einsum('bcjgd,jde->bcge', ch, w[:CMP_STRIDE])
    second = jnp.einsum('bcjgd,jde->bcge', ch, w[CMP_STRIDE:])
    return first[:, :-1] + second[:, 1:] + b


def sel_blocks(k, ns):
    B, L = k.shape[:2]
    k = jnp.pad(k, ((0, 0), (0, ns * SEL_BLOCK - L), (0, 0), (0, 0)))
    return k.reshape(B, ns, SEL_BLOCK, NSA_KV_HEADS, HEAD_DIM).transpose(0, 3, 1, 2, 4)


def cover_matrix(nc, ns):
    start = jnp.arange(nc)[:, None] * CMP_STRIDE
    j = jnp.arange(ns)[None, :]
    return ((start < (j + 1) * SEL_BLOCK) & (start + CMP_BLOCK > j * SEL_BLOCK)).astype(jnp.float32)


def nsa_context(kc_raw, vc_raw, ks, vs, cmp_w, cmp_b):
    L = kc_raw.shape[1]
    kc = nsa_compress(kc_raw, cmp_w[0], cmp_b[0])
    vc = nsa_compress(vc_raw, cmp_w[1], cmp_b[1])
    nc = kc.shape[1]
    c_end = jnp.arange(nc) * CMP_STRIDE + CMP_BLOCK - 1
    ns = -(-L // SEL_BLOCK)
    return kc, vc, c_end, sel_blocks(ks, ns), sel_blocks(vs, ns), cover_matrix(nc, ns)


def nsa_attend(q, gates, tq, kc, vc, c_end, kb, vb, cover, kw, vw, pw, rel_bias):
    f32 = jnp.float32
    B, Tq = q.shape[:2]
    G, R = NSA_KV_HEADS, NSA_REP
    qg = q.reshape(B, Tq, G, R, HEAD_DIM)

    def bias_2d(dist):
        return rel_bias[t5_bucket(dist)].reshape(dist.shape + (G, R)).transpose(2, 3, 0, 1).astype(f32)

    dist_c = tq[:, None] - c_end[None, :]
    lc = jnp.einsum('btgrd,bigd->bgrti', qg, kc).astype(f32) * ATT_SCALE + bias_2d(dist_c)
    p_c = masked_softmax(lc, dist_c >= 0)
    o_c = jnp.einsum('bgrti,bigd->btgrd', p_c.astype(vc.dtype), vc)

    ns = kb.shape[2]
    imp = jnp.einsum('bgrti,ij->btgj', p_c, cover)
    jq = tq // SEL_BLOCK
    jb = jnp.arange(ns)[None, :]
    forced = (jb == 0) | (jb == jq[:, None]) | (jb == jq[:, None] - 1)
    allowed = jb <= jq[:, None]
    score = jnp.where(allowed[None, :, None, :],
                      imp + jnp.where(forced, SEL_FORCE, 0.0)[None, :, None, :], -jnp.inf)
    top_v, top_i = lax.top_k(score, min(N_SELECT, ns))
    top_i = top_i.transpose(0, 2, 1, 3)
    valid = jnp.isfinite(top_v).transpose(0, 2, 1, 3)
    n = top_i.shape[-1]
    bi = jnp.arange(B)[:, None, None, None]
    gi = jnp.arange(G)[None, :, None, None]
    ksel = kb[bi, gi, top_i]
    vsel = vb[bi, gi, top_i]
    pos = top_i[..., None] * SEL_BLOCK + jnp.arange(SEL_BLOCK)
    dist_s = tq[None, None, :, None, None] - pos
    table_g = rel_bias.reshape(N_BUCKETS, G, R).transpose(1, 0, 2)
    bias_s = jnp.moveaxis(table_g[gi[..., None], t5_bucket(dist_s)], -1, 2).astype(f32)
    ls = jnp.einsum('btgrd,bgtnkd->bgrtnk', qg, ksel).astype(f32) * ATT_SCALE + bias_s
    ok = ((dist_s >= 0) & valid[..., None])[:, :, None]
    p_s = masked_softmax(ls.reshape(B, G, R, Tq, n * SEL_BLOCK), ok.reshape(B, G, 1, Tq, n * SEL_BLOCK))
    o_s = jnp.einsum('bgrtm,bgtmd->btgrd', p_s.astype(vsel.dtype),
                     vsel.reshape(B, G, Tq, n * SEL_BLOCK, HEAD_DIM))

    dist_w = tq[:, None] - pw[None, :]
    lw = jnp.einsum('btgrd,bsgd->bgrts', qg, kw).astype(f32) * ATT_SCALE + bias_2d(dist_w)
    p_w = masked_softmax(lw, (dist_w >= 0) & (dist_w < WINDOW) & (pw >= 0)[None, :])
    o_w = jnp.einsum('bgrts,bsgd->btgrd', p_w.astype(vw.dtype), vw)

    flat = lambda o: o.reshape(B, Tq, NSA_WIDTH).astype(q.dtype)
    return flat(o_c), flat(o_s), flat(o_w)


def nsa_sample(p_nsa, past, win_buf, cmp_w, cmp_b, rel_bias):
    q, kc_new, vc_new, ks_new, vs_new, kw, vw, gl = nsa_split(p_nsa)
    S = q.shape[1]
    P = past.shape[1]
    wb = win_buf.shape[1]
    cat = lambda i, new: jnp.concatenate([past[:, :, i], new], axis=1)
    ctx = nsa_context(cat(0, kc_new), cat(1, vc_new), cat(2, ks_new), cat(3, vs_new), cmp_w, cmp_b)
    win_all = jnp.concatenate([win_buf, jnp.stack([kw, vw], axis=2)], axis=1)
    o = nsa_attend(q, gl, P + jnp.arange(S), *ctx, win_all[:, :, 0], win_all[:, :, 1],
                   P - wb + jnp.arange(wb + S), rel_bias)
    rows = jnp.stack([kc_new, vc_new, ks_new, vs_new], axis=2)
    return o, rows, win_all[:, S:]


def fox_sample_attn(q, k, v, logf, k_past, v_past, logf_past):
    B, S = q.shape[:2]
    P = k_past.shape[1]
    c_past = jnp.cumsum(logf_past.astype(jnp.float32), axis=1)
    c_new = c_past[:, -1:] + jnp.cumsum(logf, axis=1)
    cp, cn = c_past.transpose(0, 2, 1), c_new.transpose(0, 2, 1)
    l_past = (jnp.einsum('bthd,bshd->bhts', q, k_past).astype(jnp.float32) * ATT_SCALE
              + (cn[..., None] - cp[:, :, None, :]))
    l_new = (jnp.einsum('bthd,bshd->bhts', q, k).astype(jnp.float32) * ATT_SCALE
             + (cn[..., None] - cn[:, :, None, :]))
    ar = jnp.arange(S)
    mask = jnp.concatenate([jnp.ones((S, P), bool), ar[None, :] <= ar[:, None]], axis=1)
    p = masked_softmax(jnp.concatenate([l_past, l_new], axis=-1), mask)
    o = (jnp.einsum('bhts,bshd->bthd', p[..., :P].astype(v.dtype), v_past)
         + jnp.einsum('bhts,bshd->bthd', p[..., P:].astype(v.dtype), v))
    return o.reshape(B, S, FOX_WIDTH)


def kernel(x_prompt, x_sample, cache_nsa_kv, cache_fox_k, cache_fox_v, cache_fox_logf, state_nsa_win, state_rwkv_wkv, state_rwkv_shift, page_table, ln_g, ln_b, ffn_w_gate, ffn_w_up, ffn_w_down, w_in, w_out, rw_mu, rw_w0, rw_w2, rw_a0, rw_a2, rw_g2, rw_k_k, rw_k_a, rw_r_k, rw_gn_g, rw_gn_b, nsa_cmp_w, nsa_cmp_b, nsa_out_g, rel_bias, fox_b_f, fox_out_g):
    db = x_sample.shape[0]
    t = x_prompt.shape[1]
    past_len = page_table.shape[1] * cache_nsa_kv.shape[2]

    wg_bf = ffn_w_gate.astype(BF16)
    wu_bf = ffn_w_up.astype(BF16)
    wd_bf = ffn_w_down.astype(BF16)
    wout_bf = w_out.astype(BF16)
    w_rw = w_in[:, :, :RW_COLS].astype(BF16)
    w_nsa = jnp.pad(w_in[:, :, RW_COLS:RW_COLS + NSA_COLS].astype(BF16), ((0, 0), (0, 0), (0, 1408 - NSA_COLS)))
    w_fox = jnp.pad(w_in[:, :, RW_COLS + NSA_COLS:].astype(BF16), ((0, 0), (0, 0), (0, 3200 - FOX_COLS)))

    def ffn_sub(x, x_bf, l, j):
        h = ffn_in(x_bf, wg_bf[l, j], wu_bf[l, j])
        return ffn_out(h, wd_bf[l, j], x, ln_g[l, 2 * j], ln_b[l, 2 * j])

    def rw_params(l):
        return (rw_mu[l], rw_w0[l], rw_w2[l], rw_a0[l], rw_a2[l], rw_g2[l], rw_k_k[l], rw_k_a[l],
                rw_r_k[l], rw_gn_g[l], rw_gn_b[l])

    xp = x_prompt[0]
    xp_bf = xp.astype(BF16)
    p_kv, p_fk, p_fv, p_fl, p_win, p_wkv, p_shift = [], [], [], [], [], [], []
    btab = bias_tiles(rel_bias, NSA_TILE)
    wb = min(WINDOW, t)
    for l in range(DEPTH):
        xp, xp_bf = ffn_sub(xp, xp_bf, l, 0)
        p_rw = proj(xp_bf, w_rw[l])
        nq, kv4, kwv, gl, dup = nsa_proj(xp_bf, nsa_proj_weight(w_in[l, :, RW_COLS:RW_COLS + NSA_COLS]))
        qbf, fk, fv, kbf, vbf, lf128 = fox_proj(xp_bf, w_fox[l], fox_b_f[l])
        o_rw, s_wkv = rwkv_prompt(p_rw, t, rw_params(l))
        wmat, cbias = nsa_compress_weight(nsa_cmp_w[l], nsa_cmp_b[l])
        o_c, o_s, o_w = nsa_prompt_attn(nq, kv4, dup, wmat, cbias, rel_bias, btab, t)
        o_fox = fox_prompt_attn(qbf, kbf, vbf, lf128, t)
        xp, xp_bf = mix_out(o_rw, o_c, o_s, o_w, gl, o_fox, xp, wout_bf[l], nsa_out_g[l], fox_out_g[l],
                            ln_g[l, 1], ln_b[l, 1])
        xp, xp_bf = ffn_sub(xp, xp_bf, l, 1)
        p_kv.append(kv4.reshape(1, t, 4, NSA_KV_HEADS, HEAD_DIM))
        p_fk.append(fk.reshape(1, t, FOX_HEADS, HEAD_DIM))
        p_fv.append(fv.reshape(1, t, FOX_HEADS, HEAD_DIM))
        p_fl.append(lf128[None, :, :FOX_HEADS])
        p_win.append(kwv[t - wb:].reshape(1, wb, 2, NSA_KV_HEADS, HEAD_DIM))
        p_wkv.append(s_wkv[None])
        p_shift.append(p_rw[t - 1:t])

    xs = x_sample[:, 0]
    xs_bf = xs.astype(BF16)
    s_kv, s_fk, s_fv, s_fl, s_win, s_wkv_l, s_shift_l = [], [], [], [], [], [], []
    for l in range(DEPTH):
        xs, xs_bf = ffn_sub(xs, xs_bf, l, 0)
        p_rw = proj(xs_bf, w_rw[l])[:, None]
        p_nsa = proj(xs_bf, w_nsa[l])[:, None, :NSA_COLS]
        qbf, fk, fv, kbf, vbf, lf128 = fox_proj(xs_bf, w_fox[l], fox_b_f[l])
        o_rw, s_wkv, s_shift = rwkv_mix(p_rw, state_rwkv_shift[l], state_rwkv_wkv[l], *rw_params(l))
        nsa_past = cache_nsa_kv[l, page_table].reshape(db, past_len, 4, NSA_KV_HEADS, HEAD_DIM)
        (o_c, o_s, o_w), kv_rows, win = nsa_sample(p_nsa, nsa_past, state_nsa_win[l], nsa_cmp_w[l], nsa_cmp_b[l],
                                                   rel_bias)
        gl = _pad_cols(p_nsa[:, 0, NSA_SPLITS[-1]:], LANES)
        k_past = cache_fox_k[l, page_table].reshape(db, past_len, FOX_HEADS, HEAD_DIM)
        v_past = cache_fox_v[l, page_table].reshape(db, past_len, FOX_HEADS, HEAD_DIM)
        lf_past = cache_fox_logf[l, page_table].reshape(db, past_len, FOX_HEADS)
        heads = lambda a: a.reshape(db, 1, FOX_HEADS, HEAD_DIM)
        q_f32 = qbf.astype(F32) * (1.0 / ATT_SCALE)
        o_fox = fox_sample_attn(heads(q_f32), heads(fk), heads(fv), lf128[:, None, :FOX_HEADS],
                                k_past, v_past, lf_past)
        xs, xs_bf = mix_out(o_rw[:, 0], o_c[:, 0], o_s[:, 0], o_w[:, 0], gl, o_fox[:, 0], xs, wout_bf[l],
                            nsa_out_g[l], fox_out_g[l], ln_g[l, 1], ln_b[l, 1])
        xs, xs_bf = ffn_sub(xs, xs_bf, l, 1)
        s_kv.append(kv_rows)
        s_fk.append(heads(fk))
        s_fv.append(heads(fv))
        s_fl.append(lf128[:, None, :FOX_HEADS])
        s_win.append(win)
        s_wkv_l.append(s_wkv)
        s_shift_l.append(s_shift)

    return (xp[None], xs[:, None],
            jnp.stack(p_kv), jnp.stack(p_fk), jnp.stack(p_fv), jnp.stack(p_fl),
            jnp.stack(p_win), jnp.stack(p_wkv), jnp.stack(p_shift),
            jnp.stack(s_kv), jnp.stack(s_fk), jnp.stack(s_fv), jnp.stack(s_fl),
            jnp.stack(s_win), jnp.stack(s_wkv_l), jnp.stack(s_shift_l))
```

```python
import functools
import math

import jax
import jax.numpy as jnp
from jax import lax
from jax.experimental import pallas as pl
from jax.experimental.pallas import tpu as pltpu

F32 = jnp.float32
BF16 = jnp.bfloat16

D_MODEL = 2048
DEPTH = 2
HEAD_DIM = 64
RW_HEADS = 8
RW_WIDTH = 512
DECAY_LORA = 64
AAA_LORA = 64
GATE_LORA = 128
RW_COLS = 3 * RW_WIDTH + DECAY_LORA + AAA_LORA + GATE_LORA
RW_SPLITS = [RW_WIDTH, 2 * RW_WIDTH, 3 * RW_WIDTH, 3 * RW_WIDTH + DECAY_LORA, 3 * RW_WIDTH + DECAY_LORA + AAA_LORA]
RW_GN_EPS = 64e-5
NSA_HEADS = 8
NSA_KV_HEADS = 2
NSA_REP = NSA_HEADS // NSA_KV_HEADS
NSA_WIDTH = 512
NSA_KV_WIDTH = 128
NSA_COLS = NSA_WIDTH + 6 * NSA_KV_WIDTH + 3 * NSA_HEADS
NSA_SPLITS = [NSA_WIDTH + i * NSA_KV_WIDTH for i in range(7)]
CMP_BLOCK = 32
CMP_STRIDE = 16
SEL_BLOCK = 64
N_SELECT = 16
WINDOW = 512
SEL_FORCE = 1e4
FOX_HEADS = 16
FOX_WIDTH = 1024
FOX_COLS = 3 * FOX_WIDTH + FOX_HEADS
FOX_SPLITS = [FOX_WIDTH, 2 * FOX_WIDTH, 3 * FOX_WIDTH]
IN_SPLITS = [RW_COLS, RW_COLS + NSA_COLS]
N_BUCKETS = 32
MAX_DISTANCE = 1024
Q_BLOCK = 128
DEEPNORM_ALPHA = (2 * DEPTH) ** 0.25
LN_EPS = 1e-5
RMS_EPS = 1e-6
ATT_SCALE = HEAD_DIM ** -0.5

LANES = 128
VMEM_LIMIT = 56 * 1024 * 1024
NEG = -1e30


def _cparams(sem):
    return pltpu.CompilerParams(dimension_semantics=sem, vmem_limit_bytes=VMEM_LIMIT)


def _pad_cols(w, n):
    return jnp.pad(w, ((0, 0), (0, n - w.shape[1])))


def _ffn_in_kernel(x_ref, wg_ref, wu_ref, h_ref):
    x = x_ref[...]
    g = jnp.dot(x, wg_ref[...], preferred_element_type=F32)
    u = jnp.dot(x, wu_ref[...], preferred_element_type=F32)
    h_ref[...] = (g * (1.0 / (1.0 + jnp.exp(-g))) * u).astype(BF16)


def ffn_in(x_bf, wg, wu):
    m, d = x_bf.shape
    f = wg.shape[1]
    tm = min(m, 1024)
    tn = 512
    return pl.pallas_call(
        _ffn_in_kernel,
        grid=(m // tm, f // tn),
        in_specs=[pl.BlockSpec((tm, d), lambda i, j: (i, 0)),
                  pl.BlockSpec((d, tn), lambda i, j: (0, j)),
                  pl.BlockSpec((d, tn), lambda i, j: (0, j))],
        out_specs=pl.BlockSpec((tm, tn), lambda i, j: (i, j)),
        out_shape=jax.ShapeDtypeStruct((m, f), BF16),
        compiler_params=_cparams(("parallel", "arbitrary")),
        name="ffn_in",
    )(x_bf, wg, wu)


def _layer_norm(y, g, b):
    mu = jnp.mean(y, -1, keepdims=True)
    d = y - mu
    var = jnp.mean(d * d, -1, keepdims=True)
    return d * lax.rsqrt(var + LN_EPS) * g + b


def _ffn_out_kernel(h_ref, wd_ref, x_ref, g_ref, b_ref, y_ref, ybf_ref, acc_ref):
    k = pl.program_id(1)

    @pl.when(k == 0)
    def _():
        acc_ref[...] = jnp.zeros_like(acc_ref)

    acc_ref[...] += jnp.dot(h_ref[...], wd_ref[...], preferred_element_type=F32)

    @pl.when(k == pl.num_programs(1) - 1)
    def _():
        y = _layer_norm(DEEPNORM_ALPHA * x_ref[...] + 0.5 * acc_ref[...], g_ref[...], b_ref[...])
        y_ref[...] = y
        ybf_ref[...] = y.astype(BF16)


def ffn_out(h_bf, wd, x, g, b):
    m, f = h_bf.shape
    d = wd.shape[1]
    tm = min(m, 512)
    tk = 512
    return pl.pallas_call(
        _ffn_out_kernel,
        grid=(m // tm, f // tk),
        in_specs=[pl.BlockSpec((tm, tk), lambda i, k: (i, k)),
                  pl.BlockSpec((tk, d), lambda i, k: (k, 0)),
                  pl.BlockSpec((tm, d), lambda i, k: (i, 0)),
                  pl.BlockSpec((1, d), lambda i, k: (0, 0)),
                  pl.BlockSpec((1, d), lambda i, k: (0, 0))],
        out_specs=[pl.BlockSpec((tm, d), lambda i, k: (i, 0)),
                   pl.BlockSpec((tm, d), lambda i, k: (i, 0))],
        out_shape=[jax.ShapeDtypeStruct((m, d), F32), jax.ShapeDtypeStruct((m, d), BF16)],
        scratch_shapes=[pltpu.VMEM((tm, d), F32)],
        compiler_params=_cparams(("parallel", "arbitrary")),
        name="ffn_out",
    )(h_bf, wd, x, g.reshape(1, d), b.reshape(1, d))


def _proj_kernel(x_ref, w_ref, o_ref):
    o_ref[...] = jnp.dot(x_ref[...], w_ref[...], preferred_element_type=F32)


def proj(x_bf, w_bf):
    m, d = x_bf.shape
    n = w_bf.shape[1]
    tm = min(m, 512)
    tn = 128
    for c in (640, 512, 384, 256):
        if n % c == 0:
            tn = c
            break
    return pl.pallas_call(
        _proj_kernel,
        grid=(m // tm, n // tn),
        in_specs=[pl.BlockSpec((tm, d), lambda i, j: (i, 0)),
                  pl.BlockSpec((d, tn), lambda i, j: (0, j))],
        out_specs=pl.BlockSpec((tm, tn), lambda i, j: (i, j)),
        out_shape=jax.ShapeDtypeStruct((m, n), F32),
        compiler_params=_cparams(("parallel", "arbitrary")),
        name="proj",
    )(x_bf, w_bf)


def _log_sigmoid(x):
    return jnp.minimum(x, 0.0) - jnp.log1p(jnp.exp(-jnp.abs(x)))


def _fox_proj_kernel(x_ref, w_ref, bf_ref, qbf_ref, k_ref, v_ref, kbf_ref, vbf_ref, lf_ref):
    x = x_ref[...]
    q = jnp.dot(x, w_ref[:, 0:FOX_WIDTH], preferred_element_type=F32)
    qbf_ref[...] = (q * ATT_SCALE).astype(BF16)
    k = jnp.dot(x, w_ref[:, FOX_WIDTH:2 * FOX_WIDTH], preferred_element_type=F32)
    k_ref[...] = k
    kbf_ref[...] = k.astype(BF16)
    v = jnp.dot(x, w_ref[:, 2 * FOX_WIDTH:3 * FOX_WIDTH], preferred_element_type=F32)
    v_ref[...] = v
    vbf_ref[...] = v.astype(BF16)
    f = jnp.dot(x, w_ref[:, 3 * FOX_WIDTH:3 * FOX_WIDTH + LANES], preferred_element_type=F32)
    lf_ref[...] = _log_sigmoid(f + bf_ref[...])


def fox_proj(x_bf, w_bf, b_f):
    m, d = x_bf.shape
    n = w_bf.shape[1]
    tm = min(m, 256)
    row = lambda width: pl.BlockSpec((tm, width), lambda i: (i, 0))
    bias = jnp.pad(b_f.astype(F32), (0, LANES - FOX_HEADS)).reshape(1, LANES)
    return pl.pallas_call(
        _fox_proj_kernel,
        grid=(m // tm,),
        in_specs=[row(d), pl.BlockSpec((d, n), lambda i: (0, 0)), pl.BlockSpec((1, LANES), lambda i: (0, 0))],
        out_specs=[row(FOX_WIDTH)] * 5 + [row(LANES)],
        out_shape=[jax.ShapeDtypeStruct((m, FOX_WIDTH), BF16),
                   jax.ShapeDtypeStruct((m, FOX_WIDTH), F32), jax.ShapeDtypeStruct((m, FOX_WIDTH), F32),
                   jax.ShapeDtypeStruct((m, FOX_WIDTH), BF16), jax.ShapeDtypeStruct((m, FOX_WIDTH), BF16),
                   jax.ShapeDtypeStruct((m, LANES), F32)],
        compiler_params=_cparams(("parallel",)),
        name="fox_proj",
    )(x_bf, w_bf, bias)


def _split3(x):
    hi = x.astype(BF16)
    r = x - hi.astype(F32)
    mid = r.astype(BF16)
    lo = (r - mid.astype(F32)).astype(BF16)
    return hi, mid, lo


def _seg_sum(x, ones_blk):
    hi, mid, lo = _split3(x)
    d = lambda a: jnp.dot(a, ones_blk, preferred_element_type=F32)
    return d(hi) + d(mid) + d(lo)


def _head_ones():
    r = lax.broadcasted_iota(jnp.int32, (LANES, LANES), 0) // HEAD_DIM
    c = lax.broadcasted_iota(jnp.int32, (LANES, LANES), 1) // HEAD_DIM
    return (r == c).astype(BF16)


def _mix_out_kernel(orw_ref, oc_ref, os_ref, ow_ref, gl_ref, ofox_ref, x_ref, w_ref, gn_ref, gf_ref, g_ref, b_ref,
                    y_ref, ybf_ref):
    ones_blk = _head_ones()
    gate = 1.0 / (1.0 + jnp.exp(-gl_ref[...]))
    pieces = _split3(gate)
    er = lax.broadcasted_iota(jnp.int32, (LANES, NSA_WIDTH), 0)
    ec = lax.broadcasted_iota(jnp.int32, (LANES, NSA_WIDTH), 1) // HEAD_DIM
    onsa = jnp.zeros(oc_ref.shape, F32)
    for c, br_ref in enumerate((oc_ref, os_ref, ow_ref)):
        spread = (er == ec * 3 + c).astype(BF16)
        gate_c = sum(jnp.dot(piece, spread, preferred_element_type=F32) for piece in pieces)
        onsa = onsa + gate_c * br_ref[...]

    def rms(o, gain):
        cols = []
        for c in range(o.shape[1] // LANES):
            blk = o[:, c * LANES:(c + 1) * LANES]
            ms = _seg_sum(blk * blk, ones_blk) * (1.0 / HEAD_DIM)
            cols.append(blk * lax.rsqrt(ms + RMS_EPS) * gain[:, c * LANES:(c + 1) * LANES])
        return jnp.concatenate(cols, axis=1)

    o = jnp.concatenate([orw_ref[...], rms(onsa, gn_ref[...]), rms(ofox_ref[...], gf_ref[...])], axis=1)
    acc = jnp.dot(o.astype(BF16), w_ref[...], preferred_element_type=F32)
    y = _layer_norm(DEEPNORM_ALPHA * x_ref[...] + acc, g_ref[...], b_ref[...])
    y_ref[...] = y
    ybf_ref[...] = y.astype(BF16)


def mix_out(o_rw, o_c, o_s, o_w, gl, o_fox, x, w_bf, gn, gf, g, b):
    m, d = x.shape
    tm = min(m, 256)
    row = lambda width: pl.BlockSpec((tm, width), lambda i: (i, 0))
    full = lambda a, c: pl.BlockSpec((a, c), lambda i: (0, 0))
    return pl.pallas_call(
        _mix_out_kernel,
        grid=(m // tm,),
        in_specs=[row(RW_WIDTH), row(NSA_WIDTH), row(NSA_WIDTH), row(NSA_WIDTH), row(LANES), row(FOX_WIDTH),
                  row(d), full(d, d), full(1, NSA_WIDTH), full(1, FOX_WIDTH), full(1, d), full(1, d)],
        out_specs=[row(d), row(d)],
        out_shape=[jax.ShapeDtypeStruct((m, d), F32), jax.ShapeDtypeStruct((m, d), BF16)],
        compiler_params=_cparams(("parallel",)),
        name="mix_out",
    )(o_rw, o_c, o_s, o_w, gl, o_fox, x, w_bf, gn.reshape(1, -1), gf.reshape(1, -1), g.reshape(1, d),
      b.reshape(1, d))


CUM_BLOCK = 512


def _cumsum_kernel(x_ref, o_ref, carry_ref):
    @pl.when(pl.program_id(0) == 0)
    def _():
        carry_ref[...] = jnp.zeros_like(carry_ref)

    r = lax.broadcasted_iota(jnp.int32, (CUM_BLOCK, CUM_BLOCK), 0)
    c = lax.broadcasted_iota(jnp.int32, (CUM_BLOCK, CUM_BLOCK), 1)
    upper = (r <= c).astype(BF16)
    hi, mid, lo = _split3(x_ref[...])
    d = lambda a: jnp.dot(a, upper, preferred_element_type=F32)
    cs = d(hi) + d(mid) + d(lo) + carry_ref[:, 0:1]
    o_ref[...] = cs
    carry_ref[...] = jnp.broadcast_to(cs[:, CUM_BLOCK - 1:CUM_BLOCK], carry_ref.shape)


def cumsum_lanes(x):
    rows, t = x.shape
    return pl.pallas_call(
        _cumsum_kernel,
        grid=(t // CUM_BLOCK,),
        in_specs=[pl.BlockSpec((rows, CUM_BLOCK), lambda i: (0, i))],
        out_specs=pl.BlockSpec((rows, CUM_BLOCK), lambda i: (0, i)),
        out_shape=jax.ShapeDtypeStruct((rows, t), F32),
        scratch_shapes=[pltpu.VMEM((rows, LANES), F32)],
        compiler_params=_cparams(("arbitrary",)),
        name="cumsum",
    )(x)


def _fox_flash_kernel(qt_ref, kt_ref, q_ref, k_ref, v_ref, ck_ref, o_ref, m_ref, l_ref, acc_ref, *, tq, tk):
    step = pl.program_id(1)
    qi = qt_ref[step]
    ki = kt_ref[step]

    @pl.when(ki == 0)
    def _():
        m_ref[...] = jnp.full_like(m_ref, NEG)
        l_ref[...] = jnp.zeros_like(l_ref)
        acc_ref[...] = jnp.zeros_like(acc_ref)

    left = lax.broadcasted_iota(jnp.int32, (1, LANES), 1) < HEAD_DIM

    def tile(diagonal):
        q = q_ref[...]
        k = k_ref[...]
        v = v_ref[...]
        pv, alphas = [], []
        for hh in range(2):
            qm = jnp.where(left if hh == 0 else jnp.logical_not(left), q, jnp.zeros_like(q))
            s = lax.dot_general(qm, k, (((1,), (1,)), ((), ())), preferred_element_type=F32)
            s = s - ck_ref[0, hh:hh + 1, :]
            if diagonal:
                rows = lax.broadcasted_iota(jnp.int32, (tq, tk), 0)
                cols = lax.broadcasted_iota(jnp.int32, (tq, tk), 1)
                s = jnp.where(rows >= cols, s, NEG)
            m_old = m_ref[hh]
            m_new = jnp.maximum(m_old, jnp.max(s, axis=-1, keepdims=True))
            p = jnp.exp(s - m_new)
            alpha = jnp.exp(m_old - m_new)
            l_ref[hh] = alpha * l_ref[hh] + jnp.sum(p, axis=-1, keepdims=True)
            m_ref[hh] = m_new
            pv.append(jnp.dot(p.astype(BF16), v, preferred_element_type=F32))
            alphas.append(alpha)
        acc_ref[...] = (jnp.where(left, alphas[0], alphas[1]) * acc_ref[...]
                        + jnp.where(left, pv[0], pv[1]))

    @pl.when(ki < qi)
    def _():
        tile(False)

    @pl.when(ki == qi)
    def _():
        tile(True)
        o_ref[...] = acc_ref[...] / jnp.where(left, l_ref[0], l_ref[1])


def _tri_steps(n):
    qs, ks = [], []
    for qi in range(n):
        for ki in range(qi + 1):
            qs.append(qi)
            ks.append(ki)
    return jnp.asarray(qs, jnp.int32), jnp.asarray(ks, jnp.int32)


def fox_flash(q_bf, k_bf, v_bf, ck, t):
    tq = tk = min(t, 512)
    n = t // tq
    qt, kt = _tri_steps(n)
    pairs = FOX_WIDTH // LANES
    grid_spec = pltpu.PrefetchScalarGridSpec(
        num_scalar_prefetch=2,
        grid=(pairs, qt.shape[0]),
        in_specs=[pl.BlockSpec((tq, LANES), lambda p, s, qt, kt: (qt[s], p)),
                  pl.BlockSpec((tk, LANES), lambda p, s, qt, kt: (kt[s], p)),
                  pl.BlockSpec((tk, LANES), lambda p, s, qt, kt: (kt[s], p)),
                  pl.BlockSpec((1, 8, tk), lambda p, s, qt, kt: (p, 0, kt[s]))],
        out_specs=pl.BlockSpec((tq, LANES), lambda p, s, qt, kt: (qt[s], p)),
        scratch_shapes=[pltpu.VMEM((2, tq, 1), F32), pltpu.VMEM((2, tq, 1), F32), pltpu.VMEM((tq, LANES), F32)],
    )
    return pl.pallas_call(
        functools.partial(_fox_flash_kernel, tq=tq, tk=tk),
        grid_spec=grid_spec,
        out_shape=jax.ShapeDtypeStruct((t, FOX_WIDTH), F32),
        compiler_params=_cparams(("parallel", "arbitrary")),
        name="fox_flash",
    )(qt, kt, q_bf, k_bf, v_bf, ck)


def fox_prompt_attn(qbf, kbf, vbf, logf128, t):
    lf = logf128[:t, :FOX_HEADS]
    lf_t = jnp.pad(lf.T.reshape(FOX_HEADS // 2, 2, t), ((0, 0), (0, 6), (0, 0))).reshape(-1, t)
    ck = cumsum_lanes(lf_t).reshape(FOX_HEADS // 2, 8, t)
    return fox_flash(qbf, kbf, vbf, ck, t)


T5_THRESH = (21, 27, 35, 46, 59, 77, 99, 128, 166, 216, 280, 363, 470, 609, 790)
NSA_TILE = 512
NSA_DELTAS = 4
NOT_ALLOWED = -1e30
REMOVED = -2e30


def _t5_bucket_int(dist):
    d = jnp.maximum(dist, 0)
    large = jnp.full(d.shape, N_BUCKETS // 2, jnp.int32)
    for th in T5_THRESH:
        large = large + (d >= th).astype(jnp.int32)
    return jnp.where(d < N_BUCKETS // 2, d, large)


def _bias_lookup(bucket, tab_ref, h):
    val = jnp.full(bucket.shape, tab_ref[0, h], F32)
    for b in range(1, N_BUCKETS):
        val = jnp.where(bucket == b, tab_ref[b, h], val)
    return val


def _bias_tab_kernel(tab_ref, o_ref, *, tb):
    h = pl.program_id(0)
    dl = pl.program_id(1)
    r = lax.broadcasted_iota(jnp.int32, (tb, tb), 0)
    c = lax.broadcasted_iota(jnp.int32, (tb, tb), 1)
    o_ref[0, 0] = _bias_lookup(_t5_bucket_int(dl * tb + r - c), tab_ref, h)


def bias_tiles(rel_bias, tb):
    return pl.pallas_call(
        functools.partial(_bias_tab_kernel, tb=tb),
        grid=(NSA_HEADS, NSA_DELTAS),
        in_specs=[pl.BlockSpec(memory_space=pltpu.SMEM)],
        out_specs=pl.BlockSpec((1, 1, tb, tb), lambda h, d: (h, d, 0, 0)),
        out_shape=jax.ShapeDtypeStruct((NSA_HEADS, NSA_DELTAS, tb, tb), F32),
        compiler_params=_cparams(("parallel", "parallel")),
        name="nsa_bias_tiles",
    )(rel_bias)


NSA_PROJ_COLS = 2432


def nsa_proj_weight(w):
    q, kc, vc, ks, vs, kw, vw, gl = jnp.split(w, NSA_SPLITS, axis=-1)
    dup = []
    for src in (ks, vs, kw, vw):
        for g in range(NSA_KV_HEADS):
            blk = src[:, g * HEAD_DIM:(g + 1) * HEAD_DIM]
            dup += [blk, blk]
    return jnp.concatenate([q, kc, vc, ks, vs, kw, vw, _pad_cols(gl, LANES)] + dup, axis=1).astype(BF16)


def _nsa_proj_kernel(x_ref, w_ref, qbf_ref, kv_ref, kwv_ref, gl_ref, dup_ref):
    x = x_ref[...]
    d = lambda a, b: jnp.dot(x, w_ref[:, a:b], preferred_element_type=F32)
    qbf_ref[...] = (d(0, 512) * ATT_SCALE).astype(BF16)
    kv_ref[...] = d(512, 1024)
    kwv_ref[...] = d(1024, 1280)
    gl_ref[...] = d(1280, 1408)
    dup_ref[...] = d(1408, 2432).astype(BF16)


def nsa_proj(x_bf, w_bf):
    m, dm = x_bf.shape
    tm = min(m, 256)
    row = lambda width: pl.BlockSpec((tm, width), lambda i: (i, 0))
    return pl.pallas_call(
        _nsa_proj_kernel,
        grid=(m // tm,),
        in_specs=[row(dm), pl.BlockSpec((dm, NSA_PROJ_COLS), lambda i: (0, 0))],
        out_specs=[row(512), row(512), row(256), row(LANES), row(1024)],
        out_shape=[jax.ShapeDtypeStruct((m, 512), BF16), jax.ShapeDtypeStruct((m, 512), F32),
                   jax.ShapeDtypeStruct((m, 256), F32), jax.ShapeDtypeStruct((m, LANES), F32),
                   jax.ShapeDtypeStruct((m, 1024), BF16)],
        compiler_params=_cparams(("parallel",)),
        name="nsa_proj",
    )(x_bf, w_bf)


def nsa_compress_weight(cmp_w, cmp_b):
    w = cmp_w.reshape(2, 2, CMP_STRIDE, HEAD_DIM, HEAD_DIM)
    eye = jnp.eye(2, dtype=F32)
    t = jnp.einsum('whjde,wv,gk->jwgdhvke', w, eye, eye)
    t = jnp.broadcast_to(t[..., None, :], t.shape[:-1] + (2, HEAD_DIM))
    wmat = t.reshape(CMP_STRIDE * 256, 1024).astype(BF16)
    bias = jnp.broadcast_to(cmp_b[:, None, None, :], (2, NSA_KV_HEADS, 2, HEAD_DIM)).reshape(1, 512)
    return wmat, bias.astype(F32)


def _nsa_compress_kernel(x_ref, w_ref, b_ref, o_ref, acc_ref):
    j = pl.program_id(0)

    @pl.when(j == 0)
    def _():
        acc_ref[...] = jnp.zeros_like(acc_ref)

    acc_ref[...] += jnp.dot(x_ref[...].astype(BF16), w_ref[...], preferred_element_type=F32)

    @pl.when(j == pl.num_programs(0) - 1)
    def _():
        n = acc_ref.shape[0]
        first = acc_ref[:, 0:512]
        second = pltpu.roll(acc_ref[:, 512:1024], n - 1, axis=0)
        o_ref[...] = (first + second + b_ref[...]).astype(BF16)


def nsa_compress_rows(kv4, wmat, bias):
    t = kv4.shape[0]
    n = t // CMP_STRIDE
    x = kv4.reshape(n, CMP_STRIDE * 512)
    return pl.pallas_call(
        _nsa_compress_kernel,
        grid=(CMP_STRIDE,),
        in_specs=[pl.BlockSpec((n, 256), lambda j: (0, 2 * j)),
                  pl.BlockSpec((256, 1024), lambda j: (j, 0)),
                  pl.BlockSpec((1, 512), lambda j: (0, 0))],
        out_specs=pl.BlockSpec((n, 512), lambda j: (0, 0)),
        out_shape=jax.ShapeDtypeStruct((n, 512), BF16),
        scratch_shapes=[pltpu.VMEM((n, 1024), F32)],
        compiler_params=_cparams(("arbitrary",)),
        name="nsa_compress",
    )(x, wmat, bias)


def _nsa_cmp_kernel(tab_ref, q_ref, kv_ref, oc_ref, sel_ref, *, tq, ncp, n_sel):
    t0 = pl.program_id(0) * tq
    rows = lax.broadcasted_iota(jnp.int32, (tq, ncp), 0) + t0
    cols = lax.broadcasted_iota(jnp.int32, (tq, ncp), 1)
    dist = rows - (cols * CMP_STRIDE + (CMP_BLOCK - 1))
    valid = dist >= 0
    bucket = _t5_bucket_int(dist)
    left = lax.broadcasted_iota(jnp.int32, (1, LANES), 1) < HEAD_DIM
    ci = lax.broadcasted_iota(jnp.int32, (ncp, LANES), 0) * CMP_STRIDE
    cj = lax.broadcasted_iota(jnp.int32, (ncp, LANES), 1)
    cover = jnp.where((ci < (cj + 1) * SEL_BLOCK) & (ci + CMP_BLOCK > cj * SEL_BLOCK), 1.0, 0.0).astype(BF16)
    imp = [jnp.zeros((tq, LANES), F32) for _ in range(NSA_KV_HEADS)]
    for p in range(NSA_HEADS // 2):
        g = p // (NSA_REP // 2)
        qp = q_ref[:, p * LANES:(p + 1) * LANES]
        kc = kv_ref[:, g * LANES:(g + 1) * LANES]
        vc = kv_ref[:, (NSA_KV_HEADS + g) * LANES:(NSA_KV_HEADS + g + 1) * LANES]
        outs = []
        for hh in range(2):
            qm = jnp.where(left if hh == 0 else jnp.logical_not(left), qp, jnp.zeros_like(qp))
            s = lax.dot_general(qm, kc, (((1,), (1,)), ((), ())), preferred_element_type=F32)
            s = jnp.where(valid, s + _bias_lookup(bucket, tab_ref, 2 * p + hh), NEG)
            m = jnp.max(s, axis=-1, keepdims=True)
            e = jnp.where(valid, jnp.exp(s - m), 0.0)
            pr = (e / jnp.maximum(jnp.sum(e, axis=-1, keepdims=True), 1e-30)).astype(BF16)
            outs.append(jnp.dot(pr, vc, preferred_element_type=F32))
            imp[g] = imp[g] + jnp.dot(pr, cover, preferred_element_type=F32)
        oc_ref[:, p * LANES:(p + 1) * LANES] = jnp.where(left, outs[0], outs[1])

    lane = lax.broadcasted_iota(jnp.int32, (tq, LANES), 1)
    lane_f = lane.astype(F32)
    jq = (lax.broadcasted_iota(jnp.int32, (tq, LANES), 0) + t0) // SEL_BLOCK
    forced = (lane == 0) | (lane == jq) | (lane == jq - 1)
    allowed = lane <= jq
    for g in range(NSA_KV_HEADS):
        work = jnp.where(allowed, imp[g] + jnp.where(forced, SEL_FORCE, 0.0), NOT_ALLOWED)
        sel = jnp.zeros((tq, LANES), F32)
        for _ in range(n_sel):
            mx = jnp.max(work, axis=-1, keepdims=True)
            idx = jnp.min(jnp.where(work == mx, lane_f, float(LANES)), axis=-1, keepdims=True)
            hit = lane_f == idx
            sel = jnp.where(hit & (mx > 0.5 * NOT_ALLOWED), 1.0, sel)
            work = jnp.where(hit, REMOVED, work)
        sel_ref[:, g * LANES:(g + 1) * LANES] = sel.astype(BF16)


def nsa_cmp_select(qbf, kvc, rel_bias, t):
    tq = min(t, 256)
    ncp = kvc.shape[0]
    n_sel = min(N_SELECT, t // SEL_BLOCK)
    return pl.pallas_call(
        functools.partial(_nsa_cmp_kernel, tq=tq, ncp=ncp, n_sel=n_sel),
        grid=(t // tq,),
        in_specs=[pl.BlockSpec(memory_space=pltpu.SMEM),
                  pl.BlockSpec((tq, 512), lambda i: (i, 0)),
                  pl.BlockSpec((ncp, 512), lambda i: (0, 0))],
        out_specs=[pl.BlockSpec((tq, 512), lambda i: (i, 0)), pl.BlockSpec((tq, 256), lambda i: (i, 0))],
        out_shape=[jax.ShapeDtypeStruct((t, 512), F32), jax.ShapeDtypeStruct((t, 256), BF16)],
        compiler_params=_cparams(("parallel",)),
        name="nsa_cmp_select",
    )(rel_bias, qbf, kvc)


def _nsa_flash_kernel(qt_ref, kt_ref, ft_ref, q_ref, k_ref, v_ref, b_ref, *rest, tq, tk, selected):
    if selected:
        sm_ref, ex_ref, o_ref, m_ref, l_ref, acc_ref = rest
    else:
        o_ref, m_ref, l_ref, acc_ref = rest
    step = pl.program_id(1)
    qi = qt_ref[step]
    ki = kt_ref[step]

    @pl.when(ft_ref[step] == 1)
    def _():
        m_ref[...] = jnp.full_like(m_ref, NEG)
        l_ref[...] = jnp.zeros_like(l_ref)
        acc_ref[...] = jnp.zeros_like(acc_ref)

    left = lax.broadcasted_iota(jnp.int32, (1, LANES), 1) < HEAD_DIM

    def tile(diagonal):
        q = q_ref[...]
        k = k_ref[...]
        v = v_ref[...]
        rows = lax.broadcasted_iota(jnp.int32, (tq, tk), 0)
        cols = lax.broadcasted_iota(jnp.int32, (tq, tk), 1)
        if selected:
            valid = jnp.dot(sm_ref[...], ex_ref[...], preferred_element_type=F32) > 0.5
            if diagonal:
                valid = valid & (rows >= cols)
        else:
            valid = (rows >= cols) if diagonal else (rows < cols)
        pv, alphas = [], []
        for hh in range(2):
            qm = jnp.where(left if hh == 0 else jnp.logical_not(left), q, jnp.zeros_like(q))
            s = lax.dot_general(qm, k, (((1,), (1,)), ((), ())), preferred_element_type=F32)
            s = jnp.where(valid, s + b_ref[hh, 0], NEG)
            m_old = m_ref[hh]
            m_new = jnp.maximum(m_old, jnp.max(s, axis=-1, keepdims=True))
            p = jnp.where(valid, jnp.exp(s - m_new), 0.0)
            alpha = jnp.exp(m_old - m_new)
            l_ref[hh] = alpha * l_ref[hh] + jnp.sum(p, axis=-1, keepdims=True)
            m_ref[hh] = m_new
            pv.append(jnp.dot(p.astype(BF16), v, preferred_element_type=F32))
            alphas.append(alpha)
        acc_ref[...] = (jnp.where(left, alphas[0], alphas[1]) * acc_ref[...]
                        + jnp.where(left, pv[0], pv[1]))

    @pl.when(ki < qi)
    def _():
        tile(False)

    @pl.when(ki == qi)
    def _():
        tile(True)
        o_ref[...] = acc_ref[...] / jnp.where(left, l_ref[0], l_ref[1])


def nsa_flash(qbf, dup, btab, t, selmask=None):
    selected = selmask is not None
    tq = tk = btab.shape[-1]
    n = t // tq
    qs, ks, fs = [], [], []
    for qi in range(n):
        lo = 0 if selected else max(qi - (WINDOW // tk), 0)
        for ki in range(lo, qi + 1):
            qs.append(qi)
            ks.append(ki)
            fs.append(1 if ki == lo else 0)
    qt, kt, ft = (jnp.asarray(a, jnp.int32) for a in (qs, ks, fs))
    kcol = 0 if selected else 4
    half = NSA_REP // 2
    in_specs = [pl.BlockSpec((tq, LANES), lambda p, s, qt, kt, ft: (qt[s], p)),
                pl.BlockSpec((tk, LANES), lambda p, s, qt, kt, ft: (kt[s], kcol + p // half)),
                pl.BlockSpec((tk, LANES), lambda p, s, qt, kt, ft: (kt[s], kcol + 2 + p // half)),
                pl.BlockSpec((2, 1, tq, tk),
                             lambda p, s, qt, kt, ft: (p, jnp.minimum(qt[s] - kt[s], NSA_DELTAS - 1), 0, 0))]
    args = [qbf, dup, dup, btab]
    if selected:
        n_blk = tk // SEL_BLOCK
        jj = jnp.arange(LANES)[:, None]
        ll = jnp.arange(t)[None, :]
        expand = (jj == ll // SEL_BLOCK).astype(BF16)
        in_specs += [pl.BlockSpec((tq, LANES), lambda p, s, qt, kt, ft: (qt[s], p // half)),
                     pl.BlockSpec((LANES, tk), lambda p, s, qt, kt, ft: (0, kt[s]))]
        args += [selmask, expand]
    grid_spec = pltpu.PrefetchScalarGridSpec(
        num_scalar_prefetch=3,
        grid=(NSA_HEADS // 2, len(qs)),
        in_specs=in_specs,
        out_specs=pl.BlockSpec((tq, LANES), lambda p, s, qt, kt, ft: (qt[s], p)),
        scratch_shapes=[pltpu.VMEM((2, tq, 1), F32), pltpu.VMEM((2, tq, 1), F32), pltpu.VMEM((tq, LANES), F32)],
    )
    return pl.pallas_call(
        functools.partial(_nsa_flash_kernel, tq=tq, tk=tk, selected=selected),
        grid_spec=grid_spec,
        out_shape=jax.ShapeDtypeStruct((t, NSA_WIDTH), F32),
        compiler_params=_cparams(("parallel", "arbitrary")),
        name="nsa_sel_flash" if selected else "nsa_win_flash",
    )(qt, kt, ft, *args)


def nsa_prompt_attn(qbf, kv4, dup, wmat, cbias, rel_bias, btab, t):
    kvc = nsa_compress_rows(kv4[:t], wmat, cbias)
    o_c, selmask = nsa_cmp_select(qbf, kvc, rel_bias, t)
    o_s = nsa_flash(qbf, dup, btab, t, selmask)
    o_w = nsa_flash(qbf, dup, btab, t)
    return o_c, o_s, o_w


RW_CHUNK = 128
RW_PAIRS = RW_WIDTH // LANES


def _softplus(x):
    return jnp.maximum(x, 0.0) + jnp.log1p(jnp.exp(-jnp.abs(x)))


def _sigmoid(x):
    return 1.0 / (1.0 + jnp.exp(-x))


def _seg_sum_wide(x, ones_blk):
    return jnp.concatenate([_seg_sum(x[:, c * LANES:(c + 1) * LANES], ones_blk)
                            for c in range(x.shape[1] // LANES)], axis=1)


def _rwkv_features(p, shifted, mu, w0, w2p, a0, a2p, g2, k_k, k_a, ones_blk):
    z = p + (shifted - p) * mu
    r = z[:, 0:RW_WIDTH]
    k = z[:, RW_WIDTH:2 * RW_WIDTH]
    v = z[:, 2 * RW_WIDTH:3 * RW_WIDTH]
    lora = z[:, 3 * RW_WIDTH:3 * RW_WIDTH + LANES]
    gd = z[:, 3 * RW_WIDTH + LANES:]
    w_log = -_softplus(-(w0 + jnp.dot(jnp.tanh(lora).astype(BF16), w2p, preferred_element_type=F32))) - 0.5
    decay = jnp.exp(-jnp.exp(w_log))
    a = _sigmoid(a0 + jnp.dot(lora.astype(BF16), a2p, preferred_element_type=F32))
    g = jnp.dot(_sigmoid(gd).astype(BF16), g2, preferred_element_type=F32)
    kk = k * k_k
    kk = kk / jnp.maximum(jnp.sqrt(_seg_sum_wide(kk * kk, ones_blk)), 1e-12)
    kh = k * (1.0 + (a - 1.0) * k_a)
    return r, decay, kh, v, kk, kk * a, g


def _rwkv_finish(out, r, kh, v, g, r_k, gn_g, gn_b, ones_blk):
    m = _seg_sum_wide(out, ones_blk) * (1.0 / HEAD_DIM)
    d = out - m
    var = _seg_sum_wide(d * d, ones_blk) * (1.0 / HEAD_DIM)
    y = d * lax.rsqrt(var + RW_GN_EPS) * gn_g + gn_b
    bonus = _seg_sum_wide(r * kh * r_k, ones_blk) * v
    return (y + bonus) * g


def _rwkv_kernel(p_ref, mu_ref, w0_ref, w2_ref, a0_ref, a2_ref, g2_ref, kk_ref, ka_ref, rk_ref, gng_ref, gnb_ref,
                 spread_ref, y_ref, sout_ref, s_ref, prev_ref, rows_ref, vt_ref, ot_ref, vb_ref):
    n = RW_CHUNK
    c = pl.program_id(0)

    @pl.when(c == 0)
    def _():
        s_ref[...] = jnp.zeros_like(s_ref)
        prev_ref[...] = jnp.zeros_like(prev_ref)

    p = p_ref[...]
    first = lax.broadcasted_iota(jnp.int32, (n, 1), 0) == 0
    shifted = jnp.where(first, prev_ref[0:1, :], pltpu.roll(p, 1, axis=0))
    prev_ref[...] = jnp.broadcast_to(p[n - 1:n, :], prev_ref.shape)
    ones_blk = _head_ones()
    r, decay, kh, v, kk, kka, g = _rwkv_features(p, shifted, mu_ref[...], w0_ref[...], w2_ref[...], a0_ref[...],
                                                 a2_ref[...], g2_ref[...], kk_ref[...], ka_ref[...], ones_blk)
    for qi, arr in enumerate((r, decay, kh, kk, kka)):
        rows_ref[qi] = arr
    for pr in range(RW_PAIRS):
        vt_ref[pr] = v[:, pr * LANES:(pr + 1) * LANES].T
    ot_ref[...] = jnp.zeros_like(ot_ref)
    lane_t = lax.broadcasted_iota(jnp.int32, (HEAD_DIM, n), 1)
    left = lax.broadcasted_iota(jnp.int32, (1, LANES), 1) < HEAD_DIM
    half_sum = (lax.broadcasted_iota(jnp.int32, (LANES, 2 * LANES), 0) // HEAD_DIM
                == lax.broadcasted_iota(jnp.int32, (LANES, 2 * LANES), 1) // LANES).astype(BF16)

    def group(j, carry):
        base = pl.multiple_of(j * 8, 8)
        for pr in range(RW_PAIRS):
            vtr = pltpu.roll(vt_ref[pr], lax.rem(n - base, n), axis=1)
            pieces = _split3(vtr[0:HEAD_DIM]) + _split3(vtr[HEAD_DIM:])
            vb_ref[pr] = jnp.dot(jnp.concatenate(pieces, axis=1), spread_ref[...], preferred_element_type=F32)
        blk = [[rows_ref[q, pl.ds(base, 8), pr * LANES:(pr + 1) * LANES] for q in range(5)]
               for pr in range(RW_PAIRS)]
        for i in range(8):
            hit = lane_t == base + i
            for pr in range(RW_PAIRS):
                rv = lambda q: blk[pr][q][i:i + 1, :]
                s = s_ref[pr]
                m = s * rv(3)
                sa0 = jnp.sum(jnp.where(left, m, 0.0), axis=1, keepdims=True)
                sa1 = jnp.sum(jnp.where(left, 0.0, m), axis=1, keepdims=True)
                s = s * rv(1) - jnp.where(left, sa0, sa1) * rv(4) + vb_ref[pr, :, i * LANES:(i + 1) * LANES] * rv(2)
                s_ref[pr] = s
                oc = jnp.dot((s * rv(0)).astype(BF16), half_sum, preferred_element_type=F32)
                ot_ref[pr, 0:HEAD_DIM, :] = jnp.where(hit, oc[:, 0:LANES], ot_ref[pr, 0:HEAD_DIM, :])
                ot_ref[pr, HEAD_DIM:, :] = jnp.where(hit, oc[:, LANES:], ot_ref[pr, HEAD_DIM:, :])
        return carry

    lax.fori_loop(0, n // 8, group, 0)
    out = jnp.concatenate([ot_ref[pr].T for pr in range(RW_PAIRS)], axis=1)
    y_ref[...] = _rwkv_finish(out, r, kh, v, g, rk_ref[...], gng_ref[...], gnb_ref[...], ones_blk)

    @pl.when(c == pl.num_programs(0) - 1)
    def _():
        for pr in range(RW_PAIRS):
            sout_ref[2 * pr] = s_ref[pr][:, 0:HEAD_DIM]
            sout_ref[2 * pr + 1] = s_ref[pr][:, HEAD_DIM:]


def _rwkv_params(mu, w0, w2, a0, a2, g2, k_k, k_a, r_k, gn_g, gn_b):
    row = lambda a: a.reshape(1, -1).astype(F32)
    zeros = jnp.zeros((DECAY_LORA, RW_WIDTH), F32)
    w2p = jnp.concatenate([w2, zeros], axis=0).astype(BF16)
    a2p = jnp.concatenate([zeros, a2], axis=0).astype(BF16)
    return (row(mu), row(w0), w2p, row(a0), a2p, g2.astype(BF16), row(k_k), row(k_a), row(r_k), row(gn_g), row(gn_b))


def rwkv_prompt(p_rw, t, params):
    n = RW_CHUNK
    rr = jnp.arange(2 * 3 * LANES)[:, None]
    cc = jnp.arange(8 * LANES)[None, :]
    spread = ((rr % LANES == cc // LANES) & (rr // (3 * LANES) == (cc % LANES) // HEAD_DIM)).astype(BF16)
    args = _rwkv_params(*params) + (spread,)
    full = lambda a: pl.BlockSpec(a.shape, lambda i: (0,) * a.ndim)
    return pl.pallas_call(
        _rwkv_kernel,
        grid=(t // n,),
        in_specs=[pl.BlockSpec((n, RW_COLS), lambda i: (i, 0))] + [full(a) for a in args],
        out_specs=[pl.BlockSpec((n, RW_WIDTH), lambda i: (i, 0)),
                   pl.BlockSpec((RW_HEADS, HEAD_DIM, HEAD_DIM), lambda i: (0, 0, 0))],
        out_shape=[jax.ShapeDtypeStruct((t, RW_WIDTH), F32),
                   jax.ShapeDtypeStruct((RW_HEADS, HEAD_DIM, HEAD_DIM), F32)],
        scratch_shapes=[pltpu.VMEM((RW_PAIRS, HEAD_DIM, LANES), F32),
                        pltpu.VMEM((8, RW_COLS), F32),
                        pltpu.VMEM((5, n, RW_WIDTH), F32),
                        pltpu.VMEM((RW_PAIRS, LANES, n), F32),
                        pltpu.VMEM((RW_PAIRS, LANES, n), F32),
                        pltpu.VMEM((RW_PAIRS, HEAD_DIM, 8 * LANES), F32)],
        compiler_params=_cparams(("arbitrary",)),
        name="rwkv_prompt",
    )(p_rw, *args)


def _rwkv_sample_kernel(p_ref, sh_ref, s_in_ref, mu_ref, w0_ref, w2_ref, a0_ref, a2_ref, g2_ref, kk_ref, ka_ref,
                        rk_ref, gng_ref, gnb_ref, y_ref, s_out_ref, rows_ref, vt_ref, ot_ref):
    n = p_ref.shape[0]
    ones_blk = _head_ones()
    r, decay, kh, v, kk, kka, g = _rwkv_features(p_ref[...], sh_ref[...], mu_ref[...], w0_ref[...], w2_ref[...],
                                                 a0_ref[...], a2_ref[...], g2_ref[...], kk_ref[...], ka_ref[...],
                                                 ones_blk)
    for qi, arr in enumerate((r, decay, kh, kk, kka)):
        rows_ref[qi, 0] = arr
        rows_ref[qi, 1] = pltpu.roll(arr, RW_WIDTH - HEAD_DIM, axis=1)
    vt_ref[...] = jnp.zeros_like(vt_ref)
    ot_ref[...] = jnp.zeros_like(ot_ref)
    vpad = jnp.concatenate([v, jnp.zeros((LANES - n, RW_WIDTH), F32)], axis=0) if n < LANES else v
    for pr in range(RW_PAIRS):
        vt_ref[pr] = vpad[:, pr * LANES:(pr + 1) * LANES].T
    lane_t = lax.broadcasted_iota(jnp.int32, (HEAD_DIM, LANES), 1)

    def group(j, carry):
        base = pl.multiple_of(j * 8, 8)
        for h in range(RW_HEADS):
            pr, par = h // 2, h % 2
            lo, ro = pr * LANES, par * HEAD_DIM
            blk = [rows_ref[q, par, pl.ds(base, 8), lo:lo + HEAD_DIM] for q in range(5)]
            ot = ot_ref[pr, ro:ro + HEAD_DIM, :]
            vt = vt_ref[pr, ro:ro + HEAD_DIM, :]
            for i in range(8):
                hit = lane_t == base + i
                rv = lambda q: blk[q][i:i + 1, :]
                s = s_in_ref[base + i, h]
                sa = -jnp.sum(s * rv(3), axis=1, keepdims=True)
                vcol = jnp.sum(jnp.where(hit, vt, 0.0), axis=1, keepdims=True)
                s = s * rv(1) + sa * rv(4) + vcol * rv(2)
                s_out_ref[base + i, h] = s
                ocol = jnp.sum(s * rv(0), axis=1, keepdims=True)
                ot = jnp.where(hit, ocol, ot)
            ot_ref[pr, ro:ro + HEAD_DIM, :] = ot
        return carry

    lax.fori_loop(0, n // 8, group, 0)
    out = jnp.concatenate([ot_ref[pr].T for pr in range(RW_PAIRS)], axis=1)[0:n]
    y_ref[...] = _rwkv_finish(out, r, kh, v, g, rk_ref[...], gng_ref[...], gnb_ref[...], ones_blk)


def rwkv_sample(p_rw, shift, state, params):
    b = p_rw.shape[0]
    n = min(b, 32)
    args = _rwkv_params(*params)
    full = lambda a: pl.BlockSpec(a.shape, lambda i: (0,) * a.ndim)
    st = pl.BlockSpec((n, RW_HEADS, HEAD_DIM, HEAD_DIM), lambda i: (i, 0, 0, 0))
    return pl.pallas_call(
        _rwkv_sample_kernel,
        grid=(b // n,),
        in_specs=[pl.BlockSpec((n, RW_COLS), lambda i: (i, 0)), pl.BlockSpec((n, RW_COLS), lambda i: (i, 0)), st]
        + [full(a) for a in args],
        out_specs=[pl.BlockSpec((n, RW_WIDTH), lambda i: (i, 0)), st],
        out_shape=[jax.ShapeDtypeStruct((b, RW_WIDTH), F32), jax.ShapeDtypeStruct(state.shape, F32)],
        scratch_shapes=[pltpu.VMEM((5, 2, n, RW_WIDTH), F32),
                        pltpu.VMEM((RW_PAIRS, LANES, LANES), F32),
                        pltpu.VMEM((RW_PAIRS, LANES, LANES), F32)],
        compiler_params=_cparams(("parallel",)),
        name="rwkv_sample",
    )(p_rw, shift, state, *args)


def _t5_bucket_host(dist):
    d = jnp.maximum(dist, 0)
    large = N_BUCKETS // 2 + sum((d >= th).astype(jnp.int32) for th in T5_THRESH)
    return jnp.where(d < N_BUCKETS // 2, d, large)


def _fox_sample_kernel(pt_ref, q_ref, kn_ref, vn_ref, lfn_ref, *refs, n_pages, page):
    kp, vp, lp = refs[0:n_pages], refs[n_pages:2 * n_pages], refs[2 * n_pages:3 * n_pages]
    o_ref = refs[3 * n_pages]
    q = q_ref[0]
    bd = (lax.broadcasted_iota(jnp.int32, (FOX_HEADS, FOX_WIDTH), 1) // HEAD_DIM
          == lax.broadcasted_iota(jnp.int32, (FOX_HEADS, FOX_WIDTH), 0))
    qbd_f = jnp.where(bd, q.astype(F32), 0.0)
    qbd = qbd_f.astype(BF16)
    later = (lax.broadcasted_iota(jnp.int32, (page, page), 0)
             > lax.broadcasted_iota(jnp.int32, (page, page), 1)).astype(BF16)
    nt = (((1,), (1,)), ((), ()))
    tn = (((0,), (0,)), ((), ()))
    logits = [None] * n_pages
    acc_after = jnp.zeros((FOX_HEADS, 1), F32)
    for j in reversed(range(n_pages)):
        lf = jnp.concatenate([lp[j][0], jnp.zeros((page, LANES - FOX_HEADS), F32)], axis=1)
        lft = lf.T[0:FOX_HEADS]
        within = sum(jnp.dot(piece, later, preferred_element_type=F32) for piece in _split3(lft))
        s = lax.dot_general(qbd, kp[j][0].astype(BF16), nt, preferred_element_type=F32)
        logits[j] = s + within + acc_after
        acc_after = acc_after + jnp.sum(lft, axis=1, keepdims=True)
    own = (lax.broadcasted_iota(jnp.int32, (FOX_HEADS, LANES), 0)
           == lax.broadcasted_iota(jnp.int32, (FOX_HEADS, LANES), 1))
    lfn_col = jnp.sum(jnp.where(own, lfn_ref[0], 0.0), axis=1, keepdims=True)
    logits = [lg + lfn_col for lg in logits]
    kn = kn_ref[0].astype(BF16).astype(F32)
    vn = vn_ref[0].astype(BF16).astype(F32)
    l_new = jnp.sum(qbd_f * kn, axis=1, keepdims=True)
    m = l_new
    for lg in logits:
        m = jnp.maximum(m, jnp.max(lg, axis=1, keepdims=True))
    p_new = jnp.exp(l_new - m)
    denom = p_new
    acc = p_new * vn
    for j in range(n_pages):
        p = jnp.exp(logits[j] - m)
        denom = denom + jnp.sum(p, axis=1, keepdims=True)
        acc = acc + jnp.dot(p.astype(BF16), vp[j][0].astype(BF16), preferred_element_type=F32)
    out = jnp.where(bd, acc / denom, 0.0)
    o_ref[0] = jnp.sum(out, axis=0, keepdims=True)


def fox_sample(pt, qbf, k_new, v_new, lf_new, cache_k, cache_v, cache_lf):
    b, n_pages = pt.shape
    page = cache_k.shape[1]
    row3 = lambda a: a.reshape(b, 1, a.shape[-1])
    one = lambda width: pl.BlockSpec((1, 1, width), lambda i, pt: (i, 0, 0))
    paged = lambda width, j: pl.BlockSpec((1, page, width), lambda i, pt, j=j: (pt[i, j], 0, 0))
    in_specs = ([one(FOX_WIDTH), one(FOX_WIDTH), one(FOX_WIDTH), one(LANES)]
                + [paged(FOX_WIDTH, j) for j in range(n_pages)]
                + [paged(FOX_WIDTH, j) for j in range(n_pages)]
                + [paged(FOX_HEADS, j) for j in range(n_pages)])
    grid_spec = pltpu.PrefetchScalarGridSpec(
        num_scalar_prefetch=1, grid=(b,), in_specs=in_specs, out_specs=one(FOX_WIDTH))
    out = pl.pallas_call(
        functools.partial(_fox_sample_kernel, n_pages=n_pages, page=page),
        grid_spec=grid_spec,
        out_shape=jax.ShapeDtypeStruct((b, 1, FOX_WIDTH), F32),
        compiler_params=_cparams(("parallel",)),
        name="fox_sample",
    )(pt, row3(qbf), row3(k_new), row3(v_new), row3(lf_new),
      *([cache_k] * n_pages), *([cache_v] * n_pages), *([cache_lf] * n_pages))
    return out.reshape(b, FOX_WIDTH)


NSA_KV_COLS = 4 * HEAD_DIM
NSA_SPROJ_COLS = NSA_HEADS * NSA_KV_COLS + 512 + 256 + LANES


def nsa_sample_proj_weight(w):
    q, kc, vc, ks, vs, kw, vw, gl = jnp.split(w, NSA_SPLITS, axis=-1)
    blocks = []
    for h in range(NSA_HEADS):
        g = h // NSA_REP
        qh = q[:, h * HEAD_DIM:(h + 1) * HEAD_DIM]
        blocks.append(jnp.pad(qh, ((0, 0), (g * HEAD_DIM, NSA_KV_COLS - (g + 1) * HEAD_DIM))))
    return jnp.concatenate(blocks + [kc, vc, ks, vs, kw, vw, _pad_cols(gl, LANES)], axis=1).astype(BF16)


def _nsa_sample_proj_kernel(x_ref, w_ref, q_ref, kv_ref, kwv_ref, gl_ref):
    x = x_ref[...]
    d = lambda a, b: jnp.dot(x, w_ref[:, a:b], preferred_element_type=F32)
    nq = NSA_HEADS * NSA_KV_COLS
    q_ref[...] = (d(0, nq) * ATT_SCALE).astype(BF16)
    kv_ref[...] = d(nq, nq + 512)
    kwv_ref[...] = d(nq + 512, nq + 768)
    gl_ref[...] = d(nq + 768, nq + 768 + LANES)


def nsa_sample_proj(x_bf, w_bf):
    m, dm = x_bf.shape
    nq = NSA_HEADS * NSA_KV_COLS
    full = lambda a, c: pl.BlockSpec((a, c), lambda i: (0, 0))
    return pl.pallas_call(
        _nsa_sample_proj_kernel,
        grid=(1,),
        in_specs=[full(m, dm), full(dm, NSA_SPROJ_COLS)],
        out_specs=[full(m, nq), full(m, 512), full(m, 256), full(m, LANES)],
        out_shape=[jax.ShapeDtypeStruct((m, nq), BF16), jax.ShapeDtypeStruct((m, 512), F32),
                   jax.ShapeDtypeStruct((m, 256), F32), jax.ShapeDtypeStruct((m, LANES), F32)],
        compiler_params=_cparams(("arbitrary",)),
        name="nsa_sample_proj",
    )(x_bf, w_bf)


def nsa_sample_compress_weight(cmp_w, cmp_b):
    w = cmp_w.reshape(2, 2, CMP_STRIDE, HEAD_DIM, HEAD_DIM)
    eye = jnp.eye(2, dtype=F32)
    t = jnp.einsum('whjde,wv,gk->jwgdhvke', w, eye, eye)
    bias = jnp.broadcast_to(cmp_b[:, None, :], (2, NSA_KV_HEADS, HEAD_DIM)).reshape(1, NSA_KV_COLS)
    return t.reshape(CMP_STRIDE, NSA_KV_COLS, 2 * NSA_KV_COLS).astype(BF16), bias.astype(F32)


def _masked_softmax_rows(parts, valids):
    m = jnp.full((parts[0].shape[0], 1), NEG, F32)
    for lg, ok in zip(parts, valids):
        m = jnp.maximum(m, jnp.max(jnp.where(ok, lg, NEG), axis=1, keepdims=True))
    es = [jnp.where(ok, jnp.exp(jnp.where(ok, lg, NEG) - m), 0.0) for lg, ok in zip(parts, valids)]
    den = sum(jnp.sum(e, axis=1, keepdims=True) for e in es)
    return es, jnp.maximum(den, 1e-30)


def _nsa_sample_kernel(pt_ref, q_ref, kvn_ref, kwn_ref, win_ref, wc_ref, cb_ref, bc_ref, bs_ref, bw_ref, b0_ref,
                       ex_ref, *refs, n_pages, page):
    chunk_pages = refs[0:n_pages]
    kv_pages = refs[n_pages:2 * n_pages]
    o_ref, wout_ref = refs[2 * n_pages], refs[2 * n_pages + 1]
    nt = (((1,), (1,)), ((), ()))
    q = q_ref[0]
    qf = q.astype(F32)
    b0 = b0_ref[:, 0:1]
    past = n_pages * page
    ncr = past // CMP_STRIDE

    x = jnp.concatenate([c[0] for c in chunk_pages], axis=0)
    acc = jnp.zeros((ncr, 2 * NSA_KV_COLS), F32)
    for j in range(CMP_STRIDE):
        acc = acc + jnp.dot(x[:, j * 512:j * 512 + NSA_KV_COLS].astype(BF16), wc_ref[j],
                            preferred_element_type=F32)
    kcvc = (acc[:, 0:NSA_KV_COLS] + pltpu.roll(acc[:, NSA_KV_COLS:], ncr - 1, axis=0) + cb_ref[...]).astype(BF16)

    lane_c = lax.broadcasted_iota(jnp.int32, (NSA_HEADS, ncr), 1)
    ok_c = lane_c < ncr - 1
    lc = lax.dot_general(q, kcvc, nt, preferred_element_type=F32) + bc_ref[...]
    (e_c,), den_c = _masked_softmax_rows([lc], [ok_c])
    p_c = (e_c / den_c).astype(BF16)
    o_ref[0, 0] = jnp.dot(p_c, kcvc, preferred_element_type=F32)

    ci = lax.broadcasted_iota(jnp.int32, (ncr, LANES), 0) * CMP_STRIDE
    cj = lax.broadcasted_iota(jnp.int32, (ncr, LANES), 1)
    cover = jnp.where((ci < (cj + 1) * SEL_BLOCK) & (ci + CMP_BLOCK > cj * SEL_BLOCK), 1.0, 0.0).astype(BF16)
    imp8 = jnp.dot(p_c, cover, preferred_element_type=F32)
    row8 = lax.broadcasted_iota(jnp.int32, (NSA_HEADS, LANES), 0)
    imp = jnp.where(row8 < NSA_REP, jnp.sum(imp8[0:NSA_REP], axis=0, keepdims=True),
                    jnp.sum(imp8[NSA_REP:], axis=0, keepdims=True))
    lane = lax.broadcasted_iota(jnp.int32, (NSA_HEADS, LANES), 1)
    lane_f = lane.astype(F32)
    jq = past // SEL_BLOCK
    forced = (lane == 0) | (lane == jq) | (lane == jq - 1)
    work = jnp.where(lane <= jq, imp + jnp.where(forced, SEL_FORCE, 0.0), NOT_ALLOWED)
    sel = jnp.zeros((NSA_HEADS, LANES), F32)
    for _ in range(min(N_SELECT, jq + 1)):
        mx = jnp.max(work, axis=-1, keepdims=True)
        idx = jnp.min(jnp.where(work == mx, lane_f, float(LANES)), axis=-1, keepdims=True)
        hit = lane_f == idx
        sel = jnp.where(hit & (mx > 0.5 * NOT_ALLOWED), 1.0, sel)
        work = jnp.where(hit, REMOVED, work)
    sel_bf = sel.astype(BF16)

    kvs = [kv_pages[j][0].astype(BF16) for j in range(n_pages)]
    parts, oks = [], []
    for j in range(n_pages):
        parts.append(lax.dot_general(q, kvs[j], nt, preferred_element_type=F32) + bs_ref[:, j * page:(j + 1) * page])
        oks.append(jnp.dot(sel_bf, ex_ref[:, j * page:(j + 1) * page], preferred_element_type=F32) > 0.5)
    row_s = kvn_ref[0][:, 2 * NSA_KV_WIDTH:].astype(BF16).astype(F32)
    parts.append(jnp.sum(qf * row_s, axis=1, keepdims=True) + b0)
    oks.append(sel[:, jq:jq + 1] > 0.5)
    es, den = _masked_softmax_rows(parts, oks)
    o_s = es[n_pages] * row_s
    for j in range(n_pages):
        o_s = o_s + jnp.dot(es[j].astype(BF16), kvs[j], preferred_element_type=F32)
    o_ref[0, 1] = o_s / den

    win = win_ref[0]
    wb = win.shape[0]
    win_bf = win.astype(BF16)
    lane_w = lax.broadcasted_iota(jnp.int32, (NSA_HEADS, wb), 1)
    row_w = kwn_ref[0].astype(BF16).astype(F32)
    parts = [lax.dot_general(q, win_bf, nt, preferred_element_type=F32) + bw_ref[...],
             jnp.sum(qf * row_w, axis=1, keepdims=True) + b0]
    oks = [wb - lane_w < WINDOW, jnp.full((NSA_HEADS, 1), True)]
    es, den = _masked_softmax_rows(parts, oks)
    o_ref[0, 2] = (jnp.dot(es[0].astype(BF16), win_bf, preferred_element_type=F32) + es[1] * row_w) / den

    last = lax.broadcasted_iota(jnp.int32, (wb, 1), 0) == wb - 1
    wout_ref[0] = jnp.where(last, kwn_ref[0], pltpu.roll(win, wb - 1, axis=0))


def nsa_sample(pt, qsel, kv_new, kwv_new, win, cache, wc, cb, rel_bias):
    b, n_pages = pt.shape
    page = cache.shape[1]
    past = n_pages * page
    wb = win.shape[1]
    tab = lambda dist: rel_bias[_t5_bucket_host(dist)].T.astype(F32)
    bc = tab(past - (jnp.arange(past // CMP_STRIDE) * CMP_STRIDE + CMP_BLOCK - 1))
    bs = tab(past - jnp.arange(past))
    bw = tab(wb - jnp.arange(wb))
    b0 = jnp.broadcast_to(rel_bias[0][:, None], (NSA_HEADS, LANES)).astype(F32)
    expand = (jnp.arange(LANES)[:, None] == jnp.arange(past)[None, :] // SEL_BLOCK).astype(BF16)
    chunks = cache.reshape(cache.shape[0], page // CMP_STRIDE, CMP_STRIDE * 512)
    one = lambda *shape: pl.BlockSpec((1,) + shape, lambda i, pt: (i,) + (0,) * len(shape))
    full = lambda a: pl.BlockSpec(a.shape, lambda i, pt: (0,) * a.ndim)
    in_specs = ([one(NSA_HEADS, NSA_KV_COLS), one(1, 512), one(1, 256), one(wb, 256)]
                + [full(a) for a in (wc, cb, bc, bs, bw, b0, expand)]
                + [pl.BlockSpec((1, page // CMP_STRIDE, CMP_STRIDE * 512), lambda i, pt, j=j: (pt[i, j], 0, 0))
                   for j in range(n_pages)]
                + [pl.BlockSpec((1, page, NSA_KV_COLS), lambda i, pt, j=j: (pt[i, j], 0, 1)) for j in range(n_pages)])
    grid_spec = pltpu.PrefetchScalarGridSpec(
        num_scalar_prefetch=1, grid=(b,), in_specs=in_specs,
        out_specs=[one(3, NSA_HEADS, NSA_KV_COLS), one(wb, 256)])
    return pl.pallas_call(
        functools.partial(_nsa_sample_kernel, n_pages=n_pages, page=page),
        grid_spec=grid_spec,
        out_shape=[jax.ShapeDtypeStruct((b, 3, NSA_HEADS, NSA_KV_COLS), F32),
                   jax.ShapeDtypeStruct((b, wb, 256), F32)],
        compiler_params=_cparams(("parallel",)),
        name="nsa_sample",
    )(pt, qsel, kv_new.reshape(b, 1, 512), kwv_new.reshape(b, 1, 256), win, wc, cb, bc, bs, bw, b0, expand,
      *([chunks] * n_pages), *([cache] * n_pages))


def nsa_sample_heads(o3):
    b = o3.shape[0]
    v = o3[..., 2 * HEAD_DIM:].reshape(b, 3, NSA_KV_HEADS, NSA_REP, NSA_KV_HEADS, HEAD_DIM)
    o = jnp.stack([v[:, :, g, :, g] for g in range(NSA_KV_HEADS)], axis=2).reshape(b, 3, NSA_WIDTH)
    return o[:, 0], o[:, 1], o[:, 2]


def kernel(x_prompt, x_sample, cache_nsa_kv, cache_fox_k, cache_fox_v, cache_fox_logf, state_nsa_win, state_rwkv_wkv, state_rwkv_shift, page_table, ln_g, ln_b, ffn_w_gate, ffn_w_up, ffn_w_down, w_in, w_out, rw_mu, rw_w0, rw_w2, rw_a0, rw_a2, rw_g2, rw_k_k, rw_k_a, rw_r_k, rw_gn_g, rw_gn_b, nsa_cmp_w, nsa_cmp_b, nsa_out_g, rel_bias, fox_b_f, fox_out_g):
    db = x_sample.shape[0]
    t = x_prompt.shape[1]

    wg_bf = ffn_w_gate.astype(BF16)
    wu_bf = ffn_w_up.astype(BF16)
    wd_bf = ffn_w_down.astype(BF16)
    wout_bf = w_out.astype(BF16)
    w_rw = w_in[:, :, :RW_COLS].astype(BF16)
    w_fox = jnp.pad(w_in[:, :, RW_COLS + NSA_COLS:].astype(BF16), ((0, 0), (0, 0), (0, 3200 - FOX_COLS)))

    def ffn_sub(x, x_bf, l, j):
        h = ffn_in(x_bf, wg_bf[l, j], wu_bf[l, j])
        return ffn_out(h, wd_bf[l, j], x, ln_g[l, 2 * j], ln_b[l, 2 * j])

    def rw_params(l):
        return (rw_mu[l], rw_w0[l], rw_w2[l], rw_a0[l], rw_a2[l], rw_g2[l], rw_k_k[l], rw_k_a[l],
                rw_r_k[l], rw_gn_g[l], rw_gn_b[l])

    xp = x_prompt[0]
    xp_bf = xp.astype(BF16)
    p_kv, p_fk, p_fv, p_fl, p_win, p_wkv, p_shift = [], [], [], [], [], [], []
    btab = bias_tiles(rel_bias, NSA_TILE)
    wb = min(WINDOW, t)
    for l in range(DEPTH):
        xp, xp_bf = ffn_sub(xp, xp_bf, l, 0)
        p_rw = proj(xp_bf, w_rw[l])
        nq, kv4, kwv, gl, dup = nsa_proj(xp_bf, nsa_proj_weight(w_in[l, :, RW_COLS:RW_COLS + NSA_COLS]))
        qbf, fk, fv, kbf, vbf, lf128 = fox_proj(xp_bf, w_fox[l], fox_b_f[l])
        o_rw, s_wkv = rwkv_prompt(p_rw, t, rw_params(l))
        wmat, cbias = nsa_compress_weight(nsa_cmp_w[l], nsa_cmp_b[l])
        o_c, o_s, o_w = nsa_prompt_attn(nq, kv4, dup, wmat, cbias, rel_bias, btab, t)
        o_fox = fox_prompt_attn(qbf, kbf, vbf, lf128, t)
        xp, xp_bf = mix_out(o_rw, o_c, o_s, o_w, gl, o_fox, xp, wout_bf[l], nsa_out_g[l], fox_out_g[l],
                            ln_g[l, 1], ln_b[l, 1])
        xp, xp_bf = ffn_sub(xp, xp_bf, l, 1)
        p_kv.append(kv4.reshape(1, t, 4, NSA_KV_HEADS, HEAD_DIM))
        p_fk.append(fk.reshape(1, t, FOX_HEADS, HEAD_DIM))
        p_fv.append(fv.reshape(1, t, FOX_HEADS, HEAD_DIM))
        p_fl.append(lf128[None, :, :FOX_HEADS])
        p_win.append(kwv[t - wb:].reshape(1, wb, 2, NSA_KV_HEADS, HEAD_DIM))
        p_wkv.append(s_wkv[None])
        p_shift.append(p_rw[t - 1:t])

    xs = x_sample[:, 0]
    xs_bf = xs.astype(BF16)
    s_kv, s_fk, s_fv, s_fl, s_win, s_wkv_l, s_shift_l = [], [], [], [], [], [], []
    n_phys, page = cache_nsa_kv.shape[1], cache_nsa_kv.shape[2]
    nsa_cache = cache_nsa_kv.reshape(DEPTH * n_phys, page, 512)
    fox_k_cache = cache_fox_k.reshape(DEPTH * n_phys, page, FOX_WIDTH)
    fox_v_cache = cache_fox_v.reshape(DEPTH * n_phys, page, FOX_WIDTH)
    fox_lf_cache = cache_fox_logf.reshape(DEPTH * n_phys, page, FOX_HEADS)
    swb = state_nsa_win.shape[2]
    for l in range(DEPTH):
        xs, xs_bf = ffn_sub(xs, xs_bf, l, 0)
        pt = page_table + l * n_phys
        p_rw = proj(xs_bf, w_rw[l])
        qsel, kv_new, kwv_new, gl = nsa_sample_proj(
            xs_bf, nsa_sample_proj_weight(w_in[l, :, RW_COLS:RW_COLS + NSA_COLS]))
        qbf, fk, fv, kbf, vbf, lf128 = fox_proj(xs_bf, w_fox[l], fox_b_f[l])
        o_rw, s_wkv = rwkv_sample(p_rw, state_rwkv_shift[l], state_rwkv_wkv[l], rw_params(l))
        wc, cb = nsa_sample_compress_weight(nsa_cmp_w[l], nsa_cmp_b[l])
        o3, win = nsa_sample(pt, qsel.reshape(db, NSA_HEADS, NSA_KV_COLS), kv_new, kwv_new,
                             state_nsa_win[l].reshape(db, swb, 256), nsa_cache, wc, cb, rel_bias)
        o_c, o_s, o_w = nsa_sample_heads(o3)
        o_fox = fox_sample(pt, qbf, fk, fv, lf128, fox_k_cache, fox_v_cache, fox_lf_cache)
        xs, xs_bf = mix_out(o_rw, o_c, o_s, o_w, gl, o_fox, xs, wout_bf[l],
                            nsa_out_g[l], fox_out_g[l], ln_g[l, 1], ln_b[l, 1])
        xs, xs_bf = ffn_sub(xs, xs_bf, l, 1)
        s_kv.append(kv_new.reshape(db, 1, 4, NSA_KV_HEADS, HEAD_DIM))
        s_fk.append(fk.reshape(db, 1, FOX_HEADS, HEAD_DIM))
        s_fv.append(fv.reshape(db, 1, FOX_HEADS, HEAD_DIM))
        s_fl.append(lf128[:, None, :FOX_HEADS])
        s_win.append(win.reshape(db, swb, 2, NSA_KV_HEADS, HEAD_DIM))
        s_wkv_l.append(s_wkv)
        s_shift_l.append(p_rw)

    return (xp[None], xs[:, None],
            jnp.stack(p_kv), jnp.stack(p_fk), jnp.stack(p_fv), jnp.stack(p_fl),
            jnp.stack(p_win), jnp.stack(p_wkv), jnp.stack(p_shift),
            jnp.stack(s_kv), jnp.stack(s_fk), jnp.stack(s_fv), jnp.stack(s_fl),
            jnp.stack(s_win), jnp.stack(s_wkv_l), jnp.stack(s_shift_l))
```

```python
import functools
import math

import jax
import jax.numpy as jnp
from jax import lax
from jax.experimental import pallas as pl
from jax.experimental.pallas import tpu as pltpu

F32 = jnp.float32
BF16 = jnp.bfloat16

D_MODEL = 2048
DEPTH = 2
HEAD_DIM = 64
RW_HEADS = 8
RW_WIDTH = 512
DECAY_LORA = 64
AAA_LORA = 64
GATE_LORA = 128
RW_COLS = 3 * RW_WIDTH + DECAY_LORA + AAA_LORA + GATE_LORA
RW_SPLITS = [RW_WIDTH, 2 * RW_WIDTH, 3 * RW_WIDTH, 3 * RW_WIDTH + DECAY_LORA, 3 * RW_WIDTH + DECAY_LORA + AAA_LORA]
RW_GN_EPS = 64e-5
NSA_HEADS = 8
NSA_KV_HEADS = 2
NSA_REP = NSA_HEADS // NSA_KV_HEADS
NSA_WIDTH = 512
NSA_KV_WIDTH = 128
NSA_COLS = NSA_WIDTH + 6 * NSA_KV_WIDTH + 3 * NSA_HEADS
NSA_SPLITS = [NSA_WIDTH + i * NSA_KV_WIDTH for i in range(7)]
CMP_BLOCK = 32
CMP_STRIDE = 16
SEL_BLOCK = 64
N_SELECT = 16
WINDOW = 512
SEL_FORCE = 1e4
FOX_HEADS = 16
FOX_WIDTH = 1024
FOX_COLS = 3 * FOX_WIDTH + FOX_HEADS
FOX_SPLITS = [FOX_WIDTH, 2 * FOX_WIDTH, 3 * FOX_WIDTH]
IN_SPLITS = [RW_COLS, RW_COLS + NSA_COLS]
N_BUCKETS = 32
MAX_DISTANCE = 1024
Q_BLOCK = 128
DEEPNORM_ALPHA = (2 * DEPTH) ** 0.25
LN_EPS = 1e-5
RMS_EPS = 1e-6
ATT_SCALE = HEAD_DIM ** -0.5

LANES = 128
VMEM_LIMIT = 56 * 1024 * 1024
NEG = -1e30


def _cparams(sem):
    return pltpu.CompilerParams(dimension_semantics=sem, vmem_limit_bytes=VMEM_LIMIT)


def _pad_cols(w, n):
    return jnp.pad(w, ((0, 0), (0, n - w.shape[1])))


def _ffn_in_kernel(x_ref, wg_ref, wu_ref, h_ref):
    x = x_ref[...]
    g = jnp.dot(x, wg_ref[...], preferred_element_type=F32)
    u = jnp.dot(x, wu_ref[...], preferred_element_type=F32)
    h_ref[...] = (g * (1.0 / (1.0 + jnp.exp(-g))) * u).astype(BF16)


def ffn_in(x_bf, wg, wu):
    m, d = x_bf.shape
    f = wg.shape[1]
    tm = min(m, 1024)
    tn = 512
    return pl.pallas_call(
        _ffn_in_kernel,
        grid=(m // tm, f // tn),
        in_specs=[pl.BlockSpec((tm, d), lambda i, j: (i, 0)),
                  pl.BlockSpec((d, tn), lambda i, j: (0, j)),
                  pl.BlockSpec((d, tn), lambda i, j: (0, j))],
        out_specs=pl.BlockSpec((tm, tn), lambda i, j: (i, j)),
        out_shape=jax.ShapeDtypeStruct((m, f), BF16),
        compiler_params=_cparams(("parallel", "arbitrary")),
        name="ffn_in",
    )(x_bf, wg, wu)


def _layer_norm(y, g, b):
    mu = jnp.mean(y, -1, keepdims=True)
    d = y - mu
    var = jnp.mean(d * d, -1, keepdims=True)
    return d * lax.rsqrt(var + LN_EPS) * g + b


def _ffn_out_kernel(h_ref, wd_ref, x_ref, g_ref, b_ref, y_ref, ybf_ref, acc_ref):
    k = pl.program_id(1)

    @pl.when(k == 0)
    def _():
        acc_ref[...] = jnp.zeros_like(acc_ref)

    acc_ref[...] += jnp.dot(h_ref[...], wd_ref[...], preferred_element_type=F32)

    @pl.when(k == pl.num_programs(1) - 1)
    def _():
        y = _layer_norm(DEEPNORM_ALPHA * x_ref[...] + 0.5 * acc_ref[...], g_ref[...], b_ref[...])
        y_ref[...] = y
        ybf_ref[...] = y.astype(BF16)


def ffn_out(h_bf, wd, x, g, b):
    m, f = h_bf.shape
    d = wd.shape[1]
    tm = min(m, 512)
    tk = 512
    return pl.pallas_call(
        _ffn_out_kernel,
        grid=(m // tm, f // tk),
        in_specs=[pl.BlockSpec((tm, tk), lambda i, k: (i, k)),
                  pl.BlockSpec((tk, d), lambda i, k: (k, 0)),
                  pl.BlockSpec((tm, d), lambda i, k: (i, 0)),
                  pl.BlockSpec((1, d), lambda i, k: (0, 0)),
                  pl.BlockSpec((1, d), lambda i, k: (0, 0))],
        out_specs=[pl.BlockSpec((tm, d), lambda i, k: (i, 0)),
                   pl.BlockSpec((tm, d), lambda i, k: (i, 0))],
        out_shape=[jax.ShapeDtypeStruct((m, d), F32), jax.ShapeDtypeStruct((m, d), BF16)],
        scratch_shapes=[pltpu.VMEM((tm, d), F32)],
        compiler_params=_cparams(("parallel", "arbitrary")),
        name="ffn_out",
    )(h_bf, wd, x, g.reshape(1, d), b.reshape(1, d))


def _proj_kernel(x_ref, w_ref, o_ref):
    o_ref[...] = jnp.dot(x_ref[...], w_ref[...], preferred_element_type=F32)


def proj(x_bf, w_bf):
    m, d = x_bf.shape
    n = w_bf.shape[1]
    tm = min(m, 512)
    tn = 128
    for c in (640, 512, 384, 256):
        if n % c == 0:
            tn = c
            break
    return pl.pallas_call(
        _proj_kernel,
        grid=(m // tm, n // tn),
        in_specs=[pl.BlockSpec((tm, d), lambda i, j: (i, 0)),
                  pl.BlockSpec((d, tn), lambda i, j: (0, j))],
        out_specs=pl.BlockSpec((tm, tn), lambda i, j: (i, j)),
        out_shape=jax.ShapeDtypeStruct((m, n), F32),
        compiler_params=_cparams(("parallel", "arbitrary")),
        name="proj",
    )(x_bf, w_bf)


def _log_sigmoid(x):
    return jnp.minimum(x, 0.0) - jnp.log1p(jnp.exp(-jnp.abs(x)))


def _fox_proj_kernel(x_ref, w_ref, bf_ref, qbf_ref, k_ref, v_ref, kbf_ref, vbf_ref, lf_ref):
    x = x_ref[...]
    q = jnp.dot(x, w_ref[:, 0:FOX_WIDTH], preferred_element_type=F32)
    qbf_ref[...] = (q * ATT_SCALE).astype(BF16)
    k = jnp.dot(x, w_ref[:, FOX_WIDTH:2 * FOX_WIDTH], preferred_element_type=F32)
    k_ref[...] = k
    kbf_ref[...] = k.astype(BF16)
    v = jnp.dot(x, w_ref[:, 2 * FOX_WIDTH:3 * FOX_WIDTH], preferred_element_type=F32)
    v_ref[...] = v
    vbf_ref[...] = v.astype(BF16)
    f = jnp.dot(x, w_ref[:, 3 * FOX_WIDTH:3 * FOX_WIDTH + LANES], preferred_element_type=F32)
    lf_ref[...] = _log_sigmoid(f + bf_ref[...])


def fox_proj(x_bf, w_bf, b_f):
    m, d = x_bf.shape
    n = w_bf.shape[1]
    tm = min(m, 256)
    row = lambda width: pl.BlockSpec((tm, width), lambda i: (i, 0))
    bias = jnp.pad(b_f.astype(F32), (0, LANES - FOX_HEADS)).reshape(1, LANES)
    return pl.pallas_call(
        _fox_proj_kernel,
        grid=(m // tm,),
        in_specs=[row(d), pl.BlockSpec((d, n), lambda i: (0, 0)), pl.BlockSpec((1, LANES), lambda i: (0, 0))],
        out_specs=[row(FOX_WIDTH)] * 5 + [row(LANES)],
        out_shape=[jax.ShapeDtypeStruct((m, FOX_WIDTH), BF16),
                   jax.ShapeDtypeStruct((m, FOX_WIDTH), F32), jax.ShapeDtypeStruct((m, FOX_WIDTH), F32),
                   jax.ShapeDtypeStruct((m, FOX_WIDTH), BF16), jax.ShapeDtypeStruct((m, FOX_WIDTH), BF16),
                   jax.ShapeDtypeStruct((m, LANES), F32)],
        compiler_params=_cparams(("parallel",)),
        name="fox_proj",
    )(x_bf, w_bf, bias)


def _split3(x):
    hi = x.astype(BF16)
    r = x - hi.astype(F32)
    mid = r.astype(BF16)
    lo = (r - mid.astype(F32)).astype(BF16)
    return hi, mid, lo


def _seg_sum(x, ones_blk):
    hi, mid, lo = _split3(x)
    d = lambda a: jnp.dot(a, ones_blk, preferred_element_type=F32)
    return d(hi) + d(mid) + d(lo)


def _head_ones():
    r = lax.broadcasted_iota(jnp.int32, (LANES, LANES), 0) // HEAD_DIM
    c = lax.broadcasted_iota(jnp.int32, (LANES, LANES), 1) // HEAD_DIM
    return (r == c).astype(BF16)


def _mix_out_kernel(orw_ref, oc_ref, os_ref, ow_ref, gl_ref, ofox_ref, x_ref, w_ref, gn_ref, gf_ref, g_ref, b_ref,
                    y_ref, ybf_ref):
    ones_blk = _head_ones()
    gate = 1.0 / (1.0 + jnp.exp(-gl_ref[...]))
    pieces = _split3(gate)
    er = lax.broadcasted_iota(jnp.int32, (LANES, NSA_WIDTH), 0)
    ec = lax.broadcasted_iota(jnp.int32, (LANES, NSA_WIDTH), 1) // HEAD_DIM
    onsa = jnp.zeros(oc_ref.shape, F32)
    for c, br_ref in enumerate((oc_ref, os_ref, ow_ref)):
        spread = (er == ec * 3 + c).astype(BF16)
        gate_c = sum(jnp.dot(piece, spread, preferred_element_type=F32) for piece in pieces)
        onsa = onsa + gate_c * br_ref[...]

    def rms(o, gain):
        cols = []
        for c in range(o.shape[1] // LANES):
            blk = o[:, c * LANES:(c + 1) * LANES]
            ms = _seg_sum(blk * blk, ones_blk) * (1.0 / HEAD_DIM)
            cols.append(blk * lax.rsqrt(ms + RMS_EPS) * gain[:, c * LANES:(c + 1) * LANES])
        return jnp.concatenate(cols, axis=1)

    o = jnp.concatenate([orw_ref[...], rms(onsa, gn_ref[...]), rms(ofox_ref[...], gf_ref[...])], axis=1)
    acc = jnp.dot(o.astype(BF16), w_ref[...], preferred_element_type=F32)
    y = _layer_norm(DEEPNORM_ALPHA * x_ref[...] + acc, g_ref[...], b_ref[...])
    y_ref[...] = y
    ybf_ref[...] = y.astype(BF16)


def mix_out(o_rw, o_c, o_s, o_w, gl, o_fox, x, w_bf, gn, gf, g, b):
    m, d = x.shape
    tm = min(m, 256)
    row = lambda width: pl.BlockSpec((tm, width), lambda i: (i, 0))
    full = lambda a, c: pl.BlockSpec((a, c), lambda i: (0, 0))
    return pl.pallas_call(
        _mix_out_kernel,
        grid=(m // tm,),
        in_specs=[row(RW_WIDTH), row(NSA_WIDTH), row(NSA_WIDTH), row(NSA_WIDTH), row(LANES), row(FOX_WIDTH),
                  row(d), full(d, d), full(1, NSA_WIDTH), full(1, FOX_WIDTH), full(1, d), full(1, d)],
        out_specs=[row(d), row(d)],
        out_shape=[jax.ShapeDtypeStruct((m, d), F32), jax.ShapeDtypeStruct((m, d), BF16)],
        compiler_params=_cparams(("parallel",)),
        name="mix_out",
    )(o_rw, o_c, o_s, o_w, gl, o_fox, x, w_bf, gn.reshape(1, -1), gf.reshape(1, -1), g.reshape(1, d),
      b.reshape(1, d))


CUM_BLOCK = 512


def _cumsum_kernel(x_ref, o_ref, carry_ref):
    @pl.when(pl.program_id(0) == 0)
    def _():
        carry_ref[...] = jnp.zeros_like(carry_ref)

    r = lax.broadcasted_iota(jnp.int32, (CUM_BLOCK, CUM_BLOCK), 0)
    c = lax.broadcasted_iota(jnp.int32, (CUM_BLOCK, CUM_BLOCK), 1)
    upper = (r <= c).astype(BF16)
    hi, mid, lo = _split3(x_ref[...])
    d = lambda a: jnp.dot(a, upper, preferred_element_type=F32)
    cs = d(hi) + d(mid) + d(lo) + carry_ref[:, 0:1]
    o_ref[...] = cs
    carry_ref[...] = jnp.broadcast_to(cs[:, CUM_BLOCK - 1:CUM_BLOCK], carry_ref.shape)


def cumsum_lanes(x):
    rows, t = x.shape
    return pl.pallas_call(
        _cumsum_kernel,
        grid=(t // CUM_BLOCK,),
        in_specs=[pl.BlockSpec((rows, CUM_BLOCK), lambda i: (0, i))],
        out_specs=pl.BlockSpec((rows, CUM_BLOCK), lambda i: (0, i)),
        out_shape=jax.ShapeDtypeStruct((rows, t), F32),
        scratch_shapes=[pltpu.VMEM((rows, LANES), F32)],
        compiler_params=_cparams(("arbitrary",)),
        name="cumsum",
    )(x)


def _fox_flash_kernel(qt_ref, kt_ref, q_ref, k_ref, v_ref, ck_ref, o_ref, m_ref, l_ref, acc_ref, *, tq, tk):
    step = pl.program_id(1)
    qi = qt_ref[step]
    ki = kt_ref[step]

    @pl.when(ki == 0)
    def _():
        m_ref[...] = jnp.full_like(m_ref, NEG)
        l_ref[...] = jnp.zeros_like(l_ref)
        acc_ref[...] = jnp.zeros_like(acc_ref)

    left = lax.broadcasted_iota(jnp.int32, (1, LANES), 1) < HEAD_DIM

    def tile(diagonal):
        q = q_ref[...]
        k = k_ref[...]
        v = v_ref[...]
        pv, alphas = [], []
        for hh in range(2):
            qm = jnp.where(left if hh == 0 else jnp.logical_not(left), q, jnp.zeros_like(q))
            s = lax.dot_general(qm, k, (((1,), (1,)), ((), ())), preferred_element_type=F32)
            s = s - ck_ref[0, hh:hh + 1, :]
            if diagonal:
                rows = lax.broadcasted_iota(jnp.int32, (tq, tk), 0)
                cols = lax.broadcasted_iota(jnp.int32, (tq, tk), 1)
                s = jnp.where(rows >= cols, s, NEG)
            m_old = m_ref[hh]
            m_new = jnp.maximum(m_old, jnp.max(s, axis=-1, keepdims=True))
            p = jnp.exp(s - m_new)
            alpha = jnp.exp(m_old - m_new)
            l_ref[hh] = alpha * l_ref[hh] + jnp.sum(p, axis=-1, keepdims=True)
            m_ref[hh] = m_new
            pv.append(jnp.dot(p.astype(BF16), v, preferred_element_type=F32))
            alphas.append(alpha)
        acc_ref[...] = (jnp.where(left, alphas[0], alphas[1]) * acc_ref[...]
                        + jnp.where(left, pv[0], pv[1]))

    @pl.when(ki < qi)
    def _():
        tile(False)

    @pl.when(ki == qi)
    def _():
        tile(True)
        o_ref[...] = acc_ref[...] / jnp.where(left, l_ref[0], l_ref[1])


def _tri_steps(n):
    qs, ks = [], []
    for qi in range(n):
        for ki in range(qi + 1):
            qs.append(qi)
            ks.append(ki)
    return jnp.asarray(qs, jnp.int32), jnp.asarray(ks, jnp.int32)


def fox_flash(q_bf, k_bf, v_bf, ck, t):
    tq = tk = min(t, 512)
    n = t // tq
    qt, kt = _tri_steps(n)
    pairs = FOX_WIDTH // LANES
    grid_spec = pltpu.PrefetchScalarGridSpec(
        num_scalar_prefetch=2,
        grid=(pairs, qt.shape[0]),
        in_specs=[pl.BlockSpec((tq, LANES), lambda p, s, qt, kt: (qt[s], p)),
                  pl.BlockSpec((tk, LANES), lambda p, s, qt, kt: (kt[s], p)),
                  pl.BlockSpec((tk, LANES), lambda p, s, qt, kt: (kt[s], p)),
                  pl.BlockSpec((1, 8, tk), lambda p, s, qt, kt: (p, 0, kt[s]))],
        out_specs=pl.BlockSpec((tq, LANES), lambda p, s, qt, kt: (qt[s], p)),
        scratch_shapes=[pltpu.VMEM((2, tq, 1), F32), pltpu.VMEM((2, tq, 1), F32), pltpu.VMEM((tq, LANES), F32)],
    )
    return pl.pallas_call(
        functools.partial(_fox_flash_kernel, tq=tq, tk=tk),
        grid_spec=grid_spec,
        out_shape=jax.ShapeDtypeStruct((t, FOX_WIDTH), F32),
        compiler_params=_cparams(("parallel", "arbitrary")),
        name="fox_flash",
    )(qt, kt, q_bf, k_bf, v_bf, ck)


def fox_prompt_attn(qbf, kbf, vbf, logf128, t):
    lf = logf128[:t, :FOX_HEADS]
    lf_t = jnp.pad(lf.T.reshape(FOX_HEADS // 2, 2, t), ((0, 0), (0, 6), (0, 0))).reshape(-1, t)
    ck = cumsum_lanes(lf_t).reshape(FOX_HEADS // 2, 8, t)
    return fox_flash(qbf, kbf, vbf, ck, t)


T5_THRESH = (21, 27, 35, 46, 59, 77, 99, 128, 166, 216, 280, 363, 470, 609, 790)
NSA_TILE = 512
NSA_DELTAS = 4
NOT_ALLOWED = -1e30
REMOVED = -2e30


def _t5_bucket_int(dist):
    d = jnp.maximum(dist, 0)
    large = jnp.full(d.shape, N_BUCKETS // 2, jnp.int32)
    for th in T5_THRESH:
        large = large + (d >= th).astype(jnp.int32)
    return jnp.where(d < N_BUCKETS // 2, d, large)


def _bias_lookup(bucket, tab_ref, h):
    val = jnp.full(bucket.shape, tab_ref[0, h], F32)
    for b in range(1, N_BUCKETS):
        val = jnp.where(bucket == b, tab_ref[b, h], val)
    return val


def _bias_tab_kernel(tab_ref, o_ref, *, tb):
    h = pl.program_id(0)
    dl = pl.program_id(1)
    r = lax.broadcasted_iota(jnp.int32, (tb, tb), 0)
    c = lax.broadcasted_iota(jnp.int32, (tb, tb), 1)
    o_ref[0, 0] = _bias_lookup(_t5_bucket_int(dl * tb + r - c), tab_ref, h)


def bias_tiles(rel_bias, tb):
    return pl.pallas_call(
        functools.partial(_bias_tab_kernel, tb=tb),
        grid=(NSA_HEADS, NSA_DELTAS),
        in_specs=[pl.BlockSpec(memory_space=pltpu.SMEM)],
        out_specs=pl.BlockSpec((1, 1, tb, tb), lambda h, d: (h, d, 0, 0)),
        out_shape=jax.ShapeDtypeStruct((NSA_HEADS, NSA_DELTAS, tb, tb), F32),
        compiler_params=_cparams(("parallel", "parallel")),
        name="nsa_bias_tiles",
    )(rel_bias)


NSA_PROJ_COLS = 2432


def nsa_proj_weight(w):
    q, kc, vc, ks, vs, kw, vw, gl = jnp.split(w, NSA_SPLITS, axis=-1)
    dup = []
    for src in (ks, vs, kw, vw):
        for g in range(NSA_KV_HEADS):
            blk = src[:, g * HEAD_DIM:(g + 1) * HEAD_DIM]
            dup += [blk, blk]
    return jnp.concatenate([q, kc, vc, ks, vs, kw, vw, _pad_cols(gl, LANES)] + dup, axis=1).astype(BF16)


def _nsa_proj_kernel(x_ref, w_ref, qbf_ref, kv_ref, kwv_ref, gl_ref, dup_ref):
    x = x_ref[...]
    d = lambda a, b: jnp.dot(x, w_ref[:, a:b], preferred_element_type=F32)
    qbf_ref[...] = (d(0, 512) * ATT_SCALE).astype(BF16)
    kv_ref[...] = d(512, 1024)
    kwv_ref[...] = d(1024, 1280)
    gl_ref[...] = d(1280, 1408)
    dup_ref[...] = d(1408, 2432).astype(BF16)


def nsa_proj(x_bf, w_bf):
    m, dm = x_bf.shape
    tm = min(m, 256)
    row = lambda width: pl.BlockSpec((tm, width), lambda i: (i, 0))
    return pl.pallas_call(
        _nsa_proj_kernel,
        grid=(m // tm,),
        in_specs=[row(dm), pl.BlockSpec((dm, NSA_PROJ_COLS), lambda i: (0, 0))],
        out_specs=[row(512), row(512), row(256), row(LANES), row(1024)],
        out_shape=[jax.ShapeDtypeStruct((m, 512), BF16), jax.ShapeDtypeStruct((m, 512), F32),
                   jax.ShapeDtypeStruct((m, 256), F32), jax.ShapeDtypeStruct((m, LANES), F32),
                   jax.ShapeDtypeStruct((m, 1024), BF16)],
        compiler_params=_cparams(("parallel",)),
        name="nsa_proj",
    )(x_bf, w_bf)


def nsa_compress_weight(cmp_w, cmp_b):
    w = cmp_w.reshape(2, 2, CMP_STRIDE, HEAD_DIM, HEAD_DIM)
    eye = jnp.eye(2, dtype=F32)
    t = jnp.einsum('whjde,wv,gk->jwgdhvke', w, eye, eye)
    t = jnp.broadcast_to(t[..., None, :], t.shape[:-1] + (2, HEAD_DIM))
    wmat = t.reshape(CMP_STRIDE * 256, 1024).astype(BF16)
    bias = jnp.broadcast_to(cmp_b[:, None, None, :], (2, NSA_KV_HEADS, 2, HEAD_DIM)).reshape(1, 512)
    return wmat, bias.astype(F32)


def _nsa_compress_kernel(x_ref, w_ref, b_ref, o_ref, acc_ref):
    j = pl.program_id(0)

    @pl.when(j == 0)
    def _():
        acc_ref[...] = jnp.zeros_like(acc_ref)

    acc_ref[...] += jnp.dot(x_ref[...].astype(BF16), w_ref[...], preferred_element_type=F32)

    @pl.when(j == pl.num_programs(0) - 1)
    def _():
        n = acc_ref.shape[0]
        first = acc_ref[:, 0:512]
        second = pltpu.roll(acc_ref[:, 512:1024], n - 1, axis=0)
        o_ref[...] = (first + second + b_ref[...]).astype(BF16)


def nsa_compress_rows(kv4, wmat, bias):
    t = kv4.shape[0]
    n = t // CMP_STRIDE
    x = kv4.reshape(n, CMP_STRIDE * 512)
    return pl.pallas_call(
        _nsa_compress_kernel,
        grid=(CMP_STRIDE,),
        in_specs=[pl.BlockSpec((n, 256), lambda j: (0, 2 * j)),
                  pl.BlockSpec((256, 1024), lambda j: (j, 0)),
                  pl.BlockSpec((1, 512), lambda j: (0, 0))],
        out_specs=pl.BlockSpec((n, 512), lambda j: (0, 0)),
        out_shape=jax.ShapeDtypeStruct((n, 512), BF16),
        scratch_shapes=[pltpu.VMEM((n, 1024), F32)],
        compiler_params=_cparams(("arbitrary",)),
        name="nsa_compress",
    )(x, wmat, bias)


def _nsa_cmp_kernel(tab_ref, q_ref, kv_ref, oc_ref, sel_ref, *, tq, ncp, n_sel):
    t0 = pl.program_id(0) * tq
    rows = lax.broadcasted_iota(jnp.int32, (tq, ncp), 0) + t0
    cols = lax.broadcasted_iota(jnp.int32, (tq, ncp), 1)
    dist = rows - (cols * CMP_STRIDE + (CMP_BLOCK - 1))
    valid = dist >= 0
    bucket = _t5_bucket_int(dist)
    left = lax.broadcasted_iota(jnp.int32, (1, LANES), 1) < HEAD_DIM
    ci = lax.broadcasted_iota(jnp.int32, (ncp, LANES), 0) * CMP_STRIDE
    cj = lax.broadcasted_iota(jnp.int32, (ncp, LANES), 1)
    cover = jnp.where((ci < (cj + 1) * SEL_BLOCK) & (ci + CMP_BLOCK > cj * SEL_BLOCK), 1.0, 0.0).astype(BF16)
    imp = [jnp.zeros((tq, LANES), F32) for _ in range(NSA_KV_HEADS)]
    for p in range(NSA_HEADS // 2):
        g = p // (NSA_REP // 2)
        qp = q_ref[:, p * LANES:(p + 1) * LANES]
        kc = kv_ref[:, g * LANES:(g + 1) * LANES]
        vc = kv_ref[:, (NSA_KV_HEADS + g) * LANES:(NSA_KV_HEADS + g + 1) * LANES]
        outs = []
        for hh in range(2):
            qm = jnp.where(left if hh == 0 else jnp.logical_not(left), qp, jnp.zeros_like(qp))
            s = lax.dot_general(qm, kc, (((1,), (1,)), ((), ())), preferred_element_type=F32)
            s = jnp.where(valid, s + _bias_lookup(bucket, tab_ref, 2 * p + hh), NEG)
            m = jnp.max(s, axis=-1, keepdims=True)
            e = jnp.where(valid, jnp.exp(s - m), 0.0)
            pr = (e / jnp.maximum(jnp.sum(e, axis=-1, keepdims=True), 1e-30)).astype(BF16)
            outs.append(jnp.dot(pr, vc, preferred_element_type=F32))
            imp[g] = imp[g] + jnp.dot(pr, cover, preferred_element_type=F32)
        oc_ref[:, p * LANES:(p + 1) * LANES] = jnp.where(left, outs[0], outs[1])

    lane = lax.broadcasted_iota(jnp.int32, (tq, LANES), 1)
    lane_f = lane.astype(F32)
    jq = (lax.broadcasted_iota(jnp.int32, (tq, LANES), 0) + t0) // SEL_BLOCK
    forced = (lane == 0) | (lane == jq) | (lane == jq - 1)
    allowed = lane <= jq
    for g in range(NSA_KV_HEADS):
        work = jnp.where(allowed, imp[g] + jnp.where(forced, SEL_FORCE, 0.0), NOT_ALLOWED)
        sel = jnp.zeros((tq, LANES), F32)
        for _ in range(n_sel):
            mx = jnp.max(work, axis=-1, keepdims=True)
            idx = jnp.min(jnp.where(work == mx, lane_f, float(LANES)), axis=-1, keepdims=True)
            hit = lane_f == idx
            sel = jnp.where(hit & (mx > 0.5 * NOT_ALLOWED), 1.0, sel)
            work = jnp.where(hit, REMOVED, work)
        sel_ref[:, g * LANES:(g + 1) * LANES] = sel.astype(BF16)


def nsa_cmp_select(qbf, kvc, rel_bias, t):
    tq = min(t, 256)
    ncp = kvc.shape[0]
    n_sel = min(N_SELECT, t // SEL_BLOCK)
    return pl.pallas_call(
        functools.partial(_nsa_cmp_kernel, tq=tq, ncp=ncp, n_sel=n_sel),
        grid=(t // tq,),
        in_specs=[pl.BlockSpec(memory_space=pltpu.SMEM),
                  pl.BlockSpec((tq, 512), lambda i: (i, 0)),
                  pl.BlockSpec((ncp, 512), lambda i: (0, 0))],
        out_specs=[pl.BlockSpec((tq, 512), lambda i: (i, 0)), pl.BlockSpec((tq, 256), lambda i: (i, 0))],
        out_shape=[jax.ShapeDtypeStruct((t, 512), F32), jax.ShapeDtypeStruct((t, 256), BF16)],
        compiler_params=_cparams(("parallel",)),
        name="nsa_cmp_select",
    )(rel_bias, qbf, kvc)


def _nsa_flash_kernel(qt_ref, kt_ref, ft_ref, q_ref, k_ref, v_ref, b_ref, *rest, tq, tk, selected):
    if selected:
        sm_ref, ex_ref, o_ref, m_ref, l_ref, acc_ref = rest
    else:
        o_ref, m_ref, l_ref, acc_ref = rest
    step = pl.program_id(1)
    qi = qt_ref[step]
    ki = kt_ref[step]

    @pl.when(ft_ref[step] == 1)
    def _():
        m_ref[...] = jnp.full_like(m_ref, NEG)
        l_ref[...] = jnp.zeros_like(l_ref)
        acc_ref[...] = jnp.zeros_like(acc_ref)

    left = lax.broadcasted_iota(jnp.int32, (1, LANES), 1) < HEAD_DIM

    def tile(diagonal):
        q = q_ref[...]
        k = k_ref[...]
        v = v_ref[...]
        rows = lax.broadcasted_iota(jnp.int32, (tq, tk), 0)
        cols = lax.broadcasted_iota(jnp.int32, (tq, tk), 1)
        if selected:
            valid = jnp.dot(sm_ref[...], ex_ref[...], preferred_element_type=F32) > 0.5
            if diagonal:
                valid = valid & (rows >= cols)
        else:
            valid = (rows >= cols) if diagonal else (rows < cols)
        pv, alphas = [], []
        for hh in range(2):
            qm = jnp.where(left if hh == 0 else jnp.logical_not(left), q, jnp.zeros_like(q))
            s = lax.dot_general(qm, k, (((1,), (1,)), ((), ())), preferred_element_type=F32)
            s = jnp.where(valid, s + b_ref[hh, 0], NEG)
            m_old = m_ref[hh]
            m_new = jnp.maximum(m_old, jnp.max(s, axis=-1, keepdims=True))
            p = jnp.where(valid, jnp.exp(s - m_new), 0.0)
            alpha = jnp.exp(m_old - m_new)
            l_ref[hh] = alpha * l_ref[hh] + jnp.sum(p, axis=-1, keepdims=True)
            m_ref[hh] = m_new
            pv.append(jnp.dot(p.astype(BF16), v, preferred_element_type=F32))
            alphas.append(alpha)
        acc_ref[...] = (jnp.where(left, alphas[0], alphas[1]) * acc_ref[...]
                        + jnp.where(left, pv[0], pv[1]))

    @pl.when(ki < qi)
    def _():
        tile(False)

    @pl.when(ki == qi)
    def _():
        tile(True)
        o_ref[...] = acc_ref[...] / jnp.where(left, l_ref[0], l_ref[1])


def nsa_flash(qbf, dup, btab, t, selmask=None):
    selected = selmask is not None
    tq = tk = btab.shape[-1]
    n = t // tq
    qs, ks, fs = [], [], []
    for qi in range(n):
        lo = 0 if selected else max(qi - (WINDOW // tk), 0)
        for ki in range(lo, qi + 1):
            qs.append(qi)
            ks.append(ki)
            fs.append(1 if ki == lo else 0)
    qt, kt, ft = (jnp.asarray(a, jnp.int32) for a in (qs, ks, fs))
    kcol = 0 if selected else 4
    half = NSA_REP // 2
    in_specs = [pl.BlockSpec((tq, LANES), lambda p, s, qt, kt, ft: (qt[s], p)),
                pl.BlockSpec((tk, LANES), lambda p, s, qt, kt, ft: (kt[s], kcol + p // half)),
                pl.BlockSpec((tk, LANES), lambda p, s, qt, kt, ft: (kt[s], kcol + 2 + p // half)),
                pl.BlockSpec((2, 1, tq, tk),
                             lambda p, s, qt, kt, ft: (p, jnp.minimum(qt[s] - kt[s], NSA_DELTAS - 1), 0, 0))]
    args = [qbf, dup, dup, btab]
    if selected:
        n_blk = tk // SEL_BLOCK
        jj = jnp.arange(LANES)[:, None]
        ll = jnp.arange(t)[None, :]
        expand = (jj == ll // SEL_BLOCK).astype(BF16)
        in_specs += [pl.BlockSpec((tq, LANES), lambda p, s, qt, kt, ft: (qt[s], p // half)),
                     pl.BlockSpec((LANES, tk), lambda p, s, qt, kt, ft: (0, kt[s]))]
        args += [selmask, expand]
    grid_spec = pltpu.PrefetchScalarGridSpec(
        num_scalar_prefetch=3,
        grid=(NSA_HEADS // 2, len(qs)),
        in_specs=in_specs,
        out_specs=pl.BlockSpec((tq, LANES), lambda p, s, qt, kt, ft: (qt[s], p)),
        scratch_shapes=[pltpu.VMEM((2, tq, 1), F32), pltpu.VMEM((2, tq, 1), F32), pltpu.VMEM((tq, LANES), F32)],
    )
    return pl.pallas_call(
        functools.partial(_nsa_flash_kernel, tq=tq, tk=tk, selected=selected),
        grid_spec=grid_spec,
        out_shape=jax.ShapeDtypeStruct((t, NSA_WIDTH), F32),
        compiler_params=_cparams(("parallel", "arbitrary")),
        name="nsa_sel_flash" if selected else "nsa_win_flash",
    )(qt, kt, ft, *args)


def nsa_prompt_attn(qbf, kv4, dup, wmat, cbias, rel_bias, btab, t):
    kvc = nsa_compress_rows(kv4[:t], wmat, cbias)
    o_c, selmask = nsa_cmp_select(qbf, kvc, rel_bias, t)
    o_s = nsa_flash(qbf, dup, btab, t, selmask)
    o_w = nsa_flash(qbf, dup, btab, t)
    return o_c, o_s, o_w


RW_CHUNK = 128
RW_PAIRS = RW_WIDTH // LANES


def _softplus(x):
    return jnp.maximum(x, 0.0) + jnp.log1p(jnp.exp(-jnp.abs(x)))


def _sigmoid(x):
    return 1.0 / (1.0 + jnp.exp(-x))


def _seg_sum_wide(x, ones_blk):
    return jnp.concatenate([_seg_sum(x[:, c * LANES:(c + 1) * LANES], ones_blk)
                            for c in range(x.shape[1] // LANES)], axis=1)


def _rwkv_features(p, shifted, mu, w0, w2p, a0, a2p, g2, k_k, k_a, ones_blk):
    z = p + (shifted - p) * mu
    r = z[:, 0:RW_WIDTH]
    k = z[:, RW_WIDTH:2 * RW_WIDTH]
    v = z[:, 2 * RW_WIDTH:3 * RW_WIDTH]
    lora = z[:, 3 * RW_WIDTH:3 * RW_WIDTH + LANES]
    gd = z[:, 3 * RW_WIDTH + LANES:]
    w_log = -_softplus(-(w0 + jnp.dot(jnp.tanh(lora).astype(BF16), w2p, preferred_element_type=F32))) - 0.5
    decay = jnp.exp(-jnp.exp(w_log))
    a = _sigmoid(a0 + jnp.dot(lora.astype(BF16), a2p, preferred_element_type=F32))
    g = jnp.dot(_sigmoid(gd).astype(BF16), g2, preferred_element_type=F32)
    kk = k * k_k
    kk = kk / jnp.maximum(jnp.sqrt(_seg_sum_wide(kk * kk, ones_blk)), 1e-12)
    kh = k * (1.0 + (a - 1.0) * k_a)
    return r, decay, kh, v, kk, kk * a, g


def _rwkv_finish(out, r, kh, v, g, r_k, gn_g, gn_b, ones_blk):
    m = _seg_sum_wide(out, ones_blk) * (1.0 / HEAD_DIM)
    d = out - m
    var = _seg_sum_wide(d * d, ones_blk) * (1.0 / HEAD_DIM)
    y = d * lax.rsqrt(var + RW_GN_EPS) * gn_g + gn_b
    bonus = _seg_sum_wide(r * kh * r_k, ones_blk) * v
    return (y + bonus) * g


def _rwkv_kernel(p_ref, mu_ref, w0_ref, w2_ref, a0_ref, a2_ref, g2_ref, kk_ref, ka_ref, rk_ref, gng_ref, gnb_ref,
                 spread_ref, y_ref, sout_ref, s_ref, prev_ref, rows_ref, vt_ref, ot_ref, vb_ref):
    n = RW_CHUNK
    c = pl.program_id(0)

    @pl.when(c == 0)
    def _():
        s_ref[...] = jnp.zeros_like(s_ref)
        prev_ref[...] = jnp.zeros_like(prev_ref)

    p = p_ref[...]
    first = lax.broadcasted_iota(jnp.int32, (n, 1), 0) == 0
    shifted = jnp.where(first, prev_ref[0:1, :], pltpu.roll(p, 1, axis=0))
    prev_ref[...] = jnp.broadcast_to(p[n - 1:n, :], prev_ref.shape)
    ones_blk = _head_ones()
    r, decay, kh, v, kk, kka, g = _rwkv_features(p, shifted, mu_ref[...], w0_ref[...], w2_ref[...], a0_ref[...],
                                                 a2_ref[...], g2_ref[...], kk_ref[...], ka_ref[...], ones_blk)
    for qi, arr in enumerate((r, decay, kh, kk, kka)):
        rows_ref[qi] = arr
    for pr in range(RW_PAIRS):
        vt_ref[pr] = v[:, pr * LANES:(pr + 1) * LANES].T
    ot_ref[...] = jnp.zeros_like(ot_ref)
    lane_t = lax.broadcasted_iota(jnp.int32, (HEAD_DIM, n), 1)
    left = lax.broadcasted_iota(jnp.int32, (1, LANES), 1) < HEAD_DIM
    half_sum = (lax.broadcasted_iota(jnp.int32, (LANES, 2 * LANES), 0) // HEAD_DIM
                == lax.broadcasted_iota(jnp.int32, (LANES, 2 * LANES), 1) // LANES).astype(BF16)

    def group(j, carry):
        base = pl.multiple_of(j * 8, 8)
        for pr in range(RW_PAIRS):
            vtr = pltpu.roll(vt_ref[pr], lax.rem(n - base, n), axis=1)
            pieces = _split3(vtr[0:HEAD_DIM]) + _split3(vtr[HEAD_DIM:])
            vb_ref[pr] = jnp.dot(jnp.concatenate(pieces, axis=1), spread_ref[...], preferred_element_type=F32)
        blk = [[rows_ref[q, pl.ds(base, 8), pr * LANES:(pr + 1) * LANES] for q in range(5)]
               for pr in range(RW_PAIRS)]
        for i in range(8):
            hit = lane_t == base + i
            for pr in range(RW_PAIRS):
                rv = lambda q: blk[pr][q][i:i + 1, :]
                s = s_ref[pr]
                m = s * rv(3)
                sa0 = jnp.sum(jnp.where(left, m, 0.0), axis=1, keepdims=True)
                sa1 = jnp.sum(jnp.where(left, 0.0, m), axis=1, keepdims=True)
                s = s * rv(1) - jnp.where(left, sa0, sa1) * rv(4) + vb_ref[pr, :, i * LANES:(i + 1) * LANES] * rv(2)
                s_ref[pr] = s
                oc = jnp.dot((s * rv(0)).astype(BF16), half_sum, preferred_element_type=F32)
                ot_ref[pr, 0:HEAD_DIM, :] = jnp.where(hit, oc[:, 0:LANES], ot_ref[pr, 0:HEAD_DIM, :])
                ot_ref[pr, HEAD_DIM:, :] = jnp.where(hit, oc[:, LANES:], ot_ref[pr, HEAD_DIM:, :])
        return carry

    lax.fori_loop(0, n // 8, group, 0)
    out = jnp.concatenate([ot_ref[pr].T for pr in range(RW_PAIRS)], axis=1)
    y_ref[...] = _rwkv_finish(out, r, kh, v, g, rk_ref[...], gng_ref[...], gnb_ref[...], ones_blk)

    @pl.when(c == pl.num_programs(0) - 1)
    def _():
        for pr in range(RW_PAIRS):
            sout_ref[2 * pr] = s_ref[pr][:, 0:HEAD_DIM]
            sout_ref[2 * pr + 1] = s_ref[pr][:, HEAD_DIM:]


def _rwkv_params(mu, w0, w2, a0, a2, g2, k_k, k_a, r_k, gn_g, gn_b):
    row = lambda a: a.reshape(1, -1).astype(F32)
    zeros = jnp.zeros((DECAY_LORA, RW_WIDTH), F32)
    w2p = jnp.concatenate([w2, zeros], axis=0).astype(BF16)
    a2p = jnp.concatenate([zeros, a2], axis=0).astype(BF16)
    return (row(mu), row(w0), w2p, row(a0), a2p, g2.astype(BF16), row(k_k), row(k_a), row(r_k), row(gn_g), row(gn_b))


def rwkv_prompt(p_rw, t, params):
    n = RW_CHUNK
    rr = jnp.arange(2 * 3 * LANES)[:, None]
    cc = jnp.arange(8 * LANES)[None, :]
    spread = ((rr % LANES == cc // LANES) & (rr // (3 * LANES) == (cc % LANES) // HEAD_DIM)).astype(BF16)
    args = _rwkv_params(*params) + (spread,)
    full = lambda a: pl.BlockSpec(a.shape, lambda i: (0,) * a.ndim)
    return pl.pallas_call(
        _rwkv_kernel,
        grid=(t // n,),
        in_specs=[pl.BlockSpec((n, RW_COLS), lambda i: (i, 0))] + [full(a) for a in args],
        out_specs=[pl.BlockSpec((n, RW_WIDTH), lambda i: (i, 0)),
                   pl.BlockSpec((RW_HEADS, HEAD_DIM, HEAD_DIM), lambda i: (0, 0, 0))],
        out_shape=[jax.ShapeDtypeStruct((t, RW_WIDTH), F32),
                   jax.ShapeDtypeStruct((RW_HEADS, HEAD_DIM, HEAD_DIM), F32)],
        scratch_shapes=[pltpu.VMEM((RW_PAIRS, HEAD_DIM, LANES), F32),
                        pltpu.VMEM((8, RW_COLS), F32),
                        pltpu.VMEM((5, n, RW_WIDTH), F32),
                        pltpu.VMEM((RW_PAIRS, LANES, n), F32),
                        pltpu.VMEM((RW_PAIRS, LANES, n), F32),
                        pltpu.VMEM((RW_PAIRS, HEAD_DIM, 8 * LANES), F32)],
        compiler_params=_cparams(("arbitrary",)),
        name="rwkv_prompt",
    )(p_rw, *args)


def _rwkv_sample_kernel(p_ref, sh_ref, s_in_ref, mu_ref, w0_ref, w2_ref, a0_ref, a2_ref, g2_ref, kk_ref, ka_ref,
                        rk_ref, gng_ref, gnb_ref, y_ref, s_out_ref, rows_ref, vt_ref, ot_ref):
    n = p_ref.shape[0]
    ones_blk = _head_ones()
    r, decay, kh, v, kk, kka, g = _rwkv_features(p_ref[...], sh_ref[...], mu_ref[...], w0_ref[...], w2_ref[...],
                                                 a0_ref[...], a2_ref[...], g2_ref[...], kk_ref[...], ka_ref[...],
                                                 ones_blk)
    for qi, arr in enumerate((r, decay, kh, kk, kka)):
        rows_ref[qi, 0] = arr
        rows_ref[qi, 1] = pltpu.roll(arr, RW_WIDTH - HEAD_DIM, axis=1)
    vt_ref[...] = jnp.zeros_like(vt_ref)
    ot_ref[...] = jnp.zeros_like(ot_ref)
    vpad = jnp.concatenate([v, jnp.zeros((LANES - n, RW_WIDTH), F32)], axis=0) if n < LANES else v
    for pr in range(RW_PAIRS):
        vt_ref[pr] = vpad[:, pr * LANES:(pr + 1) * LANES].T
    lane_t = lax.broadcasted_iota(jnp.int32, (HEAD_DIM, LANES), 1)

    def group(j, carry):
        base = pl.multiple_of(j * 8, 8)
        for h in range(RW_HEADS):
            pr, par = h // 2, h % 2
            lo, ro = pr * LANES, par * HEAD_DIM
            blk = [rows_ref[q, par, pl.ds(base, 8), lo:lo + HEAD_DIM] for q in range(5)]
            ot = ot_ref[pr, ro:ro + HEAD_DIM, :]
            vt = vt_ref[pr, ro:ro + HEAD_DIM, :]
            for i in range(8):
                hit = lane_t == base + i
                rv = lambda q: blk[q][i:i + 1, :]
                s = s_in_ref[base + i, h]
                sa = -jnp.sum(s * rv(3), axis=1, keepdims=True)
                vcol = jnp.sum(jnp.where(hit, vt, 0.0), axis=1, keepdims=True)
                s = s * rv(1) + sa * rv(4) + vcol * rv(2)
                s_out_ref[base + i, h] = s
                ocol = jnp.sum(s * rv(0), axis=1, keepdims=True)
                ot = jnp.where(hit, ocol, ot)
            ot_ref[pr, ro:ro + HEAD_DIM, :] = ot
        return carry

    lax.fori_loop(0, n // 8, group, 0)
    out = jnp.concatenate([ot_ref[pr].T for pr in range(RW_PAIRS)], axis=1)[0:n]
    y_ref[...] = _rwkv_finish(out, r, kh, v, g, rk_ref[...], gng_ref[...], gnb_ref[...], ones_blk)


def rwkv_sample(p_rw, shift, state, params):
    b = p_rw.shape[0]
    n = min(b, 32)
    args = _rwkv_params(*params)
    full = lambda a: pl.BlockSpec(a.shape, lambda i: (0,) * a.ndim)
    st = pl.BlockSpec((n, RW_HEADS, HEAD_DIM, HEAD_DIM), lambda i: (i, 0, 0, 0))
    return pl.pallas_call(
        _rwkv_sample_kernel,
        grid=(b // n,),
        in_specs=[pl.BlockSpec((n, RW_COLS), lambda i: (i, 0)), pl.BlockSpec((n, RW_COLS), lambda i: (i, 0)), st]
        + [full(a) for a in args],
        out_specs=[pl.BlockSpec((n, RW_WIDTH), lambda i: (i, 0)), st],
        out_shape=[jax.ShapeDtypeStruct((b, RW_WIDTH), F32), jax.ShapeDtypeStruct(state.shape, F32)],
        scratch_shapes=[pltpu.VMEM((5, 2, n, RW_WIDTH), F32),
                        pltpu.VMEM((RW_PAIRS, LANES, LANES), F32),
                        pltpu.VMEM((RW_PAIRS, LANES, LANES), F32)],
        compiler_params=_cparams(("parallel",)),
        name="rwkv_sample",
    )(p_rw, shift, state, *args)


def _t5_bucket_host(dist):
    d = jnp.maximum(dist, 0)
    large = N_BUCKETS // 2 + sum((d >= th).astype(jnp.int32) for th in T5_THRESH)
    return jnp.where(d < N_BUCKETS // 2, d, large)


FOX_PAGES_PER_STEP = 4


def _fox_paged_kernel(pt_ref, q_ref, kn_ref, vn_ref, lfn_ref, pick_ref, spread_ref, *refs, pps, page):
    kp, vp, lp = refs[0:pps], refs[pps:2 * pps], refs[2 * pps:3 * pps]
    o_ref, m_ref, l_ref, acc_ref, after_ref = refs[3 * pps:]
    step = pl.program_id(1)
    nt = (((1,), (1,)), ((), ()))
    rows = page * FOX_HEADS
    q = q_ref[0]

    @pl.when(step == 0)
    def _():
        l_new = jnp.sum(q.astype(F32) * kn_ref[0].astype(BF16).astype(F32), axis=1, keepdims=True)
        m_ref[...] = l_new
        l_ref[...] = jnp.ones_like(l_ref)
        acc_ref[...] = vn_ref[0].astype(BF16).astype(F32)
        own = (lax.broadcasted_iota(jnp.int32, (FOX_HEADS, LANES), 0)
               == lax.broadcasted_iota(jnp.int32, (FOX_HEADS, LANES), 1))
        after_ref[...] = jnp.sum(jnp.where(own, lfn_ref[0], 0.0), axis=1, keepdims=True)

    diag = (lax.broadcasted_iota(jnp.int32, (FOX_HEADS, rows), 1) % FOX_HEADS
            == lax.broadcasted_iota(jnp.int32, (FOX_HEADS, rows), 0))
    later = (lax.broadcasted_iota(jnp.int32, (page, page), 0)
             > lax.broadcasted_iota(jnp.int32, (page, page), 1)).astype(BF16)
    qk = [jnp.where(diag, lax.dot_general(q, kp[i][0, 0].reshape(rows, HEAD_DIM).astype(BF16), nt,
                                          preferred_element_type=F32), 0.0) for i in range(pps)]
    hi, mid, _ = _split3(jnp.concatenate(qk, axis=0))
    picked = (jnp.dot(hi, pick_ref[...], preferred_element_type=F32)
              + jnp.dot(mid, pick_ref[...], preferred_element_type=F32))
    after = after_ref[...]
    logits = []
    for i in range(pps):
        lf = jnp.concatenate([lp[i][0, 0], jnp.zeros((page, LANES - FOX_HEADS), F32)], axis=1)
        lft = lf.T[0:FOX_HEADS]
        within = sum(jnp.dot(piece, later, preferred_element_type=F32) for piece in _split3(lft))
        logits.append(picked[i * FOX_HEADS:(i + 1) * FOX_HEADS] + within + after)
        after = after + jnp.sum(lft, axis=1, keepdims=True)
    after_ref[...] = after
    m_old = m_ref[...]
    m_new = m_old
    for lg in logits:
        m_new = jnp.maximum(m_new, jnp.max(lg, axis=1, keepdims=True))
    alpha = jnp.exp(m_old - m_new)
    ps = [jnp.exp(lg - m_new) for lg in logits]
    l_ref[...] = alpha * l_ref[...] + sum(jnp.sum(p, axis=1, keepdims=True) for p in ps)
    m_ref[...] = m_new
    wide = jnp.dot(jnp.concatenate(ps, axis=0).astype(BF16), spread_ref[...], preferred_element_type=F32)
    acc = alpha * acc_ref[...]
    for i in range(pps):
        pw = jnp.where(diag, wide[i * FOX_HEADS:(i + 1) * FOX_HEADS], 0.0).astype(BF16)
        acc = acc + jnp.dot(pw, vp[i][0, 0].reshape(rows, HEAD_DIM).astype(BF16), preferred_element_type=F32)
    acc_ref[...] = acc

    @pl.when(step == pl.num_programs(1) - 1)
    def _():
        o_ref[0] = acc / l_ref[...]


def fox_paged(pt, layer, qbf, k_new, v_new, lf_new, cache_k, cache_v, cache_lf):
    b, n_pages = pt.shape
    page = cache_k.shape[2]
    pps = FOX_PAGES_PER_STEP
    rows = page * FOX_HEADS
    heads3 = lambda a: a.reshape(b, FOX_HEADS, HEAD_DIM)
    pick = (jnp.arange(rows)[:, None] // FOX_HEADS == jnp.arange(page)[None, :]).astype(BF16)
    per_seq = lambda *shape: pl.BlockSpec((1,) + shape, lambda i, s, pt: (i,) + (0,) * len(shape))
    full = lambda a: pl.BlockSpec(a.shape, lambda i, s, pt: (0,) * a.ndim)

    def paged(tail, j):
        return pl.BlockSpec((1, 1, page) + tail,
                            lambda i, s, pt, j=j: (layer, pt[i, n_pages - 1 - (s * pps + j)], 0) + (0,) * len(tail))

    in_specs = ([per_seq(FOX_HEADS, HEAD_DIM)] * 3 + [per_seq(1, LANES), full(pick), full(pick.T)]
                + [paged((FOX_HEADS, HEAD_DIM), j) for j in range(pps)]
                + [paged((FOX_HEADS, HEAD_DIM), j) for j in range(pps)]
                + [paged((FOX_HEADS,), j) for j in range(pps)])
    grid_spec = pltpu.PrefetchScalarGridSpec(
        num_scalar_prefetch=1, grid=(b, n_pages // pps), in_specs=in_specs,
        out_specs=per_seq(FOX_HEADS, HEAD_DIM),
        scratch_shapes=[pltpu.VMEM((FOX_HEADS, 1), F32), pltpu.VMEM((FOX_HEADS, 1), F32),
                        pltpu.VMEM((FOX_HEADS, HEAD_DIM), F32), pltpu.VMEM((FOX_HEADS, 1), F32)])
    out = pl.pallas_call(
        functools.partial(_fox_paged_kernel, pps=pps, page=page),
        grid_spec=grid_spec,
        out_shape=jax.ShapeDtypeStruct((b, FOX_HEADS, HEAD_DIM), F32),
        compiler_params=_cparams(("parallel", "arbitrary")),
        name="fox_paged",
    )(pt, heads3(qbf), heads3(k_new), heads3(v_new), lf_new.reshape(b, 1, LANES), pick, pick.T,
      *([cache_k] * pps), *([cache_v] * pps), *([cache_lf] * pps))
    return out.reshape(b, FOX_WIDTH)


def _masked_softmax_rows(parts, valids):
    m = jnp.full((parts[0].shape[0], 1), NEG, F32)
    for lg, ok in zip(parts, valids):
        m = jnp.maximum(m, jnp.max(jnp.where(ok, lg, NEG), axis=1, keepdims=True))
    es = [jnp.where(ok, jnp.exp(jnp.where(ok, lg, NEG) - m), 0.0) for lg, ok in zip(parts, valids)]
    den = sum(jnp.sum(e, axis=1, keepdims=True) for e in es)
    return es, jnp.maximum(den, 1e-30)


def _wg_mask(n_rows, n_w, w_pick):
    col = lax.broadcasted_iota(jnp.int32, (NSA_HEADS, n_rows), 1)
    head = lax.broadcasted_iota(jnp.int32, (NSA_HEADS, n_rows), 0)
    return ((col // NSA_KV_HEADS) % n_w == w_pick) & (col % NSA_KV_HEADS == head // NSA_REP)


def _pick_pair(n_pos, per_pos):
    pick = (jnp.arange(n_pos * per_pos)[:, None] // per_pos == jnp.arange(n_pos)[None, :]).astype(BF16)
    return pick, pick.T


def _nsa_paged_kernel(pt_ref, q_ref, new_ref, win_ref, wk1_ref, wk2_ref, wv1_ref, wv2_ref, cb_ref, bc_ref, bs_ref,
                      bw_ref, b0_ref, ex_ref, pick8_ref, spread8_ref, pick2_ref, spread2_ref, pickw_ref,
                      spreadw_ref, *refs, n_pages, page):
    pages = refs[0:n_pages]
    o_ref, wout_ref = refs[n_pages], refs[n_pages + 1]
    nt = (((1,), (1,)), ((), ()))
    q = q_ref[0]
    b0 = b0_ref[:, 0:1]
    past = n_pages * page
    ncr = past // CMP_STRIDE
    cpp = page // CMP_STRIDE
    g2 = NSA_KV_HEADS

    def scores(rows_bf, mask, pick):
        res = jnp.where(mask, lax.dot_general(q, rows_bf, nt, preferred_element_type=F32), 0.0)
        hi, mid, _ = _split3(res)
        return jnp.dot(hi, pick, preferred_element_type=F32) + jnp.dot(mid, pick, preferred_element_type=F32)

    def weighted(p, rows_bf, mask, spread):
        wide = jnp.where(mask, jnp.dot(p.astype(BF16), spread, preferred_element_type=F32), 0.0)
        return jnp.dot(wide.astype(BF16), rows_bf, preferred_element_type=F32)

    fk = jnp.zeros((ncr * g2, HEAD_DIM), F32)
    sk, fv, sv = fk, fk, fk
    for j in range(CMP_STRIDE):
        xs = [pages[pg][0, 0].reshape(cpp, CMP_STRIDE, 4, g2, HEAD_DIM)[:, j] for pg in range(n_pages)]
        xk = jnp.concatenate([x[:, 0].reshape(cpp * g2, HEAD_DIM) for x in xs], axis=0).astype(BF16)
        xv = jnp.concatenate([x[:, 1].reshape(cpp * g2, HEAD_DIM) for x in xs], axis=0).astype(BF16)
        fk = fk + jnp.dot(xk, wk1_ref[j], preferred_element_type=F32)
        sk = sk + jnp.dot(xk, wk2_ref[j], preferred_element_type=F32)
        fv = fv + jnp.dot(xv, wv1_ref[j], preferred_element_type=F32)
        sv = sv + jnp.dot(xv, wv2_ref[j], preferred_element_type=F32)
    kc = (fk + pltpu.roll(sk, (ncr - 1) * g2, axis=0) + cb_ref[0:1, :]).astype(BF16)
    vc = (fv + pltpu.roll(sv, (ncr - 1) * g2, axis=0) + cb_ref[1:2, :]).astype(BF16)

    m2 = _wg_mask(ncr * g2, 1, 0)
    lane_c = lax.broadcasted_iota(jnp.int32, (NSA_HEADS, ncr), 1)
    lc = scores(kc, m2, pick2_ref[...]) + bc_ref[...]
    (e_c,), den_c = _masked_softmax_rows([lc], [lane_c < ncr - 1])
    p_c = (e_c / den_c).astype(BF16)
    o_ref[0, 0] = weighted(p_c, vc, m2, spread2_ref[...])

    ci = lax.broadcasted_iota(jnp.int32, (ncr, LANES), 0) * CMP_STRIDE
    cj = lax.broadcasted_iota(jnp.int32, (ncr, LANES), 1)
    cover = jnp.where((ci < (cj + 1) * SEL_BLOCK) & (ci + CMP_BLOCK > cj * SEL_BLOCK), 1.0, 0.0).astype(BF16)
    imp8 = jnp.dot(p_c, cover, preferred_element_type=F32)
    row8 = lax.broadcasted_iota(jnp.int32, (NSA_HEADS, LANES), 0)
    imp = jnp.where(row8 < NSA_REP, jnp.sum(imp8[0:NSA_REP], axis=0, keepdims=True),
                    jnp.sum(imp8[NSA_REP:], axis=0, keepdims=True))
    lane = lax.broadcasted_iota(jnp.int32, (NSA_HEADS, LANES), 1)
    lane_f = lane.astype(F32)
    jq = past // SEL_BLOCK
    forced = (lane == 0) | (lane == jq) | (lane == jq - 1)
    work = jnp.where(lane <= jq, imp + jnp.where(forced, SEL_FORCE, 0.0), NOT_ALLOWED)
    sel = jnp.zeros((NSA_HEADS, LANES), F32)
    for _ in range(min(N_SELECT, jq + 1)):
        mx = jnp.max(work, axis=-1, keepdims=True)
        idx = jnp.min(jnp.where(work == mx, lane_f, float(LANES)), axis=-1, keepdims=True)
        hit = lane_f == idx
        sel = jnp.where(hit & (mx > 0.5 * NOT_ALLOWED), 1.0, sel)
        work = jnp.where(hit, REMOVED, work)
    sel_bf = sel.astype(BF16)

    x2 = [pages[pg][0, 0].reshape(page * 4 * g2, HEAD_DIM).astype(BF16) for pg in range(n_pages)]
    mk, mv = _wg_mask(page * 4 * g2, 4, 2), _wg_mask(page * 4 * g2, 4, 3)
    res = jnp.concatenate([jnp.where(mk, lax.dot_general(q, x, nt, preferred_element_type=F32), 0.0) for x in x2],
                          axis=0)
    hi, mid, _ = _split3(res)
    gathered = (jnp.dot(hi, pick8_ref[...], preferred_element_type=F32)
                + jnp.dot(mid, pick8_ref[...], preferred_element_type=F32))
    parts, oks = [], []
    for pg in range(n_pages):
        parts.append(gathered[pg * NSA_HEADS:(pg + 1) * NSA_HEADS] + bs_ref[:, pg * page:(pg + 1) * page])
        oks.append(jnp.dot(sel_bf, ex_ref[:, pg * page:(pg + 1) * page], preferred_element_type=F32) > 0.5)
    new_rows = new_ref[0].reshape(8 * g2, HEAD_DIM).astype(BF16)
    mnk, mnv = _wg_mask(8 * g2, 8, 2), _wg_mask(8 * g2, 8, 3)
    parts.append(jnp.sum(jnp.where(mnk, lax.dot_general(q, new_rows, nt, preferred_element_type=F32), 0.0),
                         axis=1, keepdims=True) + b0)
    oks.append(sel[:, jq:jq + 1] > 0.5)
    es, den = _masked_softmax_rows(parts, oks)
    o_s = jnp.dot(jnp.where(mnv, es[n_pages], 0.0).astype(BF16), new_rows, preferred_element_type=F32)
    wide = jnp.dot(jnp.concatenate(es[0:n_pages], axis=0).astype(BF16), spread8_ref[...],
                   preferred_element_type=F32)
    for pg in range(n_pages):
        pw = jnp.where(mv, wide[pg * NSA_HEADS:(pg + 1) * NSA_HEADS], 0.0).astype(BF16)
        o_s = o_s + jnp.dot(pw, x2[pg], preferred_element_type=F32)
    o_ref[0, 1] = o_s / den

    wb = win_ref.shape[1]
    wrows = win_ref[0].reshape(wb * 2 * g2, HEAD_DIM).astype(BF16)
    mwk, mwv = _wg_mask(wb * 2 * g2, 2, 0), _wg_mask(wb * 2 * g2, 2, 1)
    mnk, mnv = _wg_mask(8 * g2, 8, 4), _wg_mask(8 * g2, 8, 5)
    lane_w = lax.broadcasted_iota(jnp.int32, (NSA_HEADS, wb), 1)
    parts = [scores(wrows, mwk, pickw_ref[...]) + bw_ref[...],
             jnp.sum(jnp.where(mnk, lax.dot_general(q, new_rows, nt, preferred_element_type=F32), 0.0),
                     axis=1, keepdims=True) + b0]
    oks = [wb - lane_w < WINDOW, jnp.full((NSA_HEADS, 1), True)]
    es, den = _masked_softmax_rows(parts, oks)
    o_w = (weighted(es[0], wrows, mwv, spreadw_ref[...])
           + jnp.dot(jnp.where(mnv, es[1], 0.0).astype(BF16), new_rows, preferred_element_type=F32))
    o_ref[0, 2] = o_w / den

    wout_ref[0, 0:wb - 1] = win_ref[0, 1:wb]
    wout_ref[0, wb - 1] = new_ref[0, 4:6]


def nsa_paged(pt, layer, q8, new_rows, win, cache, cmp_w, cmp_b, rel_bias):
    b, n_pages = pt.shape
    page = cache.shape[2]
    past = n_pages * page
    wb = win.shape[1]
    ncr = past // CMP_STRIDE
    tab = lambda dist: rel_bias[_t5_bucket_host(dist)].T.astype(F32)
    bc = tab(past - (jnp.arange(ncr) * CMP_STRIDE + CMP_BLOCK - 1))
    bs = tab(past - jnp.arange(past))
    bw = tab(wb - jnp.arange(wb))
    b0 = jnp.broadcast_to(rel_bias[0][:, None], (NSA_HEADS, LANES)).astype(F32)
    expand = (jnp.arange(LANES)[:, None] == jnp.arange(past)[None, :] // SEL_BLOCK).astype(BF16)
    w = cmp_w.astype(BF16)
    consts = (w[0, :CMP_STRIDE], w[0, CMP_STRIDE:], w[1, :CMP_STRIDE], w[1, CMP_STRIDE:], cmp_b.astype(F32),
              bc, bs, bw, b0, expand) + _pick_pair(page, 4 * NSA_KV_HEADS) + _pick_pair(ncr, NSA_KV_HEADS) \
        + _pick_pair(wb, 2 * NSA_KV_HEADS)
    per_seq = lambda *shape: pl.BlockSpec((1,) + shape, lambda i, pt: (i,) + (0,) * len(shape))
    full = lambda a: pl.BlockSpec(a.shape, lambda i, pt: (0,) * a.ndim)
    tail = cache.shape[3:]
    in_specs = ([per_seq(NSA_HEADS, HEAD_DIM), per_seq(8, NSA_KV_HEADS, HEAD_DIM), per_seq(*win.shape[1:])]
                + [full(a) for a in consts]
                + [pl.BlockSpec((1, 1, page) + tail, lambda i, pt, j=j: (layer, pt[i, j], 0, 0, 0, 0))
                   for j in range(n_pages)])
    grid_spec = pltpu.PrefetchScalarGridSpec(
        num_scalar_prefetch=1, grid=(b,), in_specs=in_specs,
        out_specs=[per_seq(3, NSA_HEADS, HEAD_DIM), per_seq(*win.shape[1:])])
    return pl.pallas_call(
        functools.partial(_nsa_paged_kernel, n_pages=n_pages, page=page),
        grid_spec=grid_spec,
        out_shape=[jax.ShapeDtypeStruct((b, 3, NSA_HEADS, HEAD_DIM), F32), jax.ShapeDtypeStruct(win.shape, F32)],
        compiler_params=_cparams(("parallel",)),
        name="nsa_paged",
    )(pt, q8, new_rows, win, *consts, *([cache] * n_pages))


def kernel(x_prompt, x_sample, cache_nsa_kv, cache_fox_k, cache_fox_v, cache_fox_logf, state_nsa_win, state_rwkv_wkv, state_rwkv_shift, page_table, ln_g, ln_b, ffn_w_gate, ffn_w_up, ffn_w_down, w_in, w_out, rw_mu, rw_w0, rw_w2, rw_a0, rw_a2, rw_g2, rw_k_k, rw_k_a, rw_r_k, rw_gn_g, rw_gn_b, nsa_cmp_w, nsa_cmp_b, nsa_out_g, rel_bias, fox_b_f, fox_out_g):
    db = x_sample.shape[0]
    t = x_prompt.shape[1]

    wg_bf = ffn_w_gate.astype(BF16)
    wu_bf = ffn_w_up.astype(BF16)
    wd_bf = ffn_w_down.astype(BF16)
    wout_bf = w_out.astype(BF16)
    w_rw = w_in[:, :, :RW_COLS].astype(BF16)
    w_nsa = [nsa_proj_weight(w_in[l, :, RW_COLS:RW_COLS + NSA_COLS]) for l in range(DEPTH)]
    w_fox = jnp.pad(w_in[:, :, RW_COLS + NSA_COLS:].astype(BF16), ((0, 0), (0, 0), (0, 3200 - FOX_COLS)))

    def ffn_sub(x, x_bf, l, j):
        h = ffn_in(x_bf, wg_bf[l, j], wu_bf[l, j])
        return ffn_out(h, wd_bf[l, j], x, ln_g[l, 2 * j], ln_b[l, 2 * j])

    def rw_params(l):
        return (rw_mu[l], rw_w0[l], rw_w2[l], rw_a0[l], rw_a2[l], rw_g2[l], rw_k_k[l], rw_k_a[l],
                rw_r_k[l], rw_gn_g[l], rw_gn_b[l])

    xp = x_prompt[0]
    xp_bf = xp.astype(BF16)
    p_kv, p_fk, p_fv, p_fl, p_win, p_wkv, p_shift = [], [], [], [], [], [], []
    btab = bias_tiles(rel_bias, NSA_TILE)
    wb = min(WINDOW, t)
    for l in range(DEPTH):
        xp, xp_bf = ffn_sub(xp, xp_bf, l, 0)
        p_rw = proj(xp_bf, w_rw[l])
        nq, kv4, kwv, gl, dup = nsa_proj(xp_bf, w_nsa[l])
        qbf, fk, fv, kbf, vbf, lf128 = fox_proj(xp_bf, w_fox[l], fox_b_f[l])
        o_rw, s_wkv = rwkv_prompt(p_rw, t, rw_params(l))
        wmat, cbias = nsa_compress_weight(nsa_cmp_w[l], nsa_cmp_b[l])
        o_c, o_s, o_w = nsa_prompt_attn(nq, kv4, dup, wmat, cbias, rel_bias, btab, t)
        o_fox = fox_prompt_attn(qbf, kbf, vbf, lf128, t)
        xp, xp_bf = mix_out(o_rw, o_c, o_s, o_w, gl, o_fox, xp, wout_bf[l], nsa_out_g[l], fox_out_g[l],
                            ln_g[l, 1], ln_b[l, 1])
        xp, xp_bf = ffn_sub(xp, xp_bf, l, 1)
        p_kv.append(kv4.reshape(1, t, 4, NSA_KV_HEADS, HEAD_DIM))
        p_fk.append(fk.reshape(1, t, FOX_HEADS, HEAD_DIM))
        p_fv.append(fv.reshape(1, t, FOX_HEADS, HEAD_DIM))
        p_fl.append(lf128[None, :, :FOX_HEADS])
        p_win.append(kwv[t - wb:].reshape(1, wb, 2, NSA_KV_HEADS, HEAD_DIM))
        p_wkv.append(s_wkv[None])
        p_shift.append(p_rw[t - 1:t])

    xs = x_sample[:, 0]
    xs_bf = xs.astype(BF16)
    s_kv, s_fk, s_fv, s_fl, s_win, s_wkv_l, s_shift_l = [], [], [], [], [], [], []
    for l in range(DEPTH):
        xs, xs_bf = ffn_sub(xs, xs_bf, l, 0)
        p_rw = proj(xs_bf, w_rw[l])
        nq, kv_new, kwv_new, gl, _ = nsa_proj(xs_bf, w_nsa[l])
        qbf, fk, fv, kbf, vbf, lf128 = fox_proj(xs_bf, w_fox[l], fox_b_f[l])
        o_rw, s_wkv = rwkv_sample(p_rw, state_rwkv_shift[l], state_rwkv_wkv[l], rw_params(l))
        new_rows = jnp.concatenate([kv_new.reshape(db, 4, NSA_KV_HEADS, HEAD_DIM),
                                    kwv_new.reshape(db, 2, NSA_KV_HEADS, HEAD_DIM),
                                    jnp.zeros((db, 2, NSA_KV_HEADS, HEAD_DIM), F32)], axis=1)
        o3, win = nsa_paged(page_table, l, nq.reshape(db, NSA_HEADS, HEAD_DIM), new_rows, state_nsa_win[l],
                            cache_nsa_kv, nsa_cmp_w[l], nsa_cmp_b[l], rel_bias)
        o_c, o_s, o_w = (o3[:, br].reshape(db, NSA_WIDTH) for br in range(3))
        o_fox = fox_paged(page_table, l, qbf, fk, fv, lf128, cache_fox_k, cache_fox_v, cache_fox_logf)
        xs, xs_bf = mix_out(o_rw, o_c, o_s, o_w, gl, o_fox, xs, wout_bf[l],
                            nsa_out_g[l], fox_out_g[l], ln_g[l, 1], ln_b[l, 1])
        xs, xs_bf = ffn_sub(xs, xs_bf, l, 1)
        s_kv.append(kv_new.reshape(db, 1, 4, NSA_KV_HEADS, HEAD_DIM))
        s_fk.append(fk.reshape(db, 1, FOX_HEADS, HEAD_DIM))
        s_fv.append(fv.reshape(db, 1, FOX_HEADS, HEAD_DIM))
        s_fl.append(lf128[:, None, :FOX_HEADS])
        s_win.append(win)
        s_wkv_l.append(s_wkv)
        s_shift_l.append(p_rw)

    return (xp[None], xs[:, None],
            jnp.stack(p_kv), jnp.stack(p_fk), jnp.stack(p_fv), jnp.stack(p_fl),
            jnp.stack(p_win), jnp.stack(p_wkv), jnp.stack(p_shift),
            jnp.stack(s_kv), jnp.stack(s_fk), jnp.stack(s_fv), jnp.stack(s_fl),
            jnp.stack(s_win), jnp.stack(s_wkv_l), jnp.stack(s_shift_l))
```

```python
import functools
import math

import jax
import jax.numpy as jnp
from jax import lax
from jax.experimental import pallas as pl
from jax.experimental.pallas import tpu as pltpu

F32 = jnp.float32
BF16 = jnp.bfloat16

D_MODEL = 2048
DEPTH = 2
HEAD_DIM = 64
RW_HEADS = 8
RW_WIDTH = 512
DECAY_LORA = 64
AAA_LORA = 64
GATE_LORA = 128
RW_COLS = 3 * RW_WIDTH + DECAY_LORA + AAA_LORA + GATE_LORA
RW_SPLITS = [RW_WIDTH, 2 * RW_WIDTH, 3 * RW_WIDTH, 3 * RW_WIDTH + DECAY_LORA, 3 * RW_WIDTH + DECAY_LORA + AAA_LORA]
RW_GN_EPS = 64e-5
NSA_HEADS = 8
NSA_KV_HEADS = 2
NSA_REP = NSA_HEADS // NSA_KV_HEADS
NSA_WIDTH = 512
NSA_KV_WIDTH = 128
NSA_COLS = NSA_WIDTH + 6 * NSA_KV_WIDTH + 3 * NSA_HEADS
NSA_SPLITS = [NSA_WIDTH + i * NSA_KV_WIDTH for i in range(7)]
CMP_BLOCK = 32
CMP_STRIDE = 16
SEL_BLOCK = 64
N_SELECT = 16
WINDOW = 512
SEL_FORCE = 1e4
FOX_HEADS = 16
FOX_WIDTH = 1024
FOX_COLS = 3 * FOX_WIDTH + FOX_HEADS
FOX_SPLITS = [FOX_WIDTH, 2 * FOX_WIDTH, 3 * FOX_WIDTH]
IN_SPLITS = [RW_COLS, RW_COLS + NSA_COLS]
N_BUCKETS = 32
MAX_DISTANCE = 1024
Q_BLOCK = 128
DEEPNORM_ALPHA = (2 * DEPTH) ** 0.25
LN_EPS = 1e-5
RMS_EPS = 1e-6
ATT_SCALE = HEAD_DIM ** -0.5

LANES = 128
VMEM_LIMIT = 56 * 1024 * 1024
NEG = -1e30


def _cparams(sem):
    return pltpu.CompilerParams(dimension_semantics=sem, vmem_limit_bytes=VMEM_LIMIT)


def _pad_cols(w, n):
    return jnp.pad(w, ((0, 0), (0, n - w.shape[1])))


def _ffn_in_kernel(x_ref, wg_ref, wu_ref, h_ref):
    x = x_ref[...]
    g = jnp.dot(x, wg_ref[...], preferred_element_type=F32)
    u = jnp.dot(x, wu_ref[...], preferred_element_type=F32)
    h_ref[...] = (g * (1.0 / (1.0 + jnp.exp(-g))) * u).astype(BF16)


def ffn_in(x_bf, wg, wu):
    m, d = x_bf.shape
    f = wg.shape[1]
    tm = min(m, 1024)
    tn = 512
    return pl.pallas_call(
        _ffn_in_kernel,
        grid=(m // tm, f // tn),
        in_specs=[pl.BlockSpec((tm, d), lambda i, j: (i, 0)),
                  pl.BlockSpec((d, tn), lambda i, j: (0, j)),
                  pl.BlockSpec((d, tn), lambda i, j: (0, j))],
        out_specs=pl.BlockSpec((tm, tn), lambda i, j: (i, j)),
        out_shape=jax.ShapeDtypeStruct((m, f), BF16),
        compiler_params=_cparams(("parallel", "arbitrary")),
        name="ffn_in",
    )(x_bf, wg, wu)


def _layer_norm(y, g, b):
    mu = jnp.mean(y, -1, keepdims=True)
    d = y - mu
    var = jnp.mean(d * d, -1, keepdims=True)
    return d * lax.rsqrt(var + LN_EPS) * g + b


def _ffn_out_kernel(h_ref, wd_ref, x_ref, g_ref, b_ref, y_ref, ybf_ref, acc_ref):
    k = pl.program_id(1)

    @pl.when(k == 0)
    def _():
        acc_ref[...] = jnp.zeros_like(acc_ref)

    acc_ref[...] += jnp.dot(h_ref[...], wd_ref[...], preferred_element_type=F32)

    @pl.when(k == pl.num_programs(1) - 1)
    def _():
        y = _layer_norm(DEEPNORM_ALPHA * x_ref[...] + 0.5 * acc_ref[...], g_ref[...], b_ref[...])
        y_ref[...] = y
        ybf_ref[...] = y.astype(BF16)


def ffn_out(h_bf, wd, x, g, b):
    m, f = h_bf.shape
    d = wd.shape[1]
    tm = min(m, 512)
    tk = 512
    return pl.pallas_call(
        _ffn_out_kernel,
        grid=(m // tm, f // tk),
        in_specs=[pl.BlockSpec((tm, tk), lambda i, k: (i, k)),
                  pl.BlockSpec((tk, d), lambda i, k: (k, 0)),
                  pl.BlockSpec((tm, d), lambda i, k: (i, 0)),
                  pl.BlockSpec((1, d), lambda i, k: (0, 0)),
                  pl.BlockSpec((1, d), lambda i, k: (0, 0))],
        out_specs=[pl.BlockSpec((tm, d), lambda i, k: (i, 0)),
                   pl.BlockSpec((tm, d), lambda i, k: (i, 0))],
        out_shape=[jax.ShapeDtypeStruct((m, d), F32), jax.ShapeDtypeStruct((m, d), BF16)],
        scratch_shapes=[pltpu.VMEM((tm, d), F32)],
        compiler_params=_cparams(("parallel", "arbitrary")),
        name="ffn_out",
    )(h_bf, wd, x, g.reshape(1, d), b.reshape(1, d))


def _proj_kernel(x_ref, w_ref, o_ref):
    o_ref[...] = jnp.dot(x_ref[...], w_ref[...], preferred_element_type=F32)


def proj(x_bf, w_bf):
    m, d = x_bf.shape
    n = w_bf.shape[1]
    tm = min(m, 512)
    tn = 128
    for c in (640, 512, 384, 256):
        if n % c == 0:
            tn = c
            break
    return pl.pallas_call(
        _proj_kernel,
        grid=(m // tm, n // tn),
        in_specs=[pl.BlockSpec((tm, d), lambda i, j: (i, 0)),
                  pl.BlockSpec((d, tn), lambda i, j: (0, j))],
        out_specs=pl.BlockSpec((tm, tn), lambda i, j: (i, j)),
        out_shape=jax.ShapeDtypeStruct((m, n), F32),
        compiler_params=_cparams(("parallel", "arbitrary")),
        name="proj",
    )(x_bf, w_bf)


def _log_sigmoid(x):
    return jnp.minimum(x, 0.0) - jnp.log1p(jnp.exp(-jnp.abs(x)))


def _fox_proj_kernel(x_ref, w_ref, bf_ref, qbf_ref, k_ref, v_ref, kbf_ref, vbf_ref, lf_ref):
    x = x_ref[...]
    q = jnp.dot(x, w_ref[:, 0:FOX_WIDTH], preferred_element_type=F32)
    qbf_ref[...] = (q * ATT_SCALE).astype(BF16)
    k = jnp.dot(x, w_ref[:, FOX_WIDTH:2 * FOX_WIDTH], preferred_element_type=F32)
    k_ref[...] = k
    kbf_ref[...] = k.astype(BF16)
    v = jnp.dot(x, w_ref[:, 2 * FOX_WIDTH:3 * FOX_WIDTH], preferred_element_type=F32)
    v_ref[...] = v
    vbf_ref[...] = v.astype(BF16)
    f = jnp.dot(x, w_ref[:, 3 * FOX_WIDTH:3 * FOX_WIDTH + LANES], preferred_element_type=F32)
    lf_ref[...] = _log_sigmoid(f + bf_ref[...])


def fox_proj(x_bf, w_bf, b_f):
    m, d = x_bf.shape
    n = w_bf.shape[1]
    tm = min(m, 256)
    row = lambda width: pl.BlockSpec((tm, width), lambda i: (i, 0))
    bias = jnp.pad(b_f.astype(F32), (0, LANES - FOX_HEADS)).reshape(1, LANES)
    return pl.pallas_call(
        _fox_proj_kernel,
        grid=(m // tm,),
        in_specs=[row(d), pl.BlockSpec((d, n), lambda i: (0, 0)), pl.BlockSpec((1, LANES), lambda i: (0, 0))],
        out_specs=[row(FOX_WIDTH)] * 5 + [row(LANES)],
        out_shape=[jax.ShapeDtypeStruct((m, FOX_WIDTH), BF16),
                   jax.ShapeDtypeStruct((m, FOX_WIDTH), F32), jax.ShapeDtypeStruct((m, FOX_WIDTH), F32),
                   jax.ShapeDtypeStruct((m, FOX_WIDTH), BF16), jax.ShapeDtypeStruct((m, FOX_WIDTH), BF16),
                   jax.ShapeDtypeStruct((m, LANES), F32)],
        compiler_params=_cparams(("parallel",)),
        name="fox_proj",
    )(x_bf, w_bf, bias)


def _split3(x):
    hi = x.astype(BF16)
    r = x - hi.astype(F32)
    mid = r.astype(BF16)
    lo = (r - mid.astype(F32)).astype(BF16)
    return hi, mid, lo


def _seg_sum(x, ones_blk):
    hi, mid, lo = _split3(x)
    d = lambda a: jnp.dot(a, ones_blk, preferred_element_type=F32)
    return d(hi) + d(mid) + d(lo)


def _head_ones():
    r = lax.broadcasted_iota(jnp.int32, (LANES, LANES), 0) // HEAD_DIM
    c = lax.broadcasted_iota(jnp.int32, (LANES, LANES), 1) // HEAD_DIM
    return (r == c).astype(BF16)


def _mix_out_kernel(orw_ref, oc_ref, os_ref, ow_ref, gl_ref, ofox_ref, x_ref, w_ref, gn_ref, gf_ref, g_ref, b_ref,
                    y_ref, ybf_ref):
    ones_blk = _head_ones()
    gate = 1.0 / (1.0 + jnp.exp(-gl_ref[...]))
    pieces = _split3(gate)
    er = lax.broadcasted_iota(jnp.int32, (LANES, NSA_WIDTH), 0)
    ec = lax.broadcasted_iota(jnp.int32, (LANES, NSA_WIDTH), 1) // HEAD_DIM
    onsa = jnp.zeros(oc_ref.shape, F32)
    for c, br_ref in enumerate((oc_ref, os_ref, ow_ref)):
        spread = (er == ec * 3 + c).astype(BF16)
        gate_c = sum(jnp.dot(piece, spread, preferred_element_type=F32) for piece in pieces)
        onsa = onsa + gate_c * br_ref[...]

    def rms(o, gain):
        cols = []
        for c in range(o.shape[1] // LANES):
            blk = o[:, c * LANES:(c + 1) * LANES]
            ms = _seg_sum(blk * blk, ones_blk) * (1.0 / HEAD_DIM)
            cols.append(blk * lax.rsqrt(ms + RMS_EPS) * gain[:, c * LANES:(c + 1) * LANES])
        return jnp.concatenate(cols, axis=1)

    o = jnp.concatenate([orw_ref[...], rms(onsa, gn_ref[...]), rms(ofox_ref[...], gf_ref[...])], axis=1)
    acc = jnp.dot(o.astype(BF16), w_ref[...], preferred_element_type=F32)
    y = _layer_norm(DEEPNORM_ALPHA * x_ref[...] + acc, g_ref[...], b_ref[...])
    y_ref[...] = y
    ybf_ref[...] = y.astype(BF16)


def mix_out(o_rw, o_c, o_s, o_w, gl, o_fox, x, w_bf, gn, gf, g, b):
    m, d = x.shape
    tm = min(m, 256)
    row = lambda width: pl.BlockSpec((tm, width), lambda i: (i, 0))
    full = lambda a, c: pl.BlockSpec((a, c), lambda i: (0, 0))
    return pl.pallas_call(
        _mix_out_kernel,
        grid=(m // tm,),
        in_specs=[row(RW_WIDTH), row(NSA_WIDTH), row(NSA_WIDTH), row(NSA_WIDTH), row(LANES), row(FOX_WIDTH),
                  row(d), full(d, d), full(1, NSA_WIDTH), full(1, FOX_WIDTH), full(1, d), full(1, d)],
        out_specs=[row(d), row(d)],
        out_shape=[jax.ShapeDtypeStruct((m, d), F32), jax.ShapeDtypeStruct((m, d), BF16)],
        compiler_params=_cparams(("parallel",)),
        name="mix_out",
    )(o_rw, o_c, o_s, o_w, gl, o_fox, x, w_bf, gn.reshape(1, -1), gf.reshape(1, -1), g.reshape(1, d),
      b.reshape(1, d))


CUM_BLOCK = 512


def _cumsum_kernel(x_ref, o_ref, carry_ref):
    @pl.when(pl.program_id(0) == 0)
    def _():
        carry_ref[...] = jnp.zeros_like(carry_ref)

    r = lax.broadcasted_iota(jnp.int32, (CUM_BLOCK, CUM_BLOCK), 0)
    c = lax.broadcasted_iota(jnp.int32, (CUM_BLOCK, CUM_BLOCK), 1)
    upper = (r <= c).astype(BF16)
    hi, mid, lo = _split3(x_ref[...])
    d = lambda a: jnp.dot(a, upper, preferred_element_type=F32)
    cs = d(hi) + d(mid) + d(lo) + carry_ref[:, 0:1]
    o_ref[...] = cs
    carry_ref[...] = jnp.broadcast_to(cs[:, CUM_BLOCK - 1:CUM_BLOCK], carry_ref.shape)


def cumsum_lanes(x):
    rows, t = x.shape
    return pl.pallas_call(
        _cumsum_kernel,
        grid=(t // CUM_BLOCK,),
        in_specs=[pl.BlockSpec((rows, CUM_BLOCK), lambda i: (0, i))],
        out_specs=pl.BlockSpec((rows, CUM_BLOCK), lambda i: (0, i)),
        out_shape=jax.ShapeDtypeStruct((rows, t), F32),
        scratch_shapes=[pltpu.VMEM((rows, LANES), F32)],
        compiler_params=_cparams(("arbitrary",)),
        name="cumsum",
    )(x)


def _fox_flash_kernel(qt_ref, kt_ref, q_ref, k_ref, v_ref, ck_ref, o_ref, m_ref, l_ref, acc_ref, *, tq, tk):
    step = pl.program_id(1)
    qi = qt_ref[step]
    ki = kt_ref[step]

    @pl.when(ki == 0)
    def _():
        m_ref[...] = jnp.full_like(m_ref, NEG)
        l_ref[...] = jnp.zeros_like(l_ref)
        acc_ref[...] = jnp.zeros_like(acc_ref)

    left = lax.broadcasted_iota(jnp.int32, (1, LANES), 1) < HEAD_DIM

    def tile(diagonal):
        q = q_ref[...]
        k = k_ref[...]
        v = v_ref[...]
        pv, alphas = [], []
        for hh in range(2):
            qm = jnp.where(left if hh == 0 else jnp.logical_not(left), q, jnp.zeros_like(q))
            s = lax.dot_general(qm, k, (((1,), (1,)), ((), ())), preferred_element_type=F32)
            s = s - ck_ref[0, hh:hh + 1, :]
            if diagonal:
                rows = lax.broadcasted_iota(jnp.int32, (tq, tk), 0)
                cols = lax.broadcasted_iota(jnp.int32, (tq, tk), 1)
                s = jnp.where(rows >= cols, s, NEG)
            m_old = m_ref[hh]
            m_new = jnp.maximum(m_old, jnp.max(s, axis=-1, keepdims=True))
            p = jnp.exp(s - m_new)
            alpha = jnp.exp(m_old - m_new)
            l_ref[hh] = alpha * l_ref[hh] + jnp.sum(p, axis=-1, keepdims=True)
            m_ref[hh] = m_new
            pv.append(jnp.dot(p.astype(BF16), v, preferred_element_type=F32))
            alphas.append(alpha)
        acc_ref[...] = (jnp.where(left, alphas[0], alphas[1]) * acc_ref[...]
                        + jnp.where(left, pv[0], pv[1]))

    @pl.when(ki < qi)
    def _():
        tile(False)

    @pl.when(ki == qi)
    def _():
        tile(True)
        o_ref[...] = acc_ref[...] / jnp.where(left, l_ref[0], l_ref[1])


def _tri_steps(n):
    qs, ks = [], []
    for qi in range(n):
        for ki in range(qi + 1):
            qs.append(qi)
            ks.append(ki)
    return jnp.asarray(qs, jnp.int32), jnp.asarray(ks, jnp.int32)


def fox_flash(q_bf, k_bf, v_bf, ck, t):
    tq = tk = min(t, 512)
    n = t // tq
    qt, kt = _tri_steps(n)
    pairs = FOX_WIDTH // LANES
    grid_spec = pltpu.PrefetchScalarGridSpec(
        num_scalar_prefetch=2,
        grid=(pairs, qt.shape[0]),
        in_specs=[pl.BlockSpec((tq, LANES), lambda p, s, qt, kt: (qt[s], p)),
                  pl.BlockSpec((tk, LANES), lambda p, s, qt, kt: (kt[s], p)),
                  pl.BlockSpec((tk, LANES), lambda p, s, qt, kt: (kt[s], p)),
                  pl.BlockSpec((1, 8, tk), lambda p, s, qt, kt: (p, 0, kt[s]))],
        out_specs=pl.BlockSpec((tq, LANES), lambda p, s, qt, kt: (qt[s], p)),
        scratch_shapes=[pltpu.VMEM((2, tq, 1), F32), pltpu.VMEM((2, tq, 1), F32), pltpu.VMEM((tq, LANES), F32)],
    )
    return pl.pallas_call(
        functools.partial(_fox_flash_kernel, tq=tq, tk=tk),
        grid_spec=grid_spec,
        out_shape=jax.ShapeDtypeStruct((t, FOX_WIDTH), F32),
        compiler_params=_cparams(("parallel", "arbitrary")),
        name="fox_flash",
    )(qt, kt, q_bf, k_bf, v_bf, ck)


def fox_prompt_attn(qbf, kbf, vbf, logf128, t):
    lf = logf128[:t, :FOX_HEADS]
    lf_t = jnp.pad(lf.T.reshape(FOX_HEADS // 2, 2, t), ((0, 0), (0, 6), (0, 0))).reshape(-1, t)
    ck = cumsum_lanes(lf_t).reshape(FOX_HEADS // 2, 8, t)
    return fox_flash(qbf, kbf, vbf, ck, t)


T5_THRESH = (21, 27, 35, 46, 59, 77, 99, 128, 166, 216, 280, 363, 470, 609, 790)
NSA_TILE = 512
NSA_DELTAS = 4
NOT_ALLOWED = -1e30
REMOVED = -2e30


def _t5_bucket_int(dist):
    d = jnp.maximum(dist, 0)
    large = jnp.full(d.shape, N_BUCKETS // 2, jnp.int32)
    for th in T5_THRESH:
        large = large + (d >= th).astype(jnp.int32)
    return jnp.where(d < N_BUCKETS // 2, d, large)


def _bias_lookup(bucket, tab_ref, h):
    val = jnp.full(bucket.shape, tab_ref[0, h], F32)
    for b in range(1, N_BUCKETS):
        val = jnp.where(bucket == b, tab_ref[b, h], val)
    return val


def _bias_tab_kernel(tab_ref, o_ref, *, tb):
    h = pl.program_id(0)
    dl = pl.program_id(1)
    r = lax.broadcasted_iota(jnp.int32, (tb, tb), 0)
    c = lax.broadcasted_iota(jnp.int32, (tb, tb), 1)
    o_ref[0, 0] = _bias_lookup(_t5_bucket_int(dl * tb + r - c), tab_ref, h)


def bias_tiles(rel_bias, tb):
    return pl.pallas_call(
        functools.partial(_bias_tab_kernel, tb=tb),
        grid=(NSA_HEADS, NSA_DELTAS),
        in_specs=[pl.BlockSpec(memory_space=pltpu.SMEM)],
        out_specs=pl.BlockSpec((1, 1, tb, tb), lambda h, d: (h, d, 0, 0)),
        out_shape=jax.ShapeDtypeStruct((NSA_HEADS, NSA_DELTAS, tb, tb), F32),
        compiler_params=_cparams(("parallel", "parallel")),
        name="nsa_bias_tiles",
    )(rel_bias)


NSA_PROJ_COLS = 2432


def nsa_proj_weight(w):
    q, kc, vc, ks, vs, kw, vw, gl = jnp.split(w, NSA_SPLITS, axis=-1)
    dup = []
    for src in (ks, vs, kw, vw):
        for g in range(NSA_KV_HEADS):
            blk = src[:, g * HEAD_DIM:(g + 1) * HEAD_DIM]
            dup += [blk, blk]
    return jnp.concatenate([q, kc, vc, ks, vs, kw, vw, _pad_cols(gl, LANES)] + dup, axis=1).astype(BF16)


def _nsa_proj_kernel(x_ref, w_ref, qbf_ref, kv_ref, kwv_ref, gl_ref, dup_ref):
    x = x_ref[...]
    d = lambda a, b: jnp.dot(x, w_ref[:, a:b], preferred_element_type=F32)
    qbf_ref[...] = (d(0, 512) * ATT_SCALE).astype(BF16)
    kv_ref[...] = d(512, 1024)
    kwv_ref[...] = d(1024, 1280)
    gl_ref[...] = d(1280, 1408)
    dup_ref[...] = d(1408, 2432).astype(BF16)


def nsa_proj(x_bf, w_bf):
    m, dm = x_bf.shape
    tm = min(m, 256)
    row = lambda width: pl.BlockSpec((tm, width), lambda i: (i, 0))
    return pl.pallas_call(
        _nsa_proj_kernel,
        grid=(m // tm,),
        in_specs=[row(dm), pl.BlockSpec((dm, NSA_PROJ_COLS), lambda i: (0, 0))],
        out_specs=[row(512), row(512), row(256), row(LANES), row(1024)],
        out_shape=[jax.ShapeDtypeStruct((m, 512), BF16), jax.ShapeDtypeStruct((m, 512), F32),
                   jax.ShapeDtypeStruct((m, 256), F32), jax.ShapeDtypeStruct((m, LANES), F32),
                   jax.ShapeDtypeStruct((m, 1024), BF16)],
        compiler_params=_cparams(("parallel",)),
        name="nsa_proj",
    )(x_bf, w_bf)


def nsa_compress_weight(cmp_w, cmp_b):
    w = cmp_w.reshape(2, 2, CMP_STRIDE, HEAD_DIM, HEAD_DIM)
    eye = jnp.eye(2, dtype=F32)
    t = jnp.einsum('whjde,wv,gk->jwgdhvke', w, eye, eye)
    t = jnp.broadcast_to(t[..., None, :], t.shape[:-1] + (2, HEAD_DIM))
    wmat = t.reshape(CMP_STRIDE * 256, 1024).astype(BF16)
    bias = jnp.broadcast_to(cmp_b[:, None, None, :], (2, NSA_KV_HEADS, 2, HEAD_DIM)).reshape(1, 512)
    return wmat, bias.astype(F32)


def _nsa_compress_kernel(x_ref, w_ref, b_ref, o_ref, acc_ref):
    j = pl.program_id(0)

    @pl.when(j == 0)
    def _():
        acc_ref[...] = jnp.zeros_like(acc_ref)

    acc_ref[...] += jnp.dot(x_ref[...].astype(BF16), w_ref[...], preferred_element_type=F32)

    @pl.when(j == pl.num_programs(0) - 1)
    def _():
        n = acc_ref.shape[0]
        first = acc_ref[:, 0:512]
        second = pltpu.roll(acc_ref[:, 512:1024], n - 1, axis=0)
        o_ref[...] = (first + second + b_ref[...]).astype(BF16)


def nsa_compress_rows(kv4, wmat, bias):
    t = kv4.shape[0]
    n = t // CMP_STRIDE
    x = kv4.reshape(n, CMP_STRIDE * 512)
    return pl.pallas_call(
        _nsa_compress_kernel,
        grid=(CMP_STRIDE,),
        in_specs=[pl.BlockSpec((n, 256), lambda j: (0, 2 * j)),
                  pl.BlockSpec((256, 1024), lambda j: (j, 0)),
                  pl.BlockSpec((1, 512), lambda j: (0, 0))],
        out_specs=pl.BlockSpec((n, 512), lambda j: (0, 0)),
        out_shape=jax.ShapeDtypeStruct((n, 512), BF16),
        scratch_shapes=[pltpu.VMEM((n, 1024), F32)],
        compiler_params=_cparams(("arbitrary",)),
        name="nsa_compress",
    )(x, wmat, bias)


def _nsa_cmp_kernel(tab_ref, q_ref, kv_ref, oc_ref, sel_ref, *, tq, ncp, n_sel):
    t0 = pl.program_id(0) * tq
    rows = lax.broadcasted_iota(jnp.int32, (tq, ncp), 0) + t0
    cols = lax.broadcasted_iota(jnp.int32, (tq, ncp), 1)
    dist = rows - (cols * CMP_STRIDE + (CMP_BLOCK - 1))
    valid = dist >= 0
    bucket = _t5_bucket_int(dist)
    left = lax.broadcasted_iota(jnp.int32, (1, LANES), 1) < HEAD_DIM
    ci = lax.broadcasted_iota(jnp.int32, (ncp, LANES), 0) * CMP_STRIDE
    cj = lax.broadcasted_iota(jnp.int32, (ncp, LANES), 1)
    cover = jnp.where((ci < (cj + 1) * SEL_BLOCK) & (ci + CMP_BLOCK > cj * SEL_BLOCK), 1.0, 0.0).astype(BF16)
    imp = [jnp.zeros((tq, LANES), F32) for _ in range(NSA_KV_HEADS)]
    for p in range(NSA_HEADS // 2):
        g = p // (NSA_REP // 2)
        qp = q_ref[:, p * LANES:(p + 1) * LANES]
        kc = kv_ref[:, g * LANES:(g + 1) * LANES]
        vc = kv_ref[:, (NSA_KV_HEADS + g) * LANES:(NSA_KV_HEADS + g + 1) * LANES]
        outs = []
        for hh in range(2):
            qm = jnp.where(left if hh == 0 else jnp.logical_not(left), qp, jnp.zeros_like(qp))
            s = lax.dot_general(qm, kc, (((1,), (1,)), ((), ())), preferred_element_type=F32)
            s = jnp.where(valid, s + _bias_lookup(bucket, tab_ref, 2 * p + hh), NEG)
            m = jnp.max(s, axis=-1, keepdims=True)
            e = jnp.where(valid, jnp.exp(s - m), 0.0)
            pr = (e / jnp.maximum(jnp.sum(e, axis=-1, keepdims=True), 1e-30)).astype(BF16)
            outs.append(jnp.dot(pr, vc, preferred_element_type=F32))
            imp[g] = imp[g] + jnp.dot(pr, cover, preferred_element_type=F32)
        oc_ref[:, p * LANES:(p + 1) * LANES] = jnp.where(left, outs[0], outs[1])

    lane = lax.broadcasted_iota(jnp.int32, (tq, LANES), 1)
    lane_f = lane.astype(F32)
    jq = (lax.broadcasted_iota(jnp.int32, (tq, LANES), 0) + t0) // SEL_BLOCK
    forced = (lane == 0) | (lane == jq) | (lane == jq - 1)
    allowed = lane <= jq
    for g in range(NSA_KV_HEADS):
        work = jnp.where(allowed, imp[g] + jnp.where(forced, SEL_FORCE, 0.0), NOT_ALLOWED)
        sel = jnp.zeros((tq, LANES), F32)
        for _ in range(n_sel):
            mx = jnp.max(work, axis=-1, keepdims=True)
            idx = jnp.min(jnp.where(work == mx, lane_f, float(LANES)), axis=-1, keepdims=True)
            hit = lane_f == idx
            sel = jnp.where(hit & (mx > 0.5 * NOT_ALLOWED), 1.0, sel)
            work = jnp.where(hit, REMOVED, work)
        sel_ref[:, g * LANES:(g + 1) * LANES] = sel.astype(BF16)


def nsa_cmp_select(qbf, kvc, rel_bias, t):
    tq = min(t, 256)
    ncp = kvc.shape[0]
    n_sel = min(N_SELECT, t // SEL_BLOCK)
    return pl.pallas_call(
        functools.partial(_nsa_cmp_kernel, tq=tq, ncp=ncp, n_sel=n_sel),
        grid=(t // tq,),
        in_specs=[pl.BlockSpec(memory_space=pltpu.SMEM),
                  pl.BlockSpec((tq, 512), lambda i: (i, 0)),
                  pl.BlockSpec((ncp, 512), lambda i: (0, 0))],
        out_specs=[pl.BlockSpec((tq, 512), lambda i: (i, 0)), pl.BlockSpec((tq, 256), lambda i: (i, 0))],
        out_shape=[jax.ShapeDtypeStruct((t, 512), F32), jax.ShapeDtypeStruct((t, 256), BF16)],
        compiler_params=_cparams(("parallel",)),
        name="nsa_cmp_select",
    )(rel_bias, qbf, kvc)


def _nsa_flash_kernel(qt_ref, kt_ref, ft_ref, q_ref, k_ref, v_ref, b_ref, *rest, tq, tk, selected):
    if selected:
        sm_ref, ex_ref, o_ref, m_ref, l_ref, acc_ref = rest
    else:
        o_ref, m_ref, l_ref, acc_ref = rest
    step = pl.program_id(1)
    qi = qt_ref[step]
    ki = kt_ref[step]

    @pl.when(ft_ref[step] == 1)
    def _():
        m_ref[...] = jnp.full_like(m_ref, NEG)
        l_ref[...] = jnp.zeros_like(l_ref)
        acc_ref[...] = jnp.zeros_like(acc_ref)

    left = lax.broadcasted_iota(jnp.int32, (1, LANES), 1) < HEAD_DIM

    def tile(diagonal):
        q = q_ref[...]
        k = k_ref[...]
        v = v_ref[...]
        rows = lax.broadcasted_iota(jnp.int32, (tq, tk), 0)
        cols = lax.broadcasted_iota(jnp.int32, (tq, tk), 1)
        if selected:
            valid = jnp.dot(sm_ref[...], ex_ref[...], preferred_element_type=F32) > 0.5
            if diagonal:
                valid = valid & (rows >= cols)
        else:
            valid = (rows >= cols) if diagonal else (rows < cols)
        pv, alphas = [], []
        for hh in range(2):
            qm = jnp.where(left if hh == 0 else jnp.logical_not(left), q, jnp.zeros_like(q))
            s = lax.dot_general(qm, k, (((1,), (1,)), ((), ())), preferred_element_type=F32)
            s = jnp.where(valid, s + b_ref[hh, 0], NEG)
            m_old = m_ref[hh]
            m_new = jnp.maximum(m_old, jnp.max(s, axis=-1, keepdims=True))
            p = jnp.where(valid, jnp.exp(s - m_new), 0.0)
            alpha = jnp.exp(m_old - m_new)
            l_ref[hh] = alpha * l_ref[hh] + jnp.sum(p, axis=-1, keepdims=True)
            m_ref[hh] = m_new
            pv.append(jnp.dot(p.astype(BF16), v, preferred_element_type=F32))
            alphas.append(alpha)
        acc_ref[...] = (jnp.where(left, alphas[0], alphas[1]) * acc_ref[...]
                        + jnp.where(left, pv[0], pv[1]))

    @pl.when(ki < qi)
    def _():
        tile(False)

    @pl.when(ki == qi)
    def _():
        tile(True)
        o_ref[...] = acc_ref[...] / jnp.where(left, l_ref[0], l_ref[1])


def nsa_flash(qbf, dup, btab, t, selmask=None):
    selected = selmask is not None
    tq = tk = btab.shape[-1]
    n = t // tq
    qs, ks, fs = [], [], []
    for qi in range(n):
        lo = 0 if selected else max(qi - (WINDOW // tk), 0)
        for ki in range(lo, qi + 1):
            qs.append(qi)
            ks.append(ki)
            fs.append(1 if ki == lo else 0)
    qt, kt, ft = (jnp.asarray(a, jnp.int32) for a in (qs, ks, fs))
    kcol = 0 if selected else 4
    half = NSA_REP // 2
    in_specs = [pl.BlockSpec((tq, LANES), lambda p, s, qt, kt, ft: (qt[s], p)),
                pl.BlockSpec((tk, LANES), lambda p, s, qt, kt, ft: (kt[s], kcol + p // half)),
                pl.BlockSpec((tk, LANES), lambda p, s, qt, kt, ft: (kt[s], kcol + 2 + p // half)),
                pl.BlockSpec((2, 1, tq, tk),
                             lambda p, s, qt, kt, ft: (p, jnp.minimum(qt[s] - kt[s], NSA_DELTAS - 1), 0, 0))]
    args = [qbf, dup, dup, btab]
    if selected:
        n_blk = tk // SEL_BLOCK
        jj = jnp.arange(LANES)[:, None]
        ll = jnp.arange(t)[None, :]
        expand = (jj == ll // SEL_BLOCK).astype(BF16)
        in_specs += [pl.BlockSpec((tq, LANES), lambda p, s, qt, kt, ft: (qt[s], p // half)),
                     pl.BlockSpec((LANES, tk), lambda p, s, qt, kt, ft: (0, kt[s]))]
        args += [selmask, expand]
    grid_spec = pltpu.PrefetchScalarGridSpec(
        num_scalar_prefetch=3,
        grid=(NSA_HEADS // 2, len(qs)),
        in_specs=in_specs,
        out_specs=pl.BlockSpec((tq, LANES), lambda p, s, qt, kt, ft: (qt[s], p)),
        scratch_shapes=[pltpu.VMEM((2, tq, 1), F32), pltpu.VMEM((2, tq, 1), F32), pltpu.VMEM((tq, LANES), F32)],
    )
    return pl.pallas_call(
        functools.partial(_nsa_flash_kernel, tq=tq, tk=tk, selected=selected),
        grid_spec=grid_spec,
        out_shape=jax.ShapeDtypeStruct((t, NSA_WIDTH), F32),
        compiler_params=_cparams(("parallel", "arbitrary")),
        name="nsa_sel_flash" if selected else "nsa_win_flash",
    )(qt, kt, ft, *args)


def nsa_prompt_attn(qbf, kv4, dup, wmat, cbias, rel_bias, btab, t):
    kvc = nsa_compress_rows(kv4[:t], wmat, cbias)
    o_c, selmask = nsa_cmp_select(qbf, kvc, rel_bias, t)
    o_s = nsa_flash(qbf, dup, btab, t, selmask)
    o_w = nsa_flash(qbf, dup, btab, t)
    return o_c, o_s, o_w


RW_CHUNK = 128
RW_PAIRS = RW_WIDTH // LANES


def _softplus(x):
    return jnp.maximum(x, 0.0) + jnp.log1p(jnp.exp(-jnp.abs(x)))


def _sigmoid(x):
    return 1.0 / (1.0 + jnp.exp(-x))


def _seg_sum_wide(x, ones_blk):
    return jnp.concatenate([_seg_sum(x[:, c * LANES:(c + 1) * LANES], ones_blk)
                            for c in range(x.shape[1] // LANES)], axis=1)


def _rwkv_features(p, shifted, mu, w0, w2p, a0, a2p, g2, k_k, k_a, ones_blk):
    z = p + (shifted - p) * mu
    r = z[:, 0:RW_WIDTH]
    k = z[:, RW_WIDTH:2 * RW_WIDTH]
    v = z[:, 2 * RW_WIDTH:3 * RW_WIDTH]
    lora = z[:, 3 * RW_WIDTH:3 * RW_WIDTH + LANES]
    gd = z[:, 3 * RW_WIDTH + LANES:]
    w_log = -_softplus(-(w0 + jnp.dot(jnp.tanh(lora).astype(BF16), w2p, preferred_element_type=F32))) - 0.5
    decay = jnp.exp(-jnp.exp(w_log))
    a = _sigmoid(a0 + jnp.dot(lora.astype(BF16), a2p, preferred_element_type=F32))
    g = jnp.dot(_sigmoid(gd).astype(BF16), g2, preferred_element_type=F32)
    kk = k * k_k
    kk = kk / jnp.maximum(jnp.sqrt(_seg_sum_wide(kk * kk, ones_blk)), 1e-12)
    kh = k * (1.0 + (a - 1.0) * k_a)
    return r, decay, kh, v, kk, kk * a, g


def _rwkv_finish(out, r, kh, v, g, r_k, gn_g, gn_b, ones_blk):
    m = _seg_sum_wide(out, ones_blk) * (1.0 / HEAD_DIM)
    d = out - m
    var = _seg_sum_wide(d * d, ones_blk) * (1.0 / HEAD_DIM)
    y = d * lax.rsqrt(var + RW_GN_EPS) * gn_g + gn_b
    bonus = _seg_sum_wide(r * kh * r_k, ones_blk) * v
    return (y + bonus) * g


def _rwkv_kernel(p_ref, mu_ref, w0_ref, w2_ref, a0_ref, a2_ref, g2_ref, kk_ref, ka_ref, rk_ref, gng_ref, gnb_ref,
                 spread_ref, y_ref, sout_ref, s_ref, prev_ref, rows_ref, vt_ref, ot_ref, vb_ref):
    n = RW_CHUNK
    c = pl.program_id(0)

    @pl.when(c == 0)
    def _():
        s_ref[...] = jnp.zeros_like(s_ref)
        prev_ref[...] = jnp.zeros_like(prev_ref)

    p = p_ref[...]
    first = lax.broadcasted_iota(jnp.int32, (n, 1), 0) == 0
    shifted = jnp.where(first, prev_ref[0:1, :], pltpu.roll(p, 1, axis=0))
    prev_ref[...] = jnp.broadcast_to(p[n - 1:n, :], prev_ref.shape)
    ones_blk = _head_ones()
    r, decay, kh, v, kk, kka, g = _rwkv_features(p, shifted, mu_ref[...], w0_ref[...], w2_ref[...], a0_ref[...],
                                                 a2_ref[...], g2_ref[...], kk_ref[...], ka_ref[...], ones_blk)
    for qi, arr in enumerate((r, decay, kh, kk, kka)):
        rows_ref[qi] = arr
    for pr in range(RW_PAIRS):
        vt_ref[pr] = v[:, pr * LANES:(pr + 1) * LANES].T
    ot_ref[...] = jnp.zeros_like(ot_ref)
    lane_t = lax.broadcasted_iota(jnp.int32, (HEAD_DIM, n), 1)
    left = lax.broadcasted_iota(jnp.int32, (1, LANES), 1) < HEAD_DIM
    half_sum = (lax.broadcasted_iota(jnp.int32, (LANES, 2 * LANES), 0) // HEAD_DIM
                == lax.broadcasted_iota(jnp.int32, (LANES, 2 * LANES), 1) // LANES).astype(BF16)

    def group(j, carry):
        base = pl.multiple_of(j * 8, 8)
        for pr in range(RW_PAIRS):
            vtr = pltpu.roll(vt_ref[pr], lax.rem(n - base, n), axis=1)
            pieces = _split3(vtr[0:HEAD_DIM]) + _split3(vtr[HEAD_DIM:])
            vb_ref[pr] = jnp.dot(jnp.concatenate(pieces, axis=1), spread_ref[...], preferred_element_type=F32)
        blk = [[rows_ref[q, pl.ds(base, 8), pr * LANES:(pr + 1) * LANES] for q in range(5)]
               for pr in range(RW_PAIRS)]
        for i in range(8):
            hit = lane_t == base + i
            for pr in range(RW_PAIRS):
                rv = lambda q: blk[pr][q][i:i + 1, :]
                s = s_ref[pr]
                m = s * rv(3)
                sa0 = jnp.sum(jnp.where(left, m, 0.0), axis=1, keepdims=True)
                sa1 = jnp.sum(jnp.where(left, 0.0, m), axis=1, keepdims=True)
                s = s * rv(1) - jnp.where(left, sa0, sa1) * rv(4) + vb_ref[pr, :, i * LANES:(i + 1) * LANES] * rv(2)
                s_ref[pr] = s
                oc = jnp.dot((s * rv(0)).astype(BF16), half_sum, preferred_element_type=F32)
                ot_ref[pr, 0:HEAD_DIM, :] = jnp.where(hit, oc[:, 0:LANES], ot_ref[pr, 0:HEAD_DIM, :])
                ot_ref[pr, HEAD_DIM:, :] = jnp.where(hit, oc[:, LANES:], ot_ref[pr, HEAD_DIM:, :])
        return carry

    lax.fori_loop(0, n // 8, group, 0)
    out = jnp.concatenate([ot_ref[pr].T for pr in range(RW_PAIRS)], axis=1)
    y_ref[...] = _rwkv_finish(out, r, kh, v, g, rk_ref[...], gng_ref[...], gnb_ref[...], ones_blk)

    @pl.when(c == pl.num_programs(0) - 1)
    def _():
        for pr in range(RW_PAIRS):
            sout_ref[2 * pr] = s_ref[pr][:, 0:HEAD_DIM]
            sout_ref[2 * pr + 1] = s_ref[pr][:, HEAD_DIM:]


def _rwkv_params(mu, w0, w2, a0, a2, g2, k_k, k_a, r_k, gn_g, gn_b):
    row = lambda a: a.reshape(1, -1).astype(F32)
    zeros = jnp.zeros((DECAY_LORA, RW_WIDTH), F32)
    w2p = jnp.concatenate([w2, zeros], axis=0).astype(BF16)
    a2p = jnp.concatenate([zeros, a2], axis=0).astype(BF16)
    return (row(mu), row(w0), w2p, row(a0), a2p, g2.astype(BF16), row(k_k), row(k_a), row(r_k), row(gn_g), row(gn_b))


def rwkv_prompt(p_rw, t, params):
    n = RW_CHUNK
    rr = jnp.arange(2 * 3 * LANES)[:, None]
    cc = jnp.arange(8 * LANES)[None, :]
    spread = ((rr % LANES == cc // LANES) & (rr // (3 * LANES) == (cc % LANES) // HEAD_DIM)).astype(BF16)
    args = _rwkv_params(*params) + (spread,)
    full = lambda a: pl.BlockSpec(a.shape, lambda i: (0,) * a.ndim)
    return pl.pallas_call(
        _rwkv_kernel,
        grid=(t // n,),
        in_specs=[pl.BlockSpec((n, RW_COLS), lambda i: (i, 0))] + [full(a) for a in args],
        out_specs=[pl.BlockSpec((n, RW_WIDTH), lambda i: (i, 0)),
                   pl.BlockSpec((RW_HEADS, HEAD_DIM, HEAD_DIM), lambda i: (0, 0, 0))],
        out_shape=[jax.ShapeDtypeStruct((t, RW_WIDTH), F32),
                   jax.ShapeDtypeStruct((RW_HEADS, HEAD_DIM, HEAD_DIM), F32)],
        scratch_shapes=[pltpu.VMEM((RW_PAIRS, HEAD_DIM, LANES), F32),
                        pltpu.VMEM((8, RW_COLS), F32),
                        pltpu.VMEM((5, n, RW_WIDTH), F32),
                        pltpu.VMEM((RW_PAIRS, LANES, n), F32),
                        pltpu.VMEM((RW_PAIRS, LANES, n), F32),
                        pltpu.VMEM((RW_PAIRS, HEAD_DIM, 8 * LANES), F32)],
        compiler_params=_cparams(("arbitrary",)),
        name="rwkv_prompt",
    )(p_rw, *args)


def _rwkv_sample_kernel(p_ref, sh_ref, s_in_ref, mu_ref, w0_ref, w2_ref, a0_ref, a2_ref, g2_ref, kk_ref, ka_ref,
                        rk_ref, gng_ref, gnb_ref, y_ref, s_out_ref, rows_ref, vt_ref, ot_ref):
    n = p_ref.shape[0]
    ones_blk = _head_ones()
    r, decay, kh, v, kk, kka, g = _rwkv_features(p_ref[...], sh_ref[...], mu_ref[...], w0_ref[...], w2_ref[...],
                                                 a0_ref[...], a2_ref[...], g2_ref[...], kk_ref[...], ka_ref[...],
                                                 ones_blk)
    for qi, arr in enumerate((r, decay, kh, kk, kka)):
        rows_ref[qi, 0] = arr
        rows_ref[qi, 1] = pltpu.roll(arr, RW_WIDTH - HEAD_DIM, axis=1)
    vt_ref[...] = jnp.zeros_like(vt_ref)
    ot_ref[...] = jnp.zeros_like(ot_ref)
    vpad = jnp.concatenate([v, jnp.zeros((LANES - n, RW_WIDTH), F32)], axis=0) if n < LANES else v
    for pr in range(RW_PAIRS):
        vt_ref[pr] = vpad[:, pr * LANES:(pr + 1) * LANES].T
    lane_t = lax.broadcasted_iota(jnp.int32, (HEAD_DIM, LANES), 1)

    def group(j, carry):
        base = pl.multiple_of(j * 8, 8)
        for h in range(RW_HEADS):
            pr, par = h // 2, h % 2
            lo, ro = pr * LANES, par * HEAD_DIM
            blk = [rows_ref[q, par, pl.ds(base, 8), lo:lo + HEAD_DIM] for q in range(5)]
            ot = ot_ref[pr, ro:ro + HEAD_DIM, :]
            vt = vt_ref[pr, ro:ro + HEAD_DIM, :]
            for i in range(8):
                hit = lane_t == base + i
                rv = lambda q: blk[q][i:i + 1, :]
                s = s_in_ref[base + i, h]
                sa = -jnp.sum(s * rv(3), axis=1, keepdims=True)
                vcol = jnp.sum(jnp.where(hit, vt, 0.0), axis=1, keepdims=True)
                s = s * rv(1) + sa * rv(4) + vcol * rv(2)
                s_out_ref[base + i, h] = s
                ocol = jnp.sum(s * rv(0), axis=1, keepdims=True)
                ot = jnp.where(hit, ocol, ot)
            ot_ref[pr, ro:ro + HEAD_DIM, :] = ot
        return carry

    lax.fori_loop(0, n // 8, group, 0)
    out = jnp.concatenate([ot_ref[pr].T for pr in range(RW_PAIRS)], axis=1)[0:n]
    y_ref[...] = _rwkv_finish(out, r, kh, v, g, rk_ref[...], gng_ref[...], gnb_ref[...], ones_blk)


def rwkv_sample(p_rw, shift, state, params):
    b = p_rw.shape[0]
    n = min(b, 32)
    args = _rwkv_params(*params)
    full = lambda a: pl.BlockSpec(a.shape, lambda i: (0,) * a.ndim)
    st = pl.BlockSpec((n, RW_HEADS, HEAD_DIM, HEAD_DIM), lambda i: (i, 0, 0, 0))
    return pl.pallas_call(
        _rwkv_sample_kernel,
        grid=(b // n,),
        in_specs=[pl.BlockSpec((n, RW_COLS), lambda i: (i, 0)), pl.BlockSpec((n, RW_COLS), lambda i: (i, 0)), st]
        + [full(a) for a in args],
        out_specs=[pl.BlockSpec((n, RW_WIDTH), lambda i: (i, 0)), st],
        out_shape=[jax.ShapeDtypeStruct((b, RW_WIDTH), F32), jax.ShapeDtypeStruct(state.shape, F32)],
        scratch_shapes=[pltpu.VMEM((5, 2, n, RW_WIDTH), F32),
                        pltpu.VMEM((RW_PAIRS, LANES, LANES), F32),
                        pltpu.VMEM((RW_PAIRS, LANES, LANES), F32)],
        compiler_params=_cparams(("parallel",)),
        name="rwkv_sample",
    )(p_rw, shift, state, *args)


def _t5_bucket_host(dist):
    d = jnp.maximum(dist, 0)
    large = N_BUCKETS // 2 + sum((d >= th).astype(jnp.int32) for th in T5_THRESH)
    return jnp.where(d < N_BUCKETS // 2, d, large)


def _fox_paged_kernel(pt_ref, q_ref, qt_ref, kn_ref, vnt_ref, lfn_ref, *refs, n_pages, page):
    kp, vp, lp = refs[0:n_pages], refs[n_pages:2 * n_pages], refs[2 * n_pages:3 * n_pages]
    o_ref, lg_ref = refs[3 * n_pages], refs[3 * n_pages + 1]
    own = (lax.broadcasted_iota(jnp.int32, (FOX_HEADS, LANES), 0)
           == lax.broadcasted_iota(jnp.int32, (FOX_HEADS, LANES), 1))
    lane_h = lax.broadcasted_iota(jnp.int32, (HEAD_DIM, LANES), 1)
    qt = qt_ref[0]
    qcol = [jnp.sum(jnp.where(lane_h == h, qt, 0.0), axis=1, keepdims=True) for h in range(FOX_HEADS)]
    later = (lax.broadcasted_iota(jnp.int32, (page, page), 0)
             > lax.broadcasted_iota(jnp.int32, (page, page), 1)).astype(BF16)

    for pg in range(n_pages):
        for h in range(FOX_HEADS):
            lg_ref[pg, h:h + 1, :] = jnp.sum(kp[pg][0, 0, h] * qcol[h], axis=0, keepdims=True)
    after = jnp.sum(jnp.where(own, lfn_ref[0], 0.0), axis=1, keepdims=True)
    logits = [None] * n_pages
    for pg in reversed(range(n_pages)):
        lft = lp[pg][0, 0]
        within = sum(jnp.dot(piece, later, preferred_element_type=F32) for piece in _split3(lft))
        logits[pg] = lg_ref[pg] + within + after
        after = after + jnp.sum(lft, axis=1, keepdims=True)
    l_new = jnp.sum(q_ref[0] * kn_ref[0].astype(BF16).astype(F32), axis=1, keepdims=True)
    m = l_new
    for lg in logits:
        m = jnp.maximum(m, jnp.max(lg, axis=1, keepdims=True))
    p_new = jnp.exp(l_new - m)
    den = p_new
    for pg in range(n_pages):
        p = jnp.exp(logits[pg] - m)
        den = den + jnp.sum(p, axis=1, keepdims=True)
        lg_ref[pg] = p
    ot = jnp.zeros((HEAD_DIM, LANES), F32)
    for h in range(FOX_HEADS):
        acc = jnp.zeros((HEAD_DIM, page), F32)
        for pg in range(n_pages):
            acc = acc + lg_ref[pg, h:h + 1, :] * vp[pg][0, 0, h]
        ot = jnp.where(lane_h == h, jnp.sum(acc, axis=1, keepdims=True), ot)
    as_row = lambda col: jnp.sum(jnp.where(own, col, 0.0), axis=0, keepdims=True)
    vnt = vnt_ref[0].astype(BF16).astype(F32)
    o_ref[0] = (ot + vnt * as_row(p_new)) / jnp.maximum(as_row(den), 1e-30)


def fox_paged(pt, layer, qbf, k_new, v_new, lf_new, cache_k, cache_v, cache_lf):
    b, n_pages = pt.shape
    page = cache_k.shape[2]
    kt = jnp.transpose(cache_k, (0, 1, 3, 4, 2))
    vt = jnp.transpose(cache_v, (0, 1, 3, 4, 2))
    lft = jnp.transpose(cache_lf, (0, 1, 3, 2))
    q3 = qbf.astype(F32).reshape(b, FOX_HEADS, HEAD_DIM)
    lanes_t = lambda a: jnp.pad(jnp.swapaxes(a, 1, 2), ((0, 0), (0, 0), (0, LANES - FOX_HEADS)))
    per_seq = lambda *shape: pl.BlockSpec((1,) + shape, lambda i, pt: (i,) + (0,) * len(shape))

    def paged(tail, j):
        return pl.BlockSpec((1, 1) + tail, lambda i, pt, j=j: (layer, pt[i, j]) + (0,) * len(tail))

    in_specs = ([per_seq(FOX_HEADS, HEAD_DIM), per_seq(HEAD_DIM, LANES), per_seq(FOX_HEADS, HEAD_DIM),
                 per_seq(HEAD_DIM, LANES), per_seq(1, LANES)]
                + [paged((FOX_HEADS, HEAD_DIM, page), j) for j in range(n_pages)]
                + [paged((FOX_HEADS, HEAD_DIM, page), j) for j in range(n_pages)]
                + [paged((FOX_HEADS, page), j) for j in range(n_pages)])
    grid_spec = pltpu.PrefetchScalarGridSpec(
        num_scalar_prefetch=1, grid=(b,), in_specs=in_specs, out_specs=per_seq(HEAD_DIM, LANES),
        scratch_shapes=[pltpu.VMEM((n_pages, FOX_HEADS, page), F32)])
    out = pl.pallas_call(
        functools.partial(_fox_paged_kernel, n_pages=n_pages, page=page),
        grid_spec=grid_spec,
        out_shape=jax.ShapeDtypeStruct((b, HEAD_DIM, LANES), F32),
        compiler_params=_cparams(("parallel",)),
        name="fox_paged",
    )(pt, q3, lanes_t(q3), k_new.reshape(b, FOX_HEADS, HEAD_DIM), lanes_t(v_new.reshape(b, FOX_HEADS, HEAD_DIM)),
      lf_new.reshape(b, 1, LANES), *([kt] * n_pages), *([vt] * n_pages), *([lft] * n_pages))
    return jnp.swapaxes(out[:, :, :FOX_HEADS], 1, 2).reshape(b, FOX_WIDTH)


def _masked_softmax_rows(parts, valids):
    m = jnp.full((parts[0].shape[0], 1), NEG, F32)
    for lg, ok in zip(parts, valids):
        m = jnp.maximum(m, jnp.max(jnp.where(ok, lg, NEG), axis=1, keepdims=True))
    es = [jnp.where(ok, jnp.exp(jnp.where(ok, lg, NEG) - m), 0.0) for lg, ok in zip(parts, valids)]
    den = sum(jnp.sum(e, axis=1, keepdims=True) for e in es)
    return es, jnp.maximum(den, 1e-30)


def _wg_mask(n_rows, n_w, w_pick):
    col = lax.broadcasted_iota(jnp.int32, (NSA_HEADS, n_rows), 1)
    head = lax.broadcasted_iota(jnp.int32, (NSA_HEADS, n_rows), 0)
    return ((col // NSA_KV_HEADS) % n_w == w_pick) & (col % NSA_KV_HEADS == head // NSA_REP)


def _nsa_decode_kernel(pt_ref, q_ref, new_ref, newcol_ref, win_ref, wcat_ref, cb_ref, bc_ref, bs_ref, bw_ref, b0_ref,
                       ex_ref, *refs, n_pages, page):
    pages = refs[0:n_pages]
    o_ref, wout_ref, t_ref = refs[n_pages], refs[n_pages + 1], refs[n_pages + 2]
    nt = (((1,), (1,)), ((), ()))
    q = q_ref[0]
    b0 = b0_ref[:, 0:1]
    past = n_pages * page
    ncr = past // CMP_STRIDE
    g2 = NSA_KV_HEADS
    row8 = lax.broadcasted_iota(jnp.int32, (NSA_HEADS, 1), 0)
    grp0 = row8 < NSA_REP

    def qk(tile0, tile1):
        return jnp.where(grp0, jnp.dot(q, tile0.astype(BF16), preferred_element_type=F32),
                         jnp.dot(q, tile1.astype(BF16), preferred_element_type=F32))

    def pv(p, tile0, tile1):
        pb = p.astype(BF16)
        return jnp.where(grp0, lax.dot_general(pb, tile0.astype(BF16), nt, preferred_element_type=F32),
                         lax.dot_general(pb, tile1.astype(BF16), nt, preferred_element_type=F32))

    comp = []
    for w in range(2):
        for pg in range(n_pages):
            t_ref[w, pg * page:(pg + 1) * page, :] = pages[pg][0, 0, w].reshape(g2 * HEAD_DIM, page).T
        acc = jnp.zeros((ncr, 2 * LANES), F32)
        for j in range(CMP_STRIDE):
            rows = t_ref[w, pl.ds(j, ncr, stride=CMP_STRIDE), :]
            acc = acc + jnp.dot(rows.astype(BF16), wcat_ref[w, j], preferred_element_type=F32)
        comp.append((acc[:, 0:LANES] + pltpu.roll(acc[:, LANES:], ncr - 1, axis=0) + cb_ref[w:w + 1, :]).astype(BF16))
    kc, vc = comp

    qf = q.astype(F32)
    lane128 = lax.broadcasted_iota(jnp.int32, (NSA_HEADS, LANES), 1)
    own_half = (lane128 // HEAD_DIM == 0) == grp0
    qg = jnp.where(own_half, jnp.concatenate([qf, qf], axis=1), 0.0).astype(BF16)
    lane_c = lax.broadcasted_iota(jnp.int32, (NSA_HEADS, ncr), 1)
    lc = lax.dot_general(qg, kc, nt, preferred_element_type=F32) + bc_ref[...]
    (e_c,), den_c = _masked_softmax_rows([lc], [lane_c < ncr - 1])
    p_c = (e_c / den_c).astype(BF16)
    oc2 = jnp.dot(p_c, vc, preferred_element_type=F32)
    o_ref[0, 0] = jnp.where(grp0, oc2[:, 0:HEAD_DIM], oc2[:, HEAD_DIM:])

    ci = lax.broadcasted_iota(jnp.int32, (ncr, LANES), 0) * CMP_STRIDE
    cj = lax.broadcasted_iota(jnp.int32, (ncr, LANES), 1)
    cover = jnp.where((ci < (cj + 1) * SEL_BLOCK) & (ci + CMP_BLOCK > cj * SEL_BLOCK), 1.0, 0.0).astype(BF16)
    imp8 = jnp.dot(p_c, cover, preferred_element_type=F32)
    imp = jnp.where(grp0, jnp.sum(imp8[0:NSA_REP], axis=0, keepdims=True),
                    jnp.sum(imp8[NSA_REP:], axis=0, keepdims=True))
    lane_f = lane128.astype(F32)
    jq = past // SEL_BLOCK
    forced = (lane128 == 0) | (lane128 == jq) | (lane128 == jq - 1)
    work = jnp.where(lane128 <= jq, imp + jnp.where(forced, SEL_FORCE, 0.0), NOT_ALLOWED)
    sel = jnp.zeros((NSA_HEADS, LANES), F32)
    for _ in range(min(N_SELECT, jq + 1)):
        mx = jnp.max(work, axis=-1, keepdims=True)
        idx = jnp.min(jnp.where(work == mx, lane_f, float(LANES)), axis=-1, keepdims=True)
        hit = lane_f == idx
        sel = jnp.where(hit & (mx > 0.5 * NOT_ALLOWED), 1.0, sel)
        work = jnp.where(hit, REMOVED, work)
    sel_bf = sel.astype(BF16)

    parts, oks = [], []
    for pg in range(n_pages):
        parts.append(qk(pages[pg][0, 0, 2, 0], pages[pg][0, 0, 2, 1]) + bs_ref[:, pg * page:(pg + 1) * page])
        oks.append(jnp.dot(sel_bf, ex_ref[:, pg * page:(pg + 1) * page], preferred_element_type=F32) > 0.5)
    new_rows = new_ref[0].reshape(8 * g2, HEAD_DIM).astype(BF16)
    mnk, mnv = _wg_mask(8 * g2, 8, 2), _wg_mask(8 * g2, 8, 3)
    parts.append(jnp.sum(jnp.where(mnk, lax.dot_general(q, new_rows, nt, preferred_element_type=F32), 0.0),
                         axis=1, keepdims=True) + b0)
    oks.append(sel[:, jq:jq + 1] > 0.5)
    es, den = _masked_softmax_rows(parts, oks)
    o_s = jnp.dot(jnp.where(mnv, es[n_pages], 0.0).astype(BF16), new_rows, preferred_element_type=F32)
    for pg in range(n_pages):
        o_s = o_s + pv(es[pg], pages[pg][0, 0, 3, 0], pages[pg][0, 0, 3, 1])
    o_ref[0, 1] = o_s / den

    wb = win_ref.shape[-1]
    mnk, mnv = _wg_mask(8 * g2, 8, 4), _wg_mask(8 * g2, 8, 5)
    lane_w = lax.broadcasted_iota(jnp.int32, (NSA_HEADS, wb), 1)
    parts = [qk(win_ref[0, 0, 0, 0], win_ref[0, 0, 0, 1]) + bw_ref[...],
             jnp.sum(jnp.where(mnk, lax.dot_general(q, new_rows, nt, preferred_element_type=F32), 0.0),
                     axis=1, keepdims=True) + b0]
    oks = [wb - lane_w < WINDOW, jnp.full((NSA_HEADS, 1), True)]
    es, den = _masked_softmax_rows(parts, oks)
    o_w = (pv(es[0], win_ref[0, 0, 1, 0], win_ref[0, 0, 1, 1])
           + jnp.dot(jnp.where(mnv, es[1], 0.0).astype(BF16), new_rows, preferred_element_type=F32))
    o_ref[0, 2] = o_w / den

    wrows = win_ref[0, 0].reshape(2 * g2 * HEAD_DIM, wb)
    last = lax.broadcasted_iota(jnp.int32, (1, wb), 1) == wb - 1
    wout_ref[0, 0] = jnp.where(last, newcol_ref[0], pltpu.roll(wrows, wb - 1, axis=1)).reshape(2, g2, HEAD_DIM, wb)


def nsa_decode(pt, layer, q8, new_rows, kwv_new, win_all, cache, cmp_w, cmp_b, rel_bias):
    b, n_pages = pt.shape
    page = cache.shape[2]
    past = n_pages * page
    wb = win_all.shape[2]
    ncr = past // CMP_STRIDE
    cache_t = jnp.transpose(cache, (0, 1, 3, 4, 5, 2))
    win_t = jnp.transpose(win_all, (0, 1, 3, 4, 5, 2))
    tab = lambda dist: rel_bias[_t5_bucket_host(dist)].T.astype(F32)
    bc = tab(past - (jnp.arange(ncr) * CMP_STRIDE + CMP_BLOCK - 1))
    bs = tab(past - jnp.arange(past))
    bw = tab(wb - jnp.arange(wb))
    b0 = jnp.broadcast_to(rel_bias[0][:, None], (NSA_HEADS, LANES)).astype(F32)
    expand = (jnp.arange(LANES)[:, None] == jnp.arange(past)[None, :] // SEL_BLOCK).astype(BF16)
    w5 = cmp_w.reshape(2, 2, CMP_STRIDE, HEAD_DIM, HEAD_DIM)
    wcat = jnp.einsum('whjde,gk->wjgdhke', w5, jnp.eye(NSA_KV_HEADS, dtype=F32))
    wcat = wcat.reshape(2, CMP_STRIDE, LANES, 2 * LANES).astype(BF16)
    cb = jnp.tile(cmp_b.astype(F32), (1, NSA_KV_HEADS))
    consts = (wcat, cb, bc, bs, bw, b0, expand)
    per_seq = lambda *shape: pl.BlockSpec((1,) + shape, lambda i, pt: (i,) + (0,) * len(shape))
    full = lambda a: pl.BlockSpec(a.shape, lambda i, pt: (0,) * a.ndim)
    wblk = pl.BlockSpec((1, 1, 2, NSA_KV_HEADS, HEAD_DIM, wb), lambda i, pt: (layer, i, 0, 0, 0, 0))
    in_specs = ([per_seq(NSA_HEADS, HEAD_DIM), per_seq(8, NSA_KV_HEADS, HEAD_DIM), per_seq(256, 1), wblk]
                + [full(a) for a in consts]
                + [pl.BlockSpec((1, 1, 4, NSA_KV_HEADS, HEAD_DIM, page),
                                lambda i, pt, j=j: (layer, pt[i, j], 0, 0, 0, 0)) for j in range(n_pages)])
    grid_spec = pltpu.PrefetchScalarGridSpec(
        num_scalar_prefetch=1, grid=(b,), in_specs=in_specs,
        out_specs=[per_seq(3, NSA_HEADS, HEAD_DIM),
                   pl.BlockSpec((1, 1, 2, NSA_KV_HEADS, HEAD_DIM, wb), lambda i, pt: (i, 0, 0, 0, 0, 0))],
        scratch_shapes=[pltpu.VMEM((2, past, LANES), F32)])
    o3, wout = pl.pallas_call(
        functools.partial(_nsa_decode_kernel, n_pages=n_pages, page=page),
        grid_spec=grid_spec,
        out_shape=[jax.ShapeDtypeStruct((b, 3, NSA_HEADS, HEAD_DIM), F32),
                   jax.ShapeDtypeStruct((b, 1, 2, NSA_KV_HEADS, HEAD_DIM, wb), F32)],
        compiler_params=_cparams(("parallel",)),
        name="nsa_decode",
    )(pt, q8, new_rows, kwv_new.reshape(b, 256, 1), win_t, *consts, *([cache_t] * n_pages))
    return o3, jnp.transpose(wout[:, 0], (0, 4, 1, 2, 3))


def kernel(x_prompt, x_sample, cache_nsa_kv, cache_fox_k, cache_fox_v, cache_fox_logf, state_nsa_win, state_rwkv_wkv, state_rwkv_shift, page_table, ln_g, ln_b, ffn_w_gate, ffn_w_up, ffn_w_down, w_in, w_out, rw_mu, rw_w0, rw_w2, rw_a0, rw_a2, rw_g2, rw_k_k, rw_k_a, rw_r_k, rw_gn_g, rw_gn_b, nsa_cmp_w, nsa_cmp_b, nsa_out_g, rel_bias, fox_b_f, fox_out_g):
    db = x_sample.shape[0]
    t = x_prompt.shape[1]

    wg_bf = ffn_w_gate.astype(BF16)
    wu_bf = ffn_w_up.astype(BF16)
    wd_bf = ffn_w_down.astype(BF16)
    wout_bf = w_out.astype(BF16)
    w_rw = w_in[:, :, :RW_COLS].astype(BF16)
    w_nsa = [nsa_proj_weight(w_in[l, :, RW_COLS:RW_COLS + NSA_COLS]) for l in range(DEPTH)]
    w_fox = jnp.pad(w_in[:, :, RW_COLS + NSA_COLS:].astype(BF16), ((0, 0), (0, 0), (0, 3200 - FOX_COLS)))

    def ffn_sub(x, x_bf, l, j):
        h = ffn_in(x_bf, wg_bf[l, j], wu_bf[l, j])
        return ffn_out(h, wd_bf[l, j], x, ln_g[l, 2 * j], ln_b[l, 2 * j])

    def rw_params(l):
        return (rw_mu[l], rw_w0[l], rw_w2[l], rw_a0[l], rw_a2[l], rw_g2[l], rw_k_k[l], rw_k_a[l],
                rw_r_k[l], rw_gn_g[l], rw_gn_b[l])

    xp = x_prompt[0]
    xp_bf = xp.astype(BF16)
    p_kv, p_fk, p_fv, p_fl, p_win, p_wkv, p_shift = [], [], [], [], [], [], []
    btab = bias_tiles(rel_bias, NSA_TILE)
    wb = min(WINDOW, t)
    for l in range(DEPTH):
        xp, xp_bf = ffn_sub(xp, xp_bf, l, 0)
        p_rw = proj(xp_bf, w_rw[l])
        nq, kv4, kwv, gl, dup = nsa_proj(xp_bf, w_nsa[l])
        qbf, fk, fv, kbf, vbf, lf128 = fox_proj(xp_bf, w_fox[l], fox_b_f[l])
        o_rw, s_wkv = rwkv_prompt(p_rw, t, rw_params(l))
        wmat, cbias = nsa_compress_weight(nsa_cmp_w[l], nsa_cmp_b[l])
        o_c, o_s, o_w = nsa_prompt_attn(nq, kv4, dup, wmat, cbias, rel_bias, btab, t)
        o_fox = fox_prompt_attn(qbf, kbf, vbf, lf128, t)
        xp, xp_bf = mix_out(o_rw, o_c, o_s, o_w, gl, o_fox, xp, wout_bf[l], nsa_out_g[l], fox_out_g[l],
                            ln_g[l, 1], ln_b[l, 1])
        xp, xp_bf = ffn_sub(xp, xp_bf, l, 1)
        p_kv.append(kv4.reshape(1, t, 4, NSA_KV_HEADS, HEAD_DIM))
        p_fk.append(fk.reshape(1, t, FOX_HEADS, HEAD_DIM))
        p_fv.append(fv.reshape(1, t, FOX_HEADS, HEAD_DIM))
        p_fl.append(lf128[None, :, :FOX_HEADS])
        p_win.append(kwv[t - wb:].reshape(1, wb, 2, NSA_KV_HEADS, HEAD_DIM))
        p_wkv.append(s_wkv[None])
        p_shift.append(p_rw[t - 1:t])

    xs = x_sample[:, 0]
    xs_bf = xs.astype(BF16)
    s_kv, s_fk, s_fv, s_fl, s_win, s_wkv_l, s_shift_l = [], [], [], [], [], [], []
    for l in range(DEPTH):
        xs, xs_bf = ffn_sub(xs, xs_bf, l, 0)
        p_rw = proj(xs_bf, w_rw[l])
        nq, kv_new, kwv_new, gl, _ = nsa_proj(xs_bf, w_nsa[l])
        qbf, fk, fv, kbf, vbf, lf128 = fox_proj(xs_bf, w_fox[l], fox_b_f[l])
        o_rw, s_wkv = rwkv_sample(p_rw, state_rwkv_shift[l], state_rwkv_wkv[l], rw_params(l))
        new_rows = jnp.concatenate([kv_new.reshape(db, 4, NSA_KV_HEADS, HEAD_DIM),
                                    kwv_new.reshape(db, 2, NSA_KV_HEADS, HEAD_DIM),
                                    jnp.zeros((db, 2, NSA_KV_HEADS, HEAD_DIM), F32)], axis=1)
        o3, win = nsa_decode(page_table, l, nq.reshape(db, NSA_HEADS, HEAD_DIM), new_rows, kwv_new, state_nsa_win,
                             cache_nsa_kv, nsa_cmp_w[l], nsa_cmp_b[l], rel_bias)
        o_c, o_s, o_w = (o3[:, br].reshape(db, NSA_WIDTH) for br in range(3))
        o_fox = fox_paged(page_table, l, qbf, fk, fv, lf128, cache_fox_k, cache_fox_v, cache_fox_logf)
        xs, xs_bf = mix_out(o_rw, o_c, o_s, o_w, gl, o_fox, xs, wout_bf[l],
                            nsa_out_g[l], fox_out_g[l], ln_g[l, 1], ln_b[l, 1])
        xs, xs_bf = ffn_sub(xs, xs_bf, l, 1)
        s_kv.append(kv_new.reshape(db, 1, 4, NSA_KV_HEADS, HEAD_DIM))
        s_fk.append(fk.reshape(db, 1, FOX_HEADS, HEAD_DIM))
        s_fv.append(fv.reshape(db, 1, FOX_HEADS, HEAD_DIM))
        s_fl.append(lf128[:, None, :FOX_HEADS])
        s_win.append(win)
        s_wkv_l.append(s_wkv)
        s_shift_l.append(p_rw)

    return (xp[None], xs[:, None],
            jnp.stack(p_kv), jnp.stack(p_fk), jnp.stack(p_fv), jnp.stack(p_fl),
            jnp.stack(p_win), jnp.stack(p_wkv), jnp.stack(p_shift),
            jnp.stack(s_kv), jnp.stack(s_fk), jnp.stack(s_fv), jnp.stack(s_fl),
            jnp.stack(s_win), jnp.stack(s_wkv_l), jnp.stack(s_shift_l))
```

```python
import functools
import math

import jax
import jax.numpy as jnp
from jax import lax
from jax.experimental import pallas as pl
from jax.experimental.pallas import tpu as pltpu

F32 = jnp.float32
BF16 = jnp.bfloat16

D_MODEL = 2048
DEPTH = 2
HEAD_DIM = 64
RW_HEADS = 8
RW_WIDTH = 512
DECAY_LORA = 64
AAA_LORA = 64
GATE_LORA = 128
RW_COLS = 3 * RW_WIDTH + DECAY_LORA + AAA_LORA + GATE_LORA
RW_SPLITS = [RW_WIDTH, 2 * RW_WIDTH, 3 * RW_WIDTH, 3 * RW_WIDTH + DECAY_LORA, 3 * RW_WIDTH + DECAY_LORA + AAA_LORA]
RW_GN_EPS = 64e-5
NSA_HEADS = 8
NSA_KV_HEADS = 2
NSA_REP = NSA_HEADS // NSA_KV_HEADS
NSA_WIDTH = 512
NSA_KV_WIDTH = 128
NSA_COLS = NSA_WIDTH + 6 * NSA_KV_WIDTH + 3 * NSA_HEADS
NSA_SPLITS = [NSA_WIDTH + i * NSA_KV_WIDTH for i in range(7)]
CMP_BLOCK = 32
CMP_STRIDE = 16
SEL_BLOCK = 64
N_SELECT = 16
WINDOW = 512
SEL_FORCE = 1e4
FOX_HEADS = 16
FOX_WIDTH = 1024
FOX_COLS = 3 * FOX_WIDTH + FOX_HEADS
FOX_SPLITS = [FOX_WIDTH, 2 * FOX_WIDTH, 3 * FOX_WIDTH]
IN_SPLITS = [RW_COLS, RW_COLS + NSA_COLS]
N_BUCKETS = 32
MAX_DISTANCE = 1024
Q_BLOCK = 128
DEEPNORM_ALPHA = (2 * DEPTH) ** 0.25
LN_EPS = 1e-5
RMS_EPS = 1e-6
ATT_SCALE = HEAD_DIM ** -0.5

LANES = 128
VMEM_LIMIT = 56 * 1024 * 1024
NEG = -1e30


def _cparams(sem):
    return pltpu.CompilerParams(dimension_semantics=sem, vmem_limit_bytes=VMEM_LIMIT)


def _pad_cols(w, n):
    return jnp.pad(w, ((0, 0), (0, n - w.shape[1])))


def _ffn_in_kernel(x_ref, wg_ref, wu_ref, h_ref):
    x = x_ref[...]
    g = jnp.dot(x, wg_ref[...], preferred_element_type=F32)
    u = jnp.dot(x, wu_ref[...], preferred_element_type=F32)
    h_ref[...] = (g * (1.0 / (1.0 + jnp.exp(-g))) * u).astype(BF16)


def ffn_in(x_bf, wg, wu):
    m, d = x_bf.shape
    f = wg.shape[1]
    tm = min(m, 1024)
    tn = 512
    return pl.pallas_call(
        _ffn_in_kernel,
        grid=(m // tm, f // tn),
        in_specs=[pl.BlockSpec((tm, d), lambda i, j: (i, 0)),
                  pl.BlockSpec((d, tn), lambda i, j: (0, j)),
                  pl.BlockSpec((d, tn), lambda i, j: (0, j))],
        out_specs=pl.BlockSpec((tm, tn), lambda i, j: (i, j)),
        out_shape=jax.ShapeDtypeStruct((m, f), BF16),
        compiler_params=_cparams(("parallel", "arbitrary")),
        name="ffn_in",
    )(x_bf, wg, wu)


def _layer_norm(y, g, b):
    mu = jnp.mean(y, -1, keepdims=True)
    d = y - mu
    var = jnp.mean(d * d, -1, keepdims=True)
    return d * lax.rsqrt(var + LN_EPS) * g + b


def _ffn_out_kernel(h_ref, wd_ref, x_ref, g_ref, b_ref, y_ref, ybf_ref, acc_ref):
    k = pl.program_id(1)

    @pl.when(k == 0)
    def _():
        acc_ref[...] = jnp.zeros_like(acc_ref)

    acc_ref[...] += jnp.dot(h_ref[...], wd_ref[...], preferred_element_type=F32)

    @pl.when(k == pl.num_programs(1) - 1)
    def _():
        y = _layer_norm(DEEPNORM_ALPHA * x_ref[...] + 0.5 * acc_ref[...], g_ref[...], b_ref[...])
        y_ref[...] = y
        ybf_ref[...] = y.astype(BF16)


def ffn_out(h_bf, wd, x, g, b):
    m, f = h_bf.shape
    d = wd.shape[1]
    tm = min(m, 512)
    tk = 512
    return pl.pallas_call(
        _ffn_out_kernel,
        grid=(m // tm, f // tk),
        in_specs=[pl.BlockSpec((tm, tk), lambda i, k: (i, k)),
                  pl.BlockSpec((tk, d), lambda i, k: (k, 0)),
                  pl.BlockSpec((tm, d), lambda i, k: (i, 0)),
                  pl.BlockSpec((1, d), lambda i, k: (0, 0)),
                  pl.BlockSpec((1, d), lambda i, k: (0, 0))],
        out_specs=[pl.BlockSpec((tm, d), lambda i, k: (i, 0)),
                   pl.BlockSpec((tm, d), lambda i, k: (i, 0))],
        out_shape=[jax.ShapeDtypeStruct((m, d), F32), jax.ShapeDtypeStruct((m, d), BF16)],
        scratch_shapes=[pltpu.VMEM((tm, d), F32)],
        compiler_params=_cparams(("parallel", "arbitrary")),
        name="ffn_out",
    )(h_bf, wd, x, g.reshape(1, d), b.reshape(1, d))


def _proj_kernel(x_ref, w_ref, o_ref):
    o_ref[...] = jnp.dot(x_ref[...], w_ref[...], preferred_element_type=F32)


def proj(x_bf, w_bf):
    m, d = x_bf.shape
    n = w_bf.shape[1]
    tm = min(m, 512)
    tn = 128
    for c in (640, 512, 384, 256):
        if n % c == 0:
            tn = c
            break
    return pl.pallas_call(
        _proj_kernel,
        grid=(m // tm, n // tn),
        in_specs=[pl.BlockSpec((tm, d), lambda i, j: (i, 0)),
                  pl.BlockSpec((d, tn), lambda i, j: (0, j))],
        out_specs=pl.BlockSpec((tm, tn), lambda i, j: (i, j)),
        out_shape=jax.ShapeDtypeStruct((m, n), F32),
        compiler_params=_cparams(("parallel", "arbitrary")),
        name="proj",
    )(x_bf, w_bf)


def _log_sigmoid(x):
    return jnp.minimum(x, 0.0) - jnp.log1p(jnp.exp(-jnp.abs(x)))


def _fox_proj_kernel(x_ref, w_ref, bf_ref, qbf_ref, k_ref, v_ref, kbf_ref, vbf_ref, lf_ref):
    x = x_ref[...]
    q = jnp.dot(x, w_ref[:, 0:FOX_WIDTH], preferred_element_type=F32)
    qbf_ref[...] = (q * ATT_SCALE).astype(BF16)
    k = jnp.dot(x, w_ref[:, FOX_WIDTH:2 * FOX_WIDTH], preferred_element_type=F32)
    k_ref[...] = k
    kbf_ref[...] = k.astype(BF16)
    v = jnp.dot(x, w_ref[:, 2 * FOX_WIDTH:3 * FOX_WIDTH], preferred_element_type=F32)
    v_ref[...] = v
    vbf_ref[...] = v.astype(BF16)
    f = jnp.dot(x, w_ref[:, 3 * FOX_WIDTH:3 * FOX_WIDTH + LANES], preferred_element_type=F32)
    lf_ref[...] = _log_sigmoid(f + bf_ref[...])


def fox_proj(x_bf, w_bf, b_f):
    m, d = x_bf.shape
    n = w_bf.shape[1]
    tm = min(m, 256)
    row = lambda width: pl.BlockSpec((tm, width), lambda i: (i, 0))
    bias = jnp.pad(b_f.astype(F32), (0, LANES - FOX_HEADS)).reshape(1, LANES)
    return pl.pallas_call(
        _fox_proj_kernel,
        grid=(m // tm,),
        in_specs=[row(d), pl.BlockSpec((d, n), lambda i: (0, 0)), pl.BlockSpec((1, LANES), lambda i: (0, 0))],
        out_specs=[row(FOX_WIDTH)] * 5 + [row(LANES)],
        out_shape=[jax.ShapeDtypeStruct((m, FOX_WIDTH), BF16),
                   jax.ShapeDtypeStruct((m, FOX_WIDTH), F32), jax.ShapeDtypeStruct((m, FOX_WIDTH), F32),
                   jax.ShapeDtypeStruct((m, FOX_WIDTH), BF16), jax.ShapeDtypeStruct((m, FOX_WIDTH), BF16),
                   jax.ShapeDtypeStruct((m, LANES), F32)],
        compiler_params=_cparams(("parallel",)),
        name="fox_proj",
    )(x_bf, w_bf, bias)


def _split3(x):
    hi = x.astype(BF16)
    r = x - hi.astype(F32)
    mid = r.astype(BF16)
    lo = (r - mid.astype(F32)).astype(BF16)
    return hi, mid, lo


def _seg_sum(x, ones_blk):
    hi, mid, lo = _split3(x)
    d = lambda a: jnp.dot(a, ones_blk, preferred_element_type=F32)
    return d(hi) + d(mid) + d(lo)


def _head_ones():
    r = lax.broadcasted_iota(jnp.int32, (LANES, LANES), 0) // HEAD_DIM
    c = lax.broadcasted_iota(jnp.int32, (LANES, LANES), 1) // HEAD_DIM
    return (r == c).astype(BF16)


def _mix_out_kernel(orw_ref, oc_ref, os_ref, ow_ref, gl_ref, ofox_ref, x_ref, w_ref, gn_ref, gf_ref, g_ref, b_ref,
                    y_ref, ybf_ref):
    ones_blk = _head_ones()
    gate = 1.0 / (1.0 + jnp.exp(-gl_ref[...]))
    pieces = _split3(gate)
    er = lax.broadcasted_iota(jnp.int32, (LANES, NSA_WIDTH), 0)
    ec = lax.broadcasted_iota(jnp.int32, (LANES, NSA_WIDTH), 1) // HEAD_DIM
    onsa = jnp.zeros(oc_ref.shape, F32)
    for c, br_ref in enumerate((oc_ref, os_ref, ow_ref)):
        spread = (er == ec * 3 + c).astype(BF16)
        gate_c = sum(jnp.dot(piece, spread, preferred_element_type=F32) for piece in pieces)
        onsa = onsa + gate_c * br_ref[...]

    def rms(o, gain):
        cols = []
        for c in range(o.shape[1] // LANES):
            blk = o[:, c * LANES:(c + 1) * LANES]
            ms = _seg_sum(blk * blk, ones_blk) * (1.0 / HEAD_DIM)
            cols.append(blk * lax.rsqrt(ms + RMS_EPS) * gain[:, c * LANES:(c + 1) * LANES])
        return jnp.concatenate(cols, axis=1)

    o = jnp.concatenate([orw_ref[...], rms(onsa, gn_ref[...]), rms(ofox_ref[...], gf_ref[...])], axis=1)
    acc = jnp.dot(o.astype(BF16), w_ref[...], preferred_element_type=F32)
    y = _layer_norm(DEEPNORM_ALPHA * x_ref[...] + acc, g_ref[...], b_ref[...])
    y_ref[...] = y
    ybf_ref[...] = y.astype(BF16)


def mix_out(o_rw, o_c, o_s, o_w, gl, o_fox, x, w_bf, gn, gf, g, b):
    m, d = x.shape
    tm = min(m, 256)
    row = lambda width: pl.BlockSpec((tm, width), lambda i: (i, 0))
    full = lambda a, c: pl.BlockSpec((a, c), lambda i: (0, 0))
    return pl.pallas_call(
        _mix_out_kernel,
        grid=(m // tm,),
        in_specs=[row(RW_WIDTH), row(NSA_WIDTH), row(NSA_WIDTH), row(NSA_WIDTH), row(LANES), row(FOX_WIDTH),
                  row(d), full(d, d), full(1, NSA_WIDTH), full(1, FOX_WIDTH), full(1, d), full(1, d)],
        out_specs=[row(d), row(d)],
        out_shape=[jax.ShapeDtypeStruct((m, d), F32), jax.ShapeDtypeStruct((m, d), BF16)],
        compiler_params=_cparams(("parallel",)),
        name="mix_out",
    )(o_rw, o_c, o_s, o_w, gl, o_fox, x, w_bf, gn.reshape(1, -1), gf.reshape(1, -1), g.reshape(1, d),
      b.reshape(1, d))


CUM_BLOCK = 512


def _cumsum_kernel(x_ref, o_ref, carry_ref):
    @pl.when(pl.program_id(0) == 0)
    def _():
        carry_ref[...] = jnp.zeros_like(carry_ref)

    r = lax.broadcasted_iota(jnp.int32, (CUM_BLOCK, CUM_BLOCK), 0)
    c = lax.broadcasted_iota(jnp.int32, (CUM_BLOCK, CUM_BLOCK), 1)
    upper = (r <= c).astype(BF16)
    hi, mid, lo = _split3(x_ref[...])
    d = lambda a: jnp.dot(a, upper, preferred_element_type=F32)
    cs = d(hi) + d(mid) + d(lo) + carry_ref[:, 0:1]
    o_ref[...] = cs
    carry_ref[...] = jnp.broadcast_to(cs[:, CUM_BLOCK - 1:CUM_BLOCK], carry_ref.shape)


def cumsum_lanes(x):
    rows, t = x.shape
    return pl.pallas_call(
        _cumsum_kernel,
        grid=(t // CUM_BLOCK,),
        in_specs=[pl.BlockSpec((rows, CUM_BLOCK), lambda i: (0, i))],
        out_specs=pl.BlockSpec((rows, CUM_BLOCK), lambda i: (0, i)),
        out_shape=jax.ShapeDtypeStruct((rows, t), F32),
        scratch_shapes=[pltpu.VMEM((rows, LANES), F32)],
        compiler_params=_cparams(("arbitrary",)),
        name="cumsum",
    )(x)


def _fox_flash_kernel(qt_ref, kt_ref, q_ref, k_ref, v_ref, ck_ref, o_ref, m_ref, l_ref, acc_ref, *, tq, tk):
    step = pl.program_id(1)
    qi = qt_ref[step]
    ki = kt_ref[step]

    @pl.when(ki == 0)
    def _():
        m_ref[...] = jnp.full_like(m_ref, NEG)
        l_ref[...] = jnp.zeros_like(l_ref)
        acc_ref[...] = jnp.zeros_like(acc_ref)

    left = lax.broadcasted_iota(jnp.int32, (1, LANES), 1) < HEAD_DIM

    def tile(diagonal):
        q = q_ref[...]
        k = k_ref[...]
        v = v_ref[...]
        pv, alphas = [], []
        for hh in range(2):
            qm = jnp.where(left if hh == 0 else jnp.logical_not(left), q, jnp.zeros_like(q))
            s = lax.dot_general(qm, k, (((1,), (1,)), ((), ())), preferred_element_type=F32)
            s = s - ck_ref[0, hh:hh + 1, :]
            if diagonal:
                rows = lax.broadcasted_iota(jnp.int32, (tq, tk), 0)
                cols = lax.broadcasted_iota(jnp.int32, (tq, tk), 1)
                s = jnp.where(rows >= cols, s, NEG)
            m_old = m_ref[hh]
            m_new = jnp.maximum(m_old, jnp.max(s, axis=-1, keepdims=True))
            p = jnp.exp(s - m_new)
            alpha = jnp.exp(m_old - m_new)
            l_ref[hh] = alpha * l_ref[hh] + jnp.sum(p, axis=-1, keepdims=True)
            m_ref[hh] = m_new
            pv.append(jnp.dot(p.astype(BF16), v, preferred_element_type=F32))
            alphas.append(alpha)
        acc_ref[...] = (jnp.where(left, alphas[0], alphas[1]) * acc_ref[...]
                        + jnp.where(left, pv[0], pv[1]))

    @pl.when(ki < qi)
    def _():
        tile(False)

    @pl.when(ki == qi)
    def _():
        tile(True)
        o_ref[...] = acc_ref[...] / jnp.where(left, l_ref[0], l_ref[1])


def _tri_steps(n):
    qs, ks = [], []
    for qi in range(n):
        for ki in range(qi + 1):
            qs.append(qi)
            ks.append(ki)
    return jnp.asarray(qs, jnp.int32), jnp.asarray(ks, jnp.int32)


def fox_flash(q_bf, k_bf, v_bf, ck, t):
    tq = tk = min(t, 1024)
    n = t // tq
    qt, kt = _tri_steps(n)
    pairs = FOX_WIDTH // LANES
    grid_spec = pltpu.PrefetchScalarGridSpec(
        num_scalar_prefetch=2,
        grid=(pairs, qt.shape[0]),
        in_specs=[pl.BlockSpec((tq, LANES), lambda p, s, qt, kt: (qt[s], p)),
                  pl.BlockSpec((tk, LANES), lambda p, s, qt, kt: (kt[s], p)),
                  pl.BlockSpec((tk, LANES), lambda p, s, qt, kt: (kt[s], p)),
                  pl.BlockSpec((1, 8, tk), lambda p, s, qt, kt: (p, 0, kt[s]))],
        out_specs=pl.BlockSpec((tq, LANES), lambda p, s, qt, kt: (qt[s], p)),
        scratch_shapes=[pltpu.VMEM((2, tq, 1), F32), pltpu.VMEM((2, tq, 1), F32), pltpu.VMEM((tq, LANES), F32)],
    )
    return pl.pallas_call(
        functools.partial(_fox_flash_kernel, tq=tq, tk=tk),
        grid_spec=grid_spec,
        out_shape=jax.ShapeDtypeStruct((t, FOX_WIDTH), F32),
        compiler_params=_cparams(("parallel", "arbitrary")),
        name="fox_flash",
    )(qt, kt, q_bf, k_bf, v_bf, ck)


def fox_prompt_attn(qbf, kbf, vbf, logf128, t):
    lf = logf128[:t, :FOX_HEADS]
    lf_t = jnp.pad(lf.T.reshape(FOX_HEADS // 2, 2, t), ((0, 0), (0, 6), (0, 0))).reshape(-1, t)
    ck = cumsum_lanes(lf_t).reshape(FOX_HEADS // 2, 8, t)
    return fox_flash(qbf, kbf, vbf, ck, t)


T5_THRESH = (21, 27, 35, 46, 59, 77, 99, 128, 166, 216, 280, 363, 470, 609, 790)
NSA_TILE = 512
NSA_DELTAS = 4
NOT_ALLOWED = -1e30
REMOVED = -2e30


def _t5_bucket_int(dist):
    d = jnp.maximum(dist, 0)
    large = jnp.full(d.shape, N_BUCKETS // 2, jnp.int32)
    for th in T5_THRESH:
        large = large + (d >= th).astype(jnp.int32)
    return jnp.where(d < N_BUCKETS // 2, d, large)


def _bias_lookup(bucket, tab_ref, h):
    val = jnp.full(bucket.shape, tab_ref[0, h], F32)
    for b in range(1, N_BUCKETS):
        val = jnp.where(bucket == b, tab_ref[b, h], val)
    return val


def _bias_tab_kernel(tab_ref, o_ref, *, tb):
    h = pl.program_id(0)
    dl = pl.program_id(1)
    r = lax.broadcasted_iota(jnp.int32, (tb, tb), 0)
    c = lax.broadcasted_iota(jnp.int32, (tb, tb), 1)
    o_ref[0, 0] = _bias_lookup(_t5_bucket_int(dl * tb + r - c), tab_ref, h)


def bias_tiles(rel_bias, tb):
    return pl.pallas_call(
        functools.partial(_bias_tab_kernel, tb=tb),
        grid=(NSA_HEADS, NSA_DELTAS),
        in_specs=[pl.BlockSpec(memory_space=pltpu.SMEM)],
        out_specs=pl.BlockSpec((1, 1, tb, tb), lambda h, d: (h, d, 0, 0)),
        out_shape=jax.ShapeDtypeStruct((NSA_HEADS, NSA_DELTAS, tb, tb), F32),
        compiler_params=_cparams(("parallel", "parallel")),
        name="nsa_bias_tiles",
    )(rel_bias)


NSA_PROJ_COLS = 2432


def nsa_proj_weight(w):
    q, kc, vc, ks, vs, kw, vw, gl = jnp.split(w, NSA_SPLITS, axis=-1)
    dup = []
    for src in (ks, vs, kw, vw):
        for g in range(NSA_KV_HEADS):
            blk = src[:, g * HEAD_DIM:(g + 1) * HEAD_DIM]
            dup += [blk, blk]
    return jnp.concatenate([q, kc, vc, ks, vs, kw, vw, _pad_cols(gl, LANES)] + dup, axis=1).astype(BF16)


def _nsa_proj_kernel(x_ref, w_ref, qbf_ref, kv_ref, kwv_ref, gl_ref, dup_ref):
    x = x_ref[...]
    d = lambda a, b: jnp.dot(x, w_ref[:, a:b], preferred_element_type=F32)
    qbf_ref[...] = (d(0, 512) * ATT_SCALE).astype(BF16)
    kv_ref[...] = d(512, 1024)
    kwv_ref[...] = d(1024, 1280)
    gl_ref[...] = d(1280, 1408)
    dup_ref[...] = d(1408, 2432).astype(BF16)


def nsa_proj(x_bf, w_bf):
    m, dm = x_bf.shape
    tm = min(m, 256)
    row = lambda width: pl.BlockSpec((tm, width), lambda i: (i, 0))
    return pl.pallas_call(
        _nsa_proj_kernel,
        grid=(m // tm,),
        in_specs=[row(dm), pl.BlockSpec((dm, NSA_PROJ_COLS), lambda i: (0, 0))],
        out_specs=[row(512), row(512), row(256), row(LANES), row(1024)],
        out_shape=[jax.ShapeDtypeStruct((m, 512), BF16), jax.ShapeDtypeStruct((m, 512), F32),
                   jax.ShapeDtypeStruct((m, 256), F32), jax.ShapeDtypeStruct((m, LANES), F32),
                   jax.ShapeDtypeStruct((m, 1024), BF16)],
        compiler_params=_cparams(("parallel",)),
        name="nsa_proj",
    )(x_bf, w_bf)


def nsa_compress_weight(cmp_w, cmp_b):
    w = cmp_w.reshape(2, 2, CMP_STRIDE, HEAD_DIM, HEAD_DIM)
    eye = jnp.eye(2, dtype=F32)
    t = jnp.einsum('whjde,wv,gk->jwgdhvke', w, eye, eye)
    t = jnp.broadcast_to(t[..., None, :], t.shape[:-1] + (2, HEAD_DIM))
    wmat = t.reshape(CMP_STRIDE * 256, 1024).astype(BF16)
    bias = jnp.broadcast_to(cmp_b[:, None, None, :], (2, NSA_KV_HEADS, 2, HEAD_DIM)).reshape(1, 512)
    return wmat, bias.astype(F32)


def _nsa_compress_kernel(x_ref, w_ref, b_ref, o_ref, acc_ref):
    j = pl.program_id(0)

    @pl.when(j == 0)
    def _():
        acc_ref[...] = jnp.zeros_like(acc_ref)

    acc_ref[...] += jnp.dot(x_ref[...].astype(BF16), w_ref[...], preferred_element_type=F32)

    @pl.when(j == pl.num_programs(0) - 1)
    def _():
        n = acc_ref.shape[0]
        first = acc_ref[:, 0:512]
        second = pltpu.roll(acc_ref[:, 512:1024], n - 1, axis=0)
        o_ref[...] = (first + second + b_ref[...]).astype(BF16)


def nsa_compress_rows(kv4, wmat, bias):
    t = kv4.shape[0]
    n = t // CMP_STRIDE
    x = kv4.reshape(n, CMP_STRIDE * 512)
    return pl.pallas_call(
        _nsa_compress_kernel,
        grid=(CMP_STRIDE,),
        in_specs=[pl.BlockSpec((n, 256), lambda j: (0, 2 * j)),
                  pl.BlockSpec((256, 1024), lambda j: (j, 0)),
                  pl.BlockSpec((1, 512), lambda j: (0, 0))],
        out_specs=pl.BlockSpec((n, 512), lambda j: (0, 0)),
        out_shape=jax.ShapeDtypeStruct((n, 512), BF16),
        scratch_shapes=[pltpu.VMEM((n, 1024), F32)],
        compiler_params=_cparams(("arbitrary",)),
        name="nsa_compress",
    )(x, wmat, bias)


def _nsa_cmp_kernel(tab_ref, q_ref, kv_ref, oc_ref, sel_ref, *, tq, ncp, n_sel):
    t0 = pl.program_id(0) * tq
    rows = lax.broadcasted_iota(jnp.int32, (tq, ncp), 0) + t0
    cols = lax.broadcasted_iota(jnp.int32, (tq, ncp), 1)
    dist = rows - (cols * CMP_STRIDE + (CMP_BLOCK - 1))
    valid = dist >= 0
    bucket = _t5_bucket_int(dist)
    left = lax.broadcasted_iota(jnp.int32, (1, LANES), 1) < HEAD_DIM
    ci = lax.broadcasted_iota(jnp.int32, (ncp, LANES), 0) * CMP_STRIDE
    cj = lax.broadcasted_iota(jnp.int32, (ncp, LANES), 1)
    cover = jnp.where((ci < (cj + 1) * SEL_BLOCK) & (ci + CMP_BLOCK > cj * SEL_BLOCK), 1.0, 0.0).astype(BF16)
    imp = [jnp.zeros((tq, LANES), F32) for _ in range(NSA_KV_HEADS)]
    for p in range(NSA_HEADS // 2):
        g = p // (NSA_REP // 2)
        qp = q_ref[:, p * LANES:(p + 1) * LANES]
        kc = kv_ref[:, g * LANES:(g + 1) * LANES]
        vc = kv_ref[:, (NSA_KV_HEADS + g) * LANES:(NSA_KV_HEADS + g + 1) * LANES]
        outs = []
        for hh in range(2):
            qm = jnp.where(left if hh == 0 else jnp.logical_not(left), qp, jnp.zeros_like(qp))
            s = lax.dot_general(qm, kc, (((1,), (1,)), ((), ())), preferred_element_type=F32)
            s = jnp.where(valid, s + _bias_lookup(bucket, tab_ref, 2 * p + hh), NEG)
            m = jnp.max(s, axis=-1, keepdims=True)
            e = jnp.where(valid, jnp.exp(s - m), 0.0)
            pr = (e / jnp.maximum(jnp.sum(e, axis=-1, keepdims=True), 1e-30)).astype(BF16)
            outs.append(jnp.dot(pr, vc, preferred_element_type=F32))
            imp[g] = imp[g] + jnp.dot(pr, cover, preferred_element_type=F32)
        oc_ref[:, p * LANES:(p + 1) * LANES] = jnp.where(left, outs[0], outs[1])

    lane = lax.broadcasted_iota(jnp.int32, (tq, LANES), 1)
    lane_f = lane.astype(F32)
    jq = (lax.broadcasted_iota(jnp.int32, (tq, LANES), 0) + t0) // SEL_BLOCK
    forced = (lane == 0) | (lane == jq) | (lane == jq - 1)
    allowed = lane <= jq
    for g in range(NSA_KV_HEADS):
        work = jnp.where(allowed, imp[g] + jnp.where(forced, SEL_FORCE, 0.0), NOT_ALLOWED)
        sel = jnp.zeros((tq, LANES), F32)
        for _ in range(n_sel):
            mx = jnp.max(work, axis=-1, keepdims=True)
            idx = jnp.min(jnp.where(work == mx, lane_f, float(LANES)), axis=-1, keepdims=True)
            hit = lane_f == idx
            sel = jnp.where(hit & (mx > 0.5 * NOT_ALLOWED), 1.0, sel)
            work = jnp.where(hit, REMOVED, work)
        sel_ref[:, g * LANES:(g + 1) * LANES] = sel.astype(BF16)


def nsa_cmp_select(qbf, kvc, rel_bias, t):
    tq = min(t, 256)
    ncp = kvc.shape[0]
    n_sel = min(N_SELECT, t // SEL_BLOCK)
    return pl.pallas_call(
        functools.partial(_nsa_cmp_kernel, tq=tq, ncp=ncp, n_sel=n_sel),
        grid=(t // tq,),
        in_specs=[pl.BlockSpec(memory_space=pltpu.SMEM),
                  pl.BlockSpec((tq, 512), lambda i: (i, 0)),
                  pl.BlockSpec((ncp, 512), lambda i: (0, 0))],
        out_specs=[pl.BlockSpec((tq, 512), lambda i: (i, 0)), pl.BlockSpec((tq, 256), lambda i: (i, 0))],
        out_shape=[jax.ShapeDtypeStruct((t, 512), F32), jax.ShapeDtypeStruct((t, 256), BF16)],
        compiler_params=_cparams(("parallel",)),
        name="nsa_cmp_select",
    )(rel_bias, qbf, kvc)


def _nsa_flash_kernel(qt_ref, kt_ref, ft_ref, q_ref, k_ref, v_ref, b_ref, *rest, tq, tk, selected):
    if selected:
        sm_ref, ex_ref, o_ref, m_ref, l_ref, acc_ref = rest
    else:
        o_ref, m_ref, l_ref, acc_ref = rest
    step = pl.program_id(1)
    qi = qt_ref[step]
    ki = kt_ref[step]

    @pl.when(ft_ref[step] == 1)
    def _():
        m_ref[...] = jnp.full_like(m_ref, NEG)
        l_ref[...] = jnp.zeros_like(l_ref)
        acc_ref[...] = jnp.zeros_like(acc_ref)

    left = lax.broadcasted_iota(jnp.int32, (1, LANES), 1) < HEAD_DIM

    def tile(diagonal):
        q = q_ref[...]
        k = k_ref[...]
        v = v_ref[...]
        rows = lax.broadcasted_iota(jnp.int32, (tq, tk), 0)
        cols = lax.broadcasted_iota(jnp.int32, (tq, tk), 1)
        if selected:
            valid = jnp.dot(sm_ref[...], ex_ref[...], preferred_element_type=F32) > 0.5
            if diagonal:
                valid = valid & (rows >= cols)
        else:
            valid = (rows >= cols) if diagonal else (rows < cols)
        pv, alphas = [], []
        for hh in range(2):
            qm = jnp.where(left if hh == 0 else jnp.logical_not(left), q, jnp.zeros_like(q))
            s = lax.dot_general(qm, k, (((1,), (1,)), ((), ())), preferred_element_type=F32)
            s = jnp.where(valid, s + b_ref[hh, 0], NEG)
            m_old = m_ref[hh]
            m_new = jnp.maximum(m_old, jnp.max(s, axis=-1, keepdims=True))
            p = jnp.where(valid, jnp.exp(s - m_new), 0.0)
            alpha = jnp.exp(m_old - m_new)
            l_ref[hh] = alpha * l_ref[hh] + jnp.sum(p, axis=-1, keepdims=True)
            m_ref[hh] = m_new
            pv.append(jnp.dot(p.astype(BF16), v, preferred_element_type=F32))
            alphas.append(alpha)
        acc_ref[...] = (jnp.where(left, alphas[0], alphas[1]) * acc_ref[...]
                        + jnp.where(left, pv[0], pv[1]))

    @pl.when(ki < qi)
    def _():
        tile(False)

    @pl.when(ki == qi)
    def _():
        tile(True)
        o_ref[...] = acc_ref[...] / jnp.where(left, l_ref[0], l_ref[1])


def nsa_flash(qbf, dup, btab, t, selmask=None):
    selected = selmask is not None
    tq = tk = btab.shape[-1]
    n = t // tq
    qs, ks, fs = [], [], []
    for qi in range(n):
        lo = 0 if selected else max(qi - (WINDOW // tk), 0)
        for ki in range(lo, qi + 1):
            qs.append(qi)
            ks.append(ki)
            fs.append(1 if ki == lo else 0)
    qt, kt, ft = (jnp.asarray(a, jnp.int32) for a in (qs, ks, fs))
    kcol = 0 if selected else 4
    half = NSA_REP // 2
    in_specs = [pl.BlockSpec((tq, LANES), lambda p, s, qt, kt, ft: (qt[s], p)),
                pl.BlockSpec((tk, LANES), lambda p, s, qt, kt, ft: (kt[s], kcol + p // half)),
                pl.BlockSpec((tk, LANES), lambda p, s, qt, kt, ft: (kt[s], kcol + 2 + p // half)),
                pl.BlockSpec((2, 1, tq, tk),
                             lambda p, s, qt, kt, ft: (p, jnp.minimum(qt[s] - kt[s], NSA_DELTAS - 1), 0, 0))]
    args = [qbf, dup, dup, btab]
    if selected:
        n_blk = tk // SEL_BLOCK
        jj = jnp.arange(LANES)[:, None]
        ll = jnp.arange(t)[None, :]
        expand = (jj == ll // SEL_BLOCK).astype(BF16)
        in_specs += [pl.BlockSpec((tq, LANES), lambda p, s, qt, kt, ft: (qt[s], p // half)),
                     pl.BlockSpec((LANES, tk), lambda p, s, qt, kt, ft: (0, kt[s]))]
        args += [selmask, expand]
    grid_spec = pltpu.PrefetchScalarGridSpec(
        num_scalar_prefetch=3,
        grid=(NSA_HEADS // 2, len(qs)),
        in_specs=in_specs,
        out_specs=pl.BlockSpec((tq, LANES), lambda p, s, qt, kt, ft: (qt[s], p)),
        scratch_shapes=[pltpu.VMEM((2, tq, 1), F32), pltpu.VMEM((2, tq, 1), F32), pltpu.VMEM((tq, LANES), F32)],
    )
    return pl.pallas_call(
        functools.partial(_nsa_flash_kernel, tq=tq, tk=tk, selected=selected),
        grid_spec=grid_spec,
        out_shape=jax.ShapeDtypeStruct((t, NSA_WIDTH), F32),
        compiler_params=_cparams(("parallel", "arbitrary")),
        name="nsa_sel_flash" if selected else "nsa_win_flash",
    )(qt, kt, ft, *args)


def nsa_prompt_attn(qbf, kv4, dup, wmat, cbias, rel_bias, btab, t):
    kvc = nsa_compress_rows(kv4[:t], wmat, cbias)
    o_c, selmask = nsa_cmp_select(qbf, kvc, rel_bias, t)
    o_s = nsa_flash(qbf, dup, btab, t, selmask)
    o_w = nsa_flash(qbf, dup, btab, t)
    return o_c, o_s, o_w


RW_CHUNK = 128
RW_PAIRS = RW_WIDTH // LANES


def _softplus(x):
    return jnp.maximum(x, 0.0) + jnp.log1p(jnp.exp(-jnp.abs(x)))


def _sigmoid(x):
    return 1.0 / (1.0 + jnp.exp(-x))


def _seg_sum_wide(x, ones_blk):
    return jnp.concatenate([_seg_sum(x[:, c * LANES:(c + 1) * LANES], ones_blk)
                            for c in range(x.shape[1] // LANES)], axis=1)


def _rwkv_features(p, shifted, mu, w0, w2p, a0, a2p, g2, k_k, k_a, ones_blk):
    z = p + (shifted - p) * mu
    r = z[:, 0:RW_WIDTH]
    k = z[:, RW_WIDTH:2 * RW_WIDTH]
    v = z[:, 2 * RW_WIDTH:3 * RW_WIDTH]
    lora = z[:, 3 * RW_WIDTH:3 * RW_WIDTH + LANES]
    gd = z[:, 3 * RW_WIDTH + LANES:]
    w_log = -_softplus(-(w0 + jnp.dot(jnp.tanh(lora).astype(BF16), w2p, preferred_element_type=F32))) - 0.5
    decay = jnp.exp(-jnp.exp(w_log))
    a = _sigmoid(a0 + jnp.dot(lora.astype(BF16), a2p, preferred_element_type=F32))
    g = jnp.dot(_sigmoid(gd).astype(BF16), g2, preferred_element_type=F32)
    kk = k * k_k
    kk = kk / jnp.maximum(jnp.sqrt(_seg_sum_wide(kk * kk, ones_blk)), 1e-12)
    kh = k * (1.0 + (a - 1.0) * k_a)
    return r, decay, kh, v, kk, kk * a, g


def _rwkv_finish(out, r, kh, v, g, r_k, gn_g, gn_b, ones_blk):
    m = _seg_sum_wide(out, ones_blk) * (1.0 / HEAD_DIM)
    d = out - m
    var = _seg_sum_wide(d * d, ones_blk) * (1.0 / HEAD_DIM)
    y = d * lax.rsqrt(var + RW_GN_EPS) * gn_g + gn_b
    bonus = _seg_sum_wide(r * kh * r_k, ones_blk) * v
    return (y + bonus) * g


def _rwkv_kernel(p_ref, mu_ref, w0_ref, w2_ref, a0_ref, a2_ref, g2_ref, kk_ref, ka_ref, rk_ref, gng_ref, gnb_ref,
                 spread_ref, y_ref, sout_ref, s_ref, prev_ref, rows_ref, vt_ref, ot_ref, vb_ref):
    n = RW_CHUNK
    c = pl.program_id(0)

    @pl.when(c == 0)
    def _():
        s_ref[...] = jnp.zeros_like(s_ref)
        prev_ref[...] = jnp.zeros_like(prev_ref)

    p = p_ref[...]
    first = lax.broadcasted_iota(jnp.int32, (n, 1), 0) == 0
    shifted = jnp.where(first, prev_ref[0:1, :], pltpu.roll(p, 1, axis=0))
    prev_ref[...] = jnp.broadcast_to(p[n - 1:n, :], prev_ref.shape)
    ones_blk = _head_ones()
    r, decay, kh, v, kk, kka, g = _rwkv_features(p, shifted, mu_ref[...], w0_ref[...], w2_ref[...], a0_ref[...],
                                                 a2_ref[...], g2_ref[...], kk_ref[...], ka_ref[...], ones_blk)
    for qi, arr in enumerate((r, decay, kh, kk, kka)):
        rows_ref[qi] = arr
    for pr in range(RW_PAIRS):
        vt_ref[pr] = v[:, pr * LANES:(pr + 1) * LANES].T
    ot_ref[...] = jnp.zeros_like(ot_ref)
    lane_t = lax.broadcasted_iota(jnp.int32, (HEAD_DIM, n), 1)
    left = lax.broadcasted_iota(jnp.int32, (1, LANES), 1) < HEAD_DIM
    half_sum = (lax.broadcasted_iota(jnp.int32, (LANES, 2 * LANES), 0) // HEAD_DIM
                == lax.broadcasted_iota(jnp.int32, (LANES, 2 * LANES), 1) // LANES).astype(BF16)

    def group(j, carry):
        base = pl.multiple_of(j * 8, 8)
        for pr in range(RW_PAIRS):
            vtr = pltpu.roll(vt_ref[pr], lax.rem(n - base, n), axis=1)
            pieces = _split3(vtr[0:HEAD_DIM]) + _split3(vtr[HEAD_DIM:])
            vb_ref[pr] = jnp.dot(jnp.concatenate(pieces, axis=1), spread_ref[...], preferred_element_type=F32)
        blk = [[rows_ref[q, pl.ds(base, 8), pr * LANES:(pr + 1) * LANES] for q in range(5)]
               for pr in range(RW_PAIRS)]
        for i in range(8):
            hit = lane_t == base + i
            for pr in range(RW_PAIRS):
                rv = lambda q: blk[pr][q][i:i + 1, :]
                s = s_ref[pr]
                m = s * rv(3)
                sa0 = jnp.sum(jnp.where(left, m, 0.0), axis=1, keepdims=True)
                sa1 = jnp.sum(jnp.where(left, 0.0, m), axis=1, keepdims=True)
                s = s * rv(1) - jnp.where(left, sa0, sa1) * rv(4) + vb_ref[pr, :, i * LANES:(i + 1) * LANES] * rv(2)
                s_ref[pr] = s
                oc = jnp.dot((s * rv(0)).astype(BF16), half_sum, preferred_element_type=F32)
                ot_ref[pr, 0:HEAD_DIM, :] = jnp.where(hit, oc[:, 0:LANES], ot_ref[pr, 0:HEAD_DIM, :])
                ot_ref[pr, HEAD_DIM:, :] = jnp.where(hit, oc[:, LANES:], ot_ref[pr, HEAD_DIM:, :])
        return carry

    lax.fori_loop(0, n // 8, group, 0)
    out = jnp.concatenate([ot_ref[pr].T for pr in range(RW_PAIRS)], axis=1)
    y_ref[...] = _rwkv_finish(out, r, kh, v, g, rk_ref[...], gng_ref[...], gnb_ref[...], ones_blk)

    @pl.when(c == pl.num_programs(0) - 1)
    def _():
        for pr in range(RW_PAIRS):
            sout_ref[2 * pr] = s_ref[pr][:, 0:HEAD_DIM]
            sout_ref[2 * pr + 1] = s_ref[pr][:, HEAD_DIM:]


def _rwkv_params(mu, w0, w2, a0, a2, g2, k_k, k_a, r_k, gn_g, gn_b):
    row = lambda a: a.reshape(1, -1).astype(F32)
    zeros = jnp.zeros((DECAY_LORA, RW_WIDTH), F32)
    w2p = jnp.concatenate([w2, zeros], axis=0).astype(BF16)
    a2p = jnp.concatenate([zeros, a2], axis=0).astype(BF16)
    return (row(mu), row(w0), w2p, row(a0), a2p, g2.astype(BF16), row(k_k), row(k_a), row(r_k), row(gn_g), row(gn_b))


def rwkv_prompt(p_rw, t, params):
    n = RW_CHUNK
    rr = jnp.arange(2 * 3 * LANES)[:, None]
    cc = jnp.arange(8 * LANES)[None, :]
    spread = ((rr % LANES == cc // LANES) & (rr // (3 * LANES) == (cc % LANES) // HEAD_DIM)).astype(BF16)
    args = _rwkv_params(*params) + (spread,)
    full = lambda a: pl.BlockSpec(a.shape, lambda i: (0,) * a.ndim)
    return pl.pallas_call(
        _rwkv_kernel,
        grid=(t // n,),
        in_specs=[pl.BlockSpec((n, RW_COLS), lambda i: (i, 0))] + [full(a) for a in args],
        out_specs=[pl.BlockSpec((n, RW_WIDTH), lambda i: (i, 0)),
                   pl.BlockSpec((RW_HEADS, HEAD_DIM, HEAD_DIM), lambda i: (0, 0, 0))],
        out_shape=[jax.ShapeDtypeStruct((t, RW_WIDTH), F32),
                   jax.ShapeDtypeStruct((RW_HEADS, HEAD_DIM, HEAD_DIM), F32)],
        scratch_shapes=[pltpu.VMEM((RW_PAIRS, HEAD_DIM, LANES), F32),
                        pltpu.VMEM((8, RW_COLS), F32),
                        pltpu.VMEM((5, n, RW_WIDTH), F32),
                        pltpu.VMEM((RW_PAIRS, LANES, n), F32),
                        pltpu.VMEM((RW_PAIRS, LANES, n), F32),
                        pltpu.VMEM((RW_PAIRS, HEAD_DIM, 8 * LANES), F32)],
        compiler_params=_cparams(("arbitrary",)),
        name="rwkv_prompt",
    )(p_rw, *args)


def _rwkv_sample_kernel(p_ref, sh_ref, s_in_ref, mu_ref, w0_ref, w2_ref, a0_ref, a2_ref, g2_ref, kk_ref, ka_ref,
                        rk_ref, gng_ref, gnb_ref, y_ref, s_out_ref, rows_ref, vt_ref, ot_ref):
    n = p_ref.shape[0]
    ones_blk = _head_ones()
    r, decay, kh, v, kk, kka, g = _rwkv_features(p_ref[...], sh_ref[...], mu_ref[...], w0_ref[...], w2_ref[...],
                                                 a0_ref[...], a2_ref[...], g2_ref[...], kk_ref[...], ka_ref[...],
                                                 ones_blk)
    for qi, arr in enumerate((r, decay, kh, kk, kka)):
        rows_ref[qi, 0] = arr
        rows_ref[qi, 1] = pltpu.roll(arr, RW_WIDTH - HEAD_DIM, axis=1)
    vt_ref[...] = jnp.zeros_like(vt_ref)
    ot_ref[...] = jnp.zeros_like(ot_ref)
    vpad = jnp.concatenate([v, jnp.zeros((LANES - n, RW_WIDTH), F32)], axis=0) if n < LANES else v
    for pr in range(RW_PAIRS):
        vt_ref[pr] = vpad[:, pr * LANES:(pr + 1) * LANES].T
    lane_t = lax.broadcasted_iota(jnp.int32, (HEAD_DIM, LANES), 1)

    def group(j, carry):
        base = pl.multiple_of(j * 8, 8)
        for h in range(RW_HEADS):
            pr, par = h // 2, h % 2
            lo, ro = pr * LANES, par * HEAD_DIM
            blk = [rows_ref[q, par, pl.ds(base, 8), lo:lo + HEAD_DIM] for q in range(5)]
            ot = ot_ref[pr, ro:ro + HEAD_DIM, :]
            vt = vt_ref[pr, ro:ro + HEAD_DIM, :]
            for i in range(8):
                hit = lane_t == base + i
                rv = lambda q: blk[q][i:i + 1, :]
                s = s_in_ref[base + i, h]
                sa = -jnp.sum(s * rv(3), axis=1, keepdims=True)
                vcol = jnp.sum(jnp.where(hit, vt, 0.0), axis=1, keepdims=True)
                s = s * rv(1) + sa * rv(4) + vcol * rv(2)
                s_out_ref[base + i, h] = s
                ocol = jnp.sum(s * rv(0), axis=1, keepdims=True)
                ot = jnp.where(hit, ocol, ot)
            ot_ref[pr, ro:ro + HEAD_DIM, :] = ot
        return carry

    lax.fori_loop(0, n // 8, group, 0)
    out = jnp.concatenate([ot_ref[pr].T for pr in range(RW_PAIRS)], axis=1)[0:n]
    y_ref[...] = _rwkv_finish(out, r, kh, v, g, rk_ref[...], gng_ref[...], gnb_ref[...], ones_blk)


def rwkv_sample(p_rw, shift, state, params):
    b = p_rw.shape[0]
    n = min(b, 32)
    args = _rwkv_params(*params)
    full = lambda a: pl.BlockSpec(a.shape, lambda i: (0,) * a.ndim)
    st = pl.BlockSpec((n, RW_HEADS, HEAD_DIM, HEAD_DIM), lambda i: (i, 0, 0, 0))
    return pl.pallas_call(
        _rwkv_sample_kernel,
        grid=(b // n,),
        in_specs=[pl.BlockSpec((n, RW_COLS), lambda i: (i, 0)), pl.BlockSpec((n, RW_COLS), lambda i: (i, 0)), st]
        + [full(a) for a in args],
        out_specs=[pl.BlockSpec((n, RW_WIDTH), lambda i: (i, 0)), st],
        out_shape=[jax.ShapeDtypeStruct((b, RW_WIDTH), F32), jax.ShapeDtypeStruct(state.shape, F32)],
        scratch_shapes=[pltpu.VMEM((5, 2, n, RW_WIDTH), F32),
                        pltpu.VMEM((RW_PAIRS, LANES, LANES), F32),
                        pltpu.VMEM((RW_PAIRS, LANES, LANES), F32)],
        compiler_params=_cparams(("parallel",)),
        name="rwkv_sample",
    )(p_rw, shift, state, *args)


def _t5_bucket_host(dist):
    d = jnp.maximum(dist, 0)
    large = N_BUCKETS // 2 + sum((d >= th).astype(jnp.int32) for th in T5_THRESH)
    return jnp.where(d < N_BUCKETS // 2, d, large)


def _fox_paged_kernel(pt_ref, q_ref, qt_ref, kn_ref, vnt_ref, lfn_ref, *refs, n_pages, page):
    kp, vp, lp = refs[0:n_pages], refs[n_pages:2 * n_pages], refs[2 * n_pages:3 * n_pages]
    o_ref, lg_ref = refs[3 * n_pages], refs[3 * n_pages + 1]
    own = (lax.broadcasted_iota(jnp.int32, (FOX_HEADS, LANES), 0)
           == lax.broadcasted_iota(jnp.int32, (FOX_HEADS, LANES), 1))
    lane_h = lax.broadcasted_iota(jnp.int32, (HEAD_DIM, LANES), 1)
    qt = qt_ref[0]
    qcol = [jnp.sum(jnp.where(lane_h == h, qt, 0.0), axis=1, keepdims=True) for h in range(FOX_HEADS)]
    later = (lax.broadcasted_iota(jnp.int32, (page, page), 0)
             > lax.broadcasted_iota(jnp.int32, (page, page), 1)).astype(BF16)

    for pg in range(n_pages):
        for h in range(FOX_HEADS):
            lg_ref[pg, h:h + 1, :] = jnp.sum(kp[pg][0, 0, h] * qcol[h], axis=0, keepdims=True)
    after = jnp.sum(jnp.where(own, lfn_ref[0], 0.0), axis=1, keepdims=True)
    logits = [None] * n_pages
    for pg in reversed(range(n_pages)):
        lft = lp[pg][0, 0]
        within = sum(jnp.dot(piece, later, preferred_element_type=F32) for piece in _split3(lft))
        logits[pg] = lg_ref[pg] + within + after
        after = after + jnp.sum(lft, axis=1, keepdims=True)
    l_new = jnp.sum(q_ref[0] * kn_ref[0].astype(BF16).astype(F32), axis=1, keepdims=True)
    m = l_new
    for lg in logits:
        m = jnp.maximum(m, jnp.max(lg, axis=1, keepdims=True))
    p_new = jnp.exp(l_new - m)
    den = p_new
    for pg in range(n_pages):
        p = jnp.exp(logits[pg] - m)
        den = den + jnp.sum(p, axis=1, keepdims=True)
        lg_ref[pg] = p
    ot = jnp.zeros((HEAD_DIM, LANES), F32)
    for h in range(FOX_HEADS):
        acc = jnp.zeros((HEAD_DIM, page), F32)
        for pg in range(n_pages):
            acc = acc + lg_ref[pg, h:h + 1, :] * vp[pg][0, 0, h]
        ot = jnp.where(lane_h == h, jnp.sum(acc, axis=1, keepdims=True), ot)
    as_row = lambda col: jnp.sum(jnp.where(own, col, 0.0), axis=0, keepdims=True)
    vnt = vnt_ref[0].astype(BF16).astype(F32)
    o_ref[0] = (ot + vnt * as_row(p_new)) / jnp.maximum(as_row(den), 1e-30)


def fox_paged(pt, layer, qbf, k_new, v_new, lf_new, cache_k, cache_v, cache_lf):
    b, n_pages = pt.shape
    page = cache_k.shape[2]
    kt = jnp.transpose(cache_k, (0, 1, 3, 4, 2))
    vt = jnp.transpose(cache_v, (0, 1, 3, 4, 2))
    lft = jnp.transpose(cache_lf, (0, 1, 3, 2))
    q3 = qbf.astype(F32).reshape(b, FOX_HEADS, HEAD_DIM)
    lanes_t = lambda a: jnp.pad(jnp.swapaxes(a, 1, 2), ((0, 0), (0, 0), (0, LANES - FOX_HEADS)))
    per_seq = lambda *shape: pl.BlockSpec((1,) + shape, lambda i, pt: (i,) + (0,) * len(shape))

    def paged(tail, j):
        return pl.BlockSpec((1, 1) + tail, lambda i, pt, j=j: (layer, pt[i, j]) + (0,) * len(tail))

    in_specs = ([per_seq(FOX_HEADS, HEAD_DIM), per_seq(HEAD_DIM, LANES), per_seq(FOX_HEADS, HEAD_DIM),
                 per_seq(HEAD_DIM, LANES), per_seq(1, LANES)]
                + [paged((FOX_HEADS, HEAD_DIM, page), j) for j in range(n_pages)]
                + [paged((FOX_HEADS, HEAD_DIM, page), j) for j in range(n_pages)]
                + [paged((FOX_HEADS, page), j) for j in range(n_pages)])
    grid_spec = pltpu.PrefetchScalarGridSpec(
        num_scalar_prefetch=1, grid=(b,), in_specs=in_specs, out_specs=per_seq(HEAD_DIM, LANES),
        scratch_shapes=[pltpu.VMEM((n_pages, FOX_HEADS, page), F32)])
    out = pl.pallas_call(
        functools.partial(_fox_paged_kernel, n_pages=n_pages, page=page),
        grid_spec=grid_spec,
        out_shape=jax.ShapeDtypeStruct((b, HEAD_DIM, LANES), F32),
        compiler_params=_cparams(("parallel",)),
        name="fox_paged",
    )(pt, q3, lanes_t(q3), k_new.reshape(b, FOX_HEADS, HEAD_DIM), lanes_t(v_new.reshape(b, FOX_HEADS, HEAD_DIM)),
      lf_new.reshape(b, 1, LANES), *([kt] * n_pages), *([vt] * n_pages), *([lft] * n_pages))
    return jnp.swapaxes(out[:, :, :FOX_HEADS], 1, 2).reshape(b, FOX_WIDTH)


def _masked_softmax_rows(parts, valids):
    m = jnp.full((parts[0].shape[0], 1), NEG, F32)
    for lg, ok in zip(parts, valids):
        m = jnp.maximum(m, jnp.max(jnp.where(ok, lg, NEG), axis=1, keepdims=True))
    es = [jnp.where(ok, jnp.exp(jnp.where(ok, lg, NEG) - m), 0.0) for lg, ok in zip(parts, valids)]
    den = sum(jnp.sum(e, axis=1, keepdims=True) for e in es)
    return es, jnp.maximum(den, 1e-30)


def _wg_mask(n_rows, n_w, w_pick):
    col = lax.broadcasted_iota(jnp.int32, (NSA_HEADS, n_rows), 1)
    head = lax.broadcasted_iota(jnp.int32, (NSA_HEADS, n_rows), 0)
    return ((col // NSA_KV_HEADS) % n_w == w_pick) & (col % NSA_KV_HEADS == head // NSA_REP)


def _nsa_decode_kernel(pt_ref, q_ref, new_ref, newcol_ref, win_ref, wcat_ref, cb_ref, bc_ref, bs_ref, bw_ref, b0_ref,
                       ex_ref, *refs, n_pages, page):
    pages = refs[0:n_pages]
    o_ref, wout_ref, t_ref = refs[n_pages], refs[n_pages + 1], refs[n_pages + 2]
    nt = (((1,), (1,)), ((), ()))
    q = q_ref[0]
    b0 = b0_ref[:, 0:1]
    past = n_pages * page
    ncr = past // CMP_STRIDE
    g2 = NSA_KV_HEADS
    row8 = lax.broadcasted_iota(jnp.int32, (NSA_HEADS, 1), 0)
    grp0 = row8 < NSA_REP

    def qk(tile0, tile1):
        return jnp.where(grp0, jnp.dot(q, tile0.astype(BF16), preferred_element_type=F32),
                         jnp.dot(q, tile1.astype(BF16), preferred_element_type=F32))

    def pv(p, tile0, tile1):
        pb = p.astype(BF16)
        return jnp.where(grp0, lax.dot_general(pb, tile0.astype(BF16), nt, preferred_element_type=F32),
                         lax.dot_general(pb, tile1.astype(BF16), nt, preferred_element_type=F32))

    comp = []
    for w in range(2):
        for pg in range(n_pages):
            t_ref[w, pg * page:(pg + 1) * page, :] = pages[pg][0, 0, w].reshape(g2 * HEAD_DIM, page).T
        acc = jnp.zeros((ncr, 2 * LANES), F32)
        for j in range(CMP_STRIDE):
            rows = t_ref[w, pl.ds(j, ncr, stride=CMP_STRIDE), :]
            acc = acc + jnp.dot(rows.astype(BF16), wcat_ref[w, j], preferred_element_type=F32)
        comp.append((acc[:, 0:LANES] + pltpu.roll(acc[:, LANES:], ncr - 1, axis=0) + cb_ref[w:w + 1, :]).astype(BF16))
    kc, vc = comp

    qf = q.astype(F32)
    lane128 = lax.broadcasted_iota(jnp.int32, (NSA_HEADS, LANES), 1)
    own_half = (lane128 // HEAD_DIM == 0) == grp0
    qg = jnp.where(own_half, jnp.concatenate([qf, qf], axis=1), 0.0).astype(BF16)
    lane_c = lax.broadcasted_iota(jnp.int32, (NSA_HEADS, ncr), 1)
    lc = lax.dot_general(qg, kc, nt, preferred_element_type=F32) + bc_ref[...]
    (e_c,), den_c = _masked_softmax_rows([lc], [lane_c < ncr - 1])
    p_c = (e_c / den_c).astype(BF16)
    oc2 = jnp.dot(p_c, vc, preferred_element_type=F32)
    o_ref[0, 0] = jnp.where(grp0, oc2[:, 0:HEAD_DIM], oc2[:, HEAD_DIM:])

    ci = lax.broadcasted_iota(jnp.int32, (ncr, LANES), 0) * CMP_STRIDE
    cj = lax.broadcasted_iota(jnp.int32, (ncr, LANES), 1)
    cover = jnp.where((ci < (cj + 1) * SEL_BLOCK) & (ci + CMP_BLOCK > cj * SEL_BLOCK), 1.0, 0.0).astype(BF16)
    imp8 = jnp.dot(p_c, cover, preferred_element_type=F32)
    imp = jnp.where(grp0, jnp.sum(imp8[0:NSA_REP], axis=0, keepdims=True),
                    jnp.sum(imp8[NSA_REP:], axis=0, keepdims=True))
    lane_f = lane128.astype(F32)
    jq = past // SEL_BLOCK
    forced = (lane128 == 0) | (lane128 == jq) | (lane128 == jq - 1)
    work = jnp.where(lane128 <= jq, imp + jnp.where(forced, SEL_FORCE, 0.0), NOT_ALLOWED)
    sel = jnp.zeros((NSA_HEADS, LANES), F32)
    for _ in range(min(N_SELECT, jq + 1)):
        mx = jnp.max(work, axis=-1, keepdims=True)
        idx = jnp.min(jnp.where(work == mx, lane_f, float(LANES)), axis=-1, keepdims=True)
        hit = lane_f == idx
        sel = jnp.where(hit & (mx > 0.5 * NOT_ALLOWED), 1.0, sel)
        work = jnp.where(hit, REMOVED, work)
    sel_bf = sel.astype(BF16)

    parts, oks = [], []
    for pg in range(n_pages):
        parts.append(qk(pages[pg][0, 0, 2, 0], pages[pg][0, 0, 2, 1]) + bs_ref[:, pg * page:(pg + 1) * page])
        oks.append(jnp.dot(sel_bf, ex_ref[:, pg * page:(pg + 1) * page], preferred_element_type=F32) > 0.5)
    new_rows = new_ref[0].reshape(8 * g2, HEAD_DIM).astype(BF16)
    mnk, mnv = _wg_mask(8 * g2, 8, 2), _wg_mask(8 * g2, 8, 3)
    parts.append(jnp.sum(jnp.where(mnk, lax.dot_general(q, new_rows, nt, preferred_element_type=F32), 0.0),
                         axis=1, keepdims=True) + b0)
    oks.append(sel[:, jq:jq + 1] > 0.5)
    es, den = _masked_softmax_rows(parts, oks)
    o_s = jnp.dot(jnp.where(mnv, es[n_pages], 0.0).astype(BF16), new_rows, preferred_element_type=F32)
    for pg in range(n_pages):
        o_s = o_s + pv(es[pg], pages[pg][0, 0, 3, 0], pages[pg][0, 0, 3, 1])
    o_ref[0, 1] = o_s / den

    wb = win_ref.shape[-1]
    mnk, mnv = _wg_mask(8 * g2, 8, 4), _wg_mask(8 * g2, 8, 5)
    lane_w = lax.broadcasted_iota(jnp.int32, (NSA_HEADS, wb), 1)
    parts = [qk(win_ref[0, 0, 0, 0], win_ref[0, 0, 0, 1]) + bw_ref[...],
             jnp.sum(jnp.where(mnk, lax.dot_general(q, new_rows, nt, preferred_element_type=F32), 0.0),
                     axis=1, keepdims=True) + b0]
    oks = [wb - lane_w < WINDOW, jnp.full((NSA_HEADS, 1), True)]
    es, den = _masked_softmax_rows(parts, oks)
    o_w = (pv(es[0], win_ref[0, 0, 1, 0], win_ref[0, 0, 1, 1])
           + jnp.dot(jnp.where(mnv, es[1], 0.0).astype(BF16), new_rows, preferred_element_type=F32))
    o_ref[0, 2] = o_w / den

    wrows = win_ref[0, 0].reshape(2 * g2 * HEAD_DIM, wb)
    last = lax.broadcasted_iota(jnp.int32, (1, wb), 1) == wb - 1
    wout_ref[0, 0] = jnp.where(last, newcol_ref[0], pltpu.roll(wrows, wb - 1, axis=1)).reshape(2, g2, HEAD_DIM, wb)


def nsa_decode(pt, layer, q8, new_rows, kwv_new, win_all, cache, cmp_w, cmp_b, rel_bias):
    b, n_pages = pt.shape
    page = cache.shape[2]
    past = n_pages * page
    wb = win_all.shape[2]
    ncr = past // CMP_STRIDE
    cache_t = jnp.transpose(cache, (0, 1, 3, 4, 5, 2))
    win_t = jnp.transpose(win_all, (0, 1, 3, 4, 5, 2))
    tab = lambda dist: rel_bias[_t5_bucket_host(dist)].T.astype(F32)
    bc = tab(past - (jnp.arange(ncr) * CMP_STRIDE + CMP_BLOCK - 1))
    bs = tab(past - jnp.arange(past))
    bw = tab(wb - jnp.arange(wb))
    b0 = jnp.broadcast_to(rel_bias[0][:, None], (NSA_HEADS, LANES)).astype(F32)
    expand = (jnp.arange(LANES)[:, None] == jnp.arange(past)[None, :] // SEL_BLOCK).astype(BF16)
    w5 = cmp_w.reshape(2, 2, CMP_STRIDE, HEAD_DIM, HEAD_DIM)
    wcat = jnp.einsum('whjde,gk->wjgdhke', w5, jnp.eye(NSA_KV_HEADS, dtype=F32))
    wcat = wcat.reshape(2, CMP_STRIDE, LANES, 2 * LANES).astype(BF16)
    cb = jnp.tile(cmp_b.astype(F32), (1, NSA_KV_HEADS))
    consts = (wcat, cb, bc, bs, bw, b0, expand)
    per_seq = lambda *shape: pl.BlockSpec((1,) + shape, lambda i, pt: (i,) + (0,) * len(shape))
    full = lambda a: pl.BlockSpec(a.shape, lambda i, pt: (0,) * a.ndim)
    wblk = pl.BlockSpec((1, 1, 2, NSA_KV_HEADS, HEAD_DIM, wb), lambda i, pt: (layer, i, 0, 0, 0, 0))
    in_specs = ([per_seq(NSA_HEADS, HEAD_DIM), per_seq(8, NSA_KV_HEADS, HEAD_DIM), per_seq(256, 1), wblk]
                + [full(a) for a in consts]
                + [pl.BlockSpec((1, 1, 4, NSA_KV_HEADS, HEAD_DIM, page),
                                lambda i, pt, j=j: (layer, pt[i, j], 0, 0, 0, 0)) for j in range(n_pages)])
    grid_spec = pltpu.PrefetchScalarGridSpec(
        num_scalar_prefetch=1, grid=(b,), in_specs=in_specs,
        out_specs=[per_seq(3, NSA_HEADS, HEAD_DIM),
                   pl.BlockSpec((1, 1, 2, NSA_KV_HEADS, HEAD_DIM, wb), lambda i, pt: (i, 0, 0, 0, 0, 0))],
        scratch_shapes=[pltpu.VMEM((2, past, LANES), F32)])
    o3, wout = pl.pallas_call(
        functools.partial(_nsa_decode_kernel, n_pages=n_pages, page=page),
        grid_spec=grid_spec,
        out_shape=[jax.ShapeDtypeStruct((b, 3, NSA_HEADS, HEAD_DIM), F32),
                   jax.ShapeDtypeStruct((b, 1, 2, NSA_KV_HEADS, HEAD_DIM, wb), F32)],
        compiler_params=_cparams(("parallel",)),
        name="nsa_decode",
    )(pt, q8, new_rows, kwv_new.reshape(b, 256, 1), win_t, *consts, *([cache_t] * n_pages))
    return o3, jnp.transpose(wout[:, 0], (0, 4, 1, 2, 3))


def kernel(x_prompt, x_sample, cache_nsa_kv, cache_fox_k, cache_fox_v, cache_fox_logf, state_nsa_win, state_rwkv_wkv, state_rwkv_shift, page_table, ln_g, ln_b, ffn_w_gate, ffn_w_up, ffn_w_down, w_in, w_out, rw_mu, rw_w0, rw_w2, rw_a0, rw_a2, rw_g2, rw_k_k, rw_k_a, rw_r_k, rw_gn_g, rw_gn_b, nsa_cmp_w, nsa_cmp_b, nsa_out_g, rel_bias, fox_b_f, fox_out_g):
    db = x_sample.shape[0]
    t = x_prompt.shape[1]

    wg_bf = ffn_w_gate.astype(BF16)
    wu_bf = ffn_w_up.astype(BF16)
    wd_bf = ffn_w_down.astype(BF16)
    wout_bf = w_out.astype(BF16)
    w_rw = w_in[:, :, :RW_COLS].astype(BF16)
    w_nsa = [nsa_proj_weight(w_in[l, :, RW_COLS:RW_COLS + NSA_COLS]) for l in range(DEPTH)]
    w_fox = jnp.pad(w_in[:, :, RW_COLS + NSA_COLS:].astype(BF16), ((0, 0), (0, 0), (0, 3200 - FOX_COLS)))

    def ffn_sub(x, x_bf, l, j):
        h = ffn_in(x_bf, wg_bf[l, j], wu_bf[l, j])
        return ffn_out(h, wd_bf[l, j], x, ln_g[l, 2 * j], ln_b[l, 2 * j])

    def rw_params(l):
        return (rw_mu[l], rw_w0[l], rw_w2[l], rw_a0[l], rw_a2[l], rw_g2[l], rw_k_k[l], rw_k_a[l],
                rw_r_k[l], rw_gn_g[l], rw_gn_b[l])

    xp = x_prompt[0]
    xp_bf = xp.astype(BF16)
    p_kv, p_fk, p_fv, p_fl, p_win, p_wkv, p_shift = [], [], [], [], [], [], []
    btab = bias_tiles(rel_bias, NSA_TILE)
    wb = min(WINDOW, t)
    for l in range(DEPTH):
        xp, xp_bf = ffn_sub(xp, xp_bf, l, 0)
        p_rw = proj(xp_bf, w_rw[l])
        nq, kv4, kwv, gl, dup = nsa_proj(xp_bf, w_nsa[l])
        qbf, fk, fv, kbf, vbf, lf128 = fox_proj(xp_bf, w_fox[l], fox_b_f[l])
        o_rw, s_wkv = rwkv_prompt(p_rw, t, rw_params(l))
        wmat, cbias = nsa_compress_weight(nsa_cmp_w[l], nsa_cmp_b[l])
        o_c, o_s, o_w = nsa_prompt_attn(nq, kv4, dup, wmat, cbias, rel_bias, btab, t)
        o_fox = fox_prompt_attn(qbf, kbf, vbf, lf128, t)
        xp, xp_bf = mix_out(o_rw, o_c, o_s, o_w, gl, o_fox, xp, wout_bf[l], nsa_out_g[l], fox_out_g[l],
                            ln_g[l, 1], ln_b[l, 1])
        xp, xp_bf = ffn_sub(xp, xp_bf, l, 1)
        p_kv.append(kv4.reshape(1, t, 4, NSA_KV_HEADS, HEAD_DIM))
        p_fk.append(fk.reshape(1, t, FOX_HEADS, HEAD_DIM))
        p_fv.append(fv.reshape(1, t, FOX_HEADS, HEAD_DIM))
        p_fl.append(lf128[None, :, :FOX_HEADS])
        p_win.append(kwv[t - wb:].reshape(1, wb, 2, NSA_KV_HEADS, HEAD_DIM))
        p_wkv.append(s_wkv[None])
        p_shift.append(p_rw[t - 1:t])

    xs = x_sample[:, 0]
    xs_bf = xs.astype(BF16)
    s_kv, s_fk, s_fv, s_fl, s_win, s_wkv_l, s_shift_l = [], [], [], [], [], [], []
    for l in range(DEPTH):
        xs, xs_bf = ffn_sub(xs, xs_bf, l, 0)
        p_rw = proj(xs_bf, w_rw[l])
        nq, kv_new, kwv_new, gl, _ = nsa_proj(xs_bf, w_nsa[l])
        qbf, fk, fv, kbf, vbf, lf128 = fox_proj(xs_bf, w_fox[l], fox_b_f[l])
        o_rw, s_wkv = rwkv_sample(p_rw, state_rwkv_shift[l], state_rwkv_wkv[l], rw_params(l))
        new_rows = jnp.concatenate([kv_new.reshape(db, 4, NSA_KV_HEADS, HEAD_DIM),
                                    kwv_new.reshape(db, 2, NSA_KV_HEADS, HEAD_DIM),
                                    jnp.zeros((db, 2, NSA_KV_HEADS, HEAD_DIM), F32)], axis=1)
        o3, win = nsa_decode(page_table, l, nq.reshape(db, NSA_HEADS, HEAD_DIM), new_rows, kwv_new, state_nsa_win,
                             cache_nsa_kv, nsa_cmp_w[l], nsa_cmp_b[l], rel_bias)
        o_c, o_s, o_w = (o3[:, br].reshape(db, NSA_WIDTH) for br in range(3))
        o_fox = fox_paged(page_table, l, qbf, fk, fv, lf128, cache_fox_k, cache_fox_v, cache_fox_logf)
        xs, xs_bf = mix_out(o_rw, o_c, o_s, o_w, gl, o_fox, xs, wout_bf[l],
                            nsa_out_g[l], fox_out_g[l], ln_g[l, 1], ln_b[l, 1])
        xs, xs_bf = ffn_sub(xs, xs_bf, l, 1)
        s_kv.append(kv_new.reshape(db, 1, 4, NSA_KV_HEADS, HEAD_DIM))
        s_fk.append(fk.reshape(db, 1, FOX_HEADS, HEAD_DIM))
        s_fv.append(fv.reshape(db, 1, FOX_HEADS, HEAD_DIM))
        s_fl.append(lf128[:, None, :FOX_HEADS])
        s_win.append(win)
        s_wkv_l.append(s_wkv)
        s_shift_l.append(p_rw)

    return (xp[None], xs[:, None],
            jnp.stack(p_kv), jnp.stack(p_fk), jnp.stack(p_fv), jnp.stack(p_fl),
            jnp.stack(p_win), jnp.stack(p_wkv), jnp.stack(p_shift),
            jnp.stack(s_kv), jnp.stack(s_fk), jnp.stack(s_fv), jnp.stack(s_fl),
            jnp.stack(s_win), jnp.stack(s_wkv_l), jnp.stack(s_shift_l))
```
